```python
import math
import jax
import jax.numpy as jnp
from jax import lax
import numpy as np


D_MODEL = 2048
BATCH = 4
SEQ = 8192
DEPTH = 1

M_HEADS = 4
M_DQK = 128
M_DV = 256
M_CHUNK = 128
GATE_CAP = 15.0
A_HEADS = 8
A_DH = 64
A_DV = 2 * A_DH
Q_BLOCK = 128
ROPE_THETA = 10000.0
D_FF = -(-8 * D_MODEL // (3 * 256)) * 256
N_BRANCH = 2
EPS = 1e-6
M_WIDTH = M_HEADS * M_DV
A_WIDTH = A_HEADS * A_DV
SPLIT_SIZES = (M_HEADS * M_DQK, M_HEADS * M_DQK, M_WIDTH, M_WIDTH, 4 * M_HEADS,
               A_HEADS * 2 * A_DH, A_HEADS * 2 * A_DH, A_WIDTH, N_BRANCH * D_MODEL)
D_IN = sum(SPLIT_SIZES)
IN_OFFSETS = tuple(int(o) for o in np.cumsum(SPLIT_SIZES)[:-1])

kernel_name = 'hybrid_mlstm_diffattn_encoder'


def _rmsnorm(x, w):
    xf = x.astype(jnp.float32)
    y = xf * lax.rsqrt(jnp.mean(xf * xf, axis=-1, keepdims=True) + EPS)
    return (y * w.astype(jnp.float32)).astype(x.dtype)


def _softcap(t):
    return GATE_CAP * jnp.tanh(t / GATE_CAP)


def _rope(t):
    S = t.shape[1]
    d = t.shape[-1]
    inv = ROPE_THETA ** (-jnp.arange(0, d, 2, dtype=jnp.float32) / d)
    ang = jnp.arange(S, dtype=jnp.float32)[:, None] * inv[None, :]
    cos = jnp.cos(ang)[:, None, None, :]
    sin = jnp.sin(ang)[:, None, None, :]
    tf = t.astype(jnp.float32)
    t1, t2 = tf[..., : d // 2], tf[..., d // 2:]
    return jnp.concatenate([t1 * cos - t2 * sin, t1 * sin + t2 * cos], axis=-1).astype(t.dtype)


def _mlstm_scan(q, k, v, ig, lf):
    B, H, S, Dk = q.shape
    Dv = v.shape[-1]
    L = M_CHUNK
    nc = S // L

    def to_chunks(t):
        return jnp.moveaxis(t.astype(jnp.float32).reshape((B, H, nc, L) + t.shape[3:]), 2, 0)

    xs = tuple(to_chunks(t) for t in (q, k, v, ig, lf))
    tril = jnp.tril(jnp.ones((L, L), dtype=bool))

    def step(carry, inp):
        C, n, m = carry
        qc, kc, vc, igc, lfc = inp
        b = jnp.cumsum(lfc, axis=-1)
        dmat = b[..., :, None] - b[..., None, :] + igc[..., None, :]
        dmat = jnp.where(tril, dmat, -jnp.inf)
        inter = b + m[..., None]
        m_t = jnp.maximum(inter, jnp.max(dmat, axis=-1))
        wts = jnp.exp(dmat - m_t[..., None])
        a = jnp.exp(inter - m_t)
        s = jnp.einsum('bhtd,bhsd->bhts', qc, kc) * wts
        num = a[..., None] * jnp.einsum('bhvd,bhtd->bhtv', C, qc) + jnp.einsum('bhts,bhsv->bhtv', s, vc)
        den = a * jnp.einsum('bhd,bhtd->bht', n, qc) + jnp.sum(s, axis=-1)
        h = num / jnp.maximum(jnp.abs(den), jnp.exp(-m_t))[..., None]
        bL = b[..., -1]
        g = bL[..., None] - b + igc
        m_new = jnp.maximum(bL + m, jnp.max(g, axis=-1))
        decay = jnp.exp(bL + m - m_new)
        wk = jnp.exp(g - m_new[..., None])
        C = decay[..., None, None] * C + jnp.einsum('bhsv,bhsd->bhvd', vc, kc * wk[..., None])
        n = decay[..., None] * n + jnp.einsum('bhs,bhsd->bhd', wk, kc)
        return (C, n, m_new), h

    init = (jnp.zeros((B, H, Dv, Dk), jnp.float32),
            jnp.zeros((B, H, Dk), jnp.float32),
            jnp.zeros((B, H), jnp.float32))
    _, hs = lax.scan(step, init, xs)
    return jnp.moveaxis(hs, 0, 2).reshape(B, H, S, Dv)


def _diff_attention(q, k, v, lam):
    B, S, H, _, dh = q.shape
    nb = S // Q_BLOCK
    qb = jnp.moveaxis(q.reshape(B, nb, Q_BLOCK, H, 2, dh), 1, 0)
    scale = dh ** -0.5

    def block(qi):
        s = jnp.einsum('bqhcd,bkhcd->bhcqk', qi, k).astype(jnp.float32) * scale
        p = jax.nn.softmax(s, axis=-1)
        p = p[:, :, 0] - lam * p[:, :, 1]
        return jnp.einsum('bhqk,bkhv->bqhv', p.astype(v.dtype), v)

    o = lax.map(block, qb)
    return jnp.moveaxis(o, 0, 1).reshape(B, S, H, v.shape[-1])


def setup_inputs(seed: int = 0) -> dict:
    key = jax.random.key(seed)
    ks = jax.random.split(key, 20)
    f32 = jnp.float32

    def w(k, shape, fan_in):
        return jax.random.normal(k, shape, f32) * fan_in ** -0.5

    def gain(k, shape):
        return 1.0 + 0.02 * jax.random.normal(k, shape, f32)

    x = jax.random.normal(ks[0], (BATCH, SEQ, D_MODEL), f32)
    norm1_w = gain(ks[1], (DEPTH, D_MODEL))
    w_in = w(ks[2], (DEPTH, D_MODEL, D_IN), D_MODEL)
    b_igate = 0.1 * jax.random.normal(ks[3], (DEPTH, 2, M_HEADS), f32)
    b_fgate = jnp.linspace(3.0, 6.0, M_HEADS, dtype=f32)[None, None, :] + 0.1 * jax.random.normal(ks[4], (DEPTH, 2, M_HEADS), f32)
    b_branch_gate = 0.1 * jax.random.normal(ks[5], (DEPTH, N_BRANCH * D_MODEL), f32)
    mlstm_norm_w = gain(ks[6], (DEPTH, M_WIDTH))
    lam_q1 = 0.1 * jax.random.normal(ks[7], (DEPTH, A_DH), f32)
    lam_k1 = 0.1 * jax.random.normal(ks[8], (DEPTH, A_DH), f32)
    lam_q2 = 0.1 * jax.random.normal(ks[9], (DEPTH, A_DH), f32)
    lam_k2 = 0.1 * jax.random.normal(ks[10], (DEPTH, A_DH), f32)
    attn_norm_w = gain(ks[11], (DEPTH, A_DV))
    w_branch_m = w(ks[12], (DEPTH, M_WIDTH, D_MODEL), M_WIDTH)
    w_branch_a = w(ks[13], (DEPTH, A_WIDTH, D_MODEL), A_WIDTH)
    w_out = w(ks[14], (DEPTH, D_MODEL, D_MODEL), D_MODEL)
    norm2_w = gain(ks[15], (DEPTH, D_MODEL))
    w_ffn_in = w(ks[16], (DEPTH, D_MODEL, 2 * D_FF), D_MODEL)
    w_ffn_out = w(ks[17], (DEPTH, D_FF, D_MODEL), D_FF)
    final_norm_w = gain(ks[18], (D_MODEL,))
    return {'x': x, 'norm1_w': norm1_w, 'w_in': w_in, 'b_igate': b_igate, 'b_fgate': b_fgate,
            'b_branch_gate': b_branch_gate, 'mlstm_norm_w': mlstm_norm_w,
            'lam_q1': lam_q1, 'lam_k1': lam_k1, 'lam_q2': lam_q2, 'lam_k2': lam_k2,
            'attn_norm_w': attn_norm_w, 'w_branch_m': w_branch_m, 'w_branch_a': w_branch_a,
            'w_out': w_out, 'norm2_w': norm2_w, 'w_ffn_in': w_ffn_in, 'w_ffn_out': w_ffn_out,
            'final_norm_w': final_norm_w}


def reference(x, norm1_w, w_in, b_igate, b_fgate, b_branch_gate, mlstm_norm_w,
              lam_q1, lam_k1, lam_q2, lam_k2, attn_norm_w, w_branch_m, w_branch_a,
              w_out, norm2_w, w_ffn_in, w_ffn_out, final_norm_w):
    B, S, D = x.shape
    for l in range(DEPTH):
        h = _rmsnorm(x, norm1_w[l])
        proj = h @ w_in[l]
        mq, mk, mv, mo, mg, aq, ak, av, gt = jnp.split(proj, IN_OFFSETS, axis=-1)

        q = mq.reshape(B, S, M_HEADS, M_DQK).transpose(0, 2, 1, 3)
        k = mk.reshape(B, S, M_HEADS, M_DQK).transpose(0, 2, 1, 3) * (M_DQK ** -0.5)
        v = mv.reshape(B, S, M_HEADS, M_DV).transpose(0, 2, 1, 3)
        g = mg.astype(jnp.float32).reshape(B, S, 2, 2, M_HEADS)
        ig = _softcap(g[:, :, :, 0] + b_igate[l].astype(jnp.float32))
        lf = jax.nn.log_sigmoid(_softcap(g[:, :, :, 1] + b_fgate[l].astype(jnp.float32)))
        ig = ig.transpose(2, 0, 3, 1)
        lf = lf.transpose(2, 0, 3, 1)
        h_fwd = _mlstm_scan(q, k, v, ig[0], lf[0])
        h_bwd = jnp.flip(_mlstm_scan(jnp.flip(q, 2), jnp.flip(k, 2), jnp.flip(v, 2),
                                     jnp.flip(ig[1], 2), jnp.flip(lf[1], 2)), 2)
        hm = (h_fwd + h_bwd).astype(x.dtype).transpose(0, 2, 1, 3)
        hm = _rmsnorm(hm, mlstm_norm_w[l].reshape(M_HEADS, M_DV))
        hm = hm * jax.nn.sigmoid(mo).reshape(B, S, M_HEADS, M_DV)
        branch_m = hm.reshape(B, S, M_WIDTH) @ w_branch_m[l]

        lam_init = 0.8 - 0.6 * math.exp(-0.3 * l)
        lam = (jnp.exp(jnp.dot(lam_q1[l].astype(jnp.float32), lam_k1[l].astype(jnp.float32)))
               - jnp.exp(jnp.dot(lam_q2[l].astype(jnp.float32), lam_k2[l].astype(jnp.float32)))
               + lam_init)
        qa = _rope(aq.reshape(B, S, A_HEADS, 2, A_DH))
        ka = _rope(ak.reshape(B, S, A_HEADS, 2, A_DH))
        va = av.reshape(B, S, A_HEADS, A_DV)
        ha = _diff_attention(qa, ka, va, lam)
        ha = _rmsnorm(ha, attn_norm_w[l]) * (1.0 - lam_init)
        branch_a = ha.reshape(B, S, A_WIDTH) @ w_branch_a[l]

        gates = jax.nn.sigmoid(gt + b_branch_gate[l])
        g_m, g_a = jnp.split(gates, N_BRANCH, axis=-1)
        x = x + (g_m * branch_m + g_a * branch_a) @ w_out[l]

        h2 = _rmsnorm(x, norm2_w[l])
        gate, up = jnp.split(h2 @ w_ffn_in[l], 2, axis=-1)
        x = x + (jax.nn.silu(gate) * up) @ w_ffn_out[l]
    return _rmsnorm(x, final_norm_w)
```

```python
import functools
import math

import jax
import jax.numpy as jnp
from jax import lax
from jax.experimental import pallas as pl
from jax.experimental.pallas import tpu as pltpu

F32 = jnp.float32
BF16 = jnp.bfloat16

D_MODEL = 2048
M_HEADS = 4
M_DQK = 128
M_DV = 256
M_CHUNK = 128
GATE_CAP = 15.0
A_HEADS = 8
A_DH = 64
A_DV = 2 * A_DH
ROPE_THETA = 10000.0
D_FF = 5632
EPS = 1e-6
M_WIDTH = M_HEADS * M_DV
A_WIDTH = A_HEADS * A_DV
LAM_INIT = 0.8 - 0.6 * math.exp(-0.3 * 0)

N_MAIN = 10240
GATE_OFF = 3072
N_GATE = 4 * M_HEADS
LANES = 128

VMEM_LIMIT = 56 * 1024 * 1024


def _cparams(sem):
    return pltpu.CompilerParams(dimension_semantics=sem, vmem_limit_bytes=VMEM_LIMIT)


def _dot(a, b):
    return jnp.dot(a, b, preferred_element_type=F32)


def _dot_nt(a, b):
    return lax.dot_general(a, b, (((1,), (1,)), ((), ())), preferred_element_type=F32)


def _dot_tn(a, b):
    return lax.dot_general(a, b, (((0,), (0,)), ((), ())), preferred_element_type=F32)


def _sigmoid(x):
    return 1.0 / (1.0 + jnp.exp(-x))


IN_TM = 1024
IN_TN = 512


def _rope_tile(acc, cos, sin_signed):
    lane = lax.broadcasted_iota(jnp.int32, (1, LANES), 1)
    first_half = (lane % A_DH) < (A_DH // 2)
    outs = []
    for c in range(IN_TN // LANES):
        t = acc[:, c * LANES:(c + 1) * LANES]
        partner = jnp.where(first_half,
                            pltpu.roll(t, LANES - A_DH // 2, axis=1),
                            pltpu.roll(t, A_DH // 2, axis=1))
        outs.append(t * cos + partner * sin_signed)
    return jnp.concatenate(outs, axis=1)


def _inproj_kernel(x_ref, nw_ref, w_ref, wg_ref, bg_ref, cos_ref, sin_ref,
                   out_ref, gate_ref, hn_ref):
    j = pl.program_id(1)

    @pl.when(j == 0)
    def _():
        x = x_ref[...]
        ms = jnp.mean(x * x, axis=-1, keepdims=True)
        hn = (x * lax.rsqrt(ms + EPS) * nw_ref[...]).astype(BF16)
        hn_ref[...] = hn
        gate_ref[...] = _dot(hn, wg_ref[...])

    acc = _dot(hn_ref[...], w_ref[...])

    is_plain = (j == 0) | (j == 2) | (j == 3) | (j == 10) | (j == 11)

    @pl.when(is_plain)
    def _():
        out_ref[...] = acc.astype(BF16)

    @pl.when(j == 1)
    def _():
        out_ref[...] = (acc * (M_DQK ** -0.5)).astype(BF16)

    @pl.when((j == 4) | (j == 5))
    def _():
        out_ref[...] = _sigmoid(acc).astype(BF16)

    @pl.when((j == 6) | (j == 7))
    def _():
        r = _rope_tile(acc, cos_ref[...], sin_ref[...])
        out_ref[...] = (r * (A_DH ** -0.5)).astype(BF16)

    @pl.when((j == 8) | (j == 9))
    def _():
        out_ref[...] = _rope_tile(acc, cos_ref[...], sin_ref[...]).astype(BF16)

    @pl.when(j >= 12)
    def _():
        out_ref[...] = _sigmoid(acc + bg_ref[...]).astype(BF16)


def _inproj(x2, norm_w, w_main, w_gate, b_gate, cos_t, sin_t, seq):
    t_rows = x2.shape[0]
    nj = N_MAIN // IN_TN
    s_blocks = seq // IN_TM
    return pl.pallas_call(
        _inproj_kernel,
        name="inproj",
        grid=(t_rows // IN_TM, nj),
        in_specs=[
            pl.BlockSpec((IN_TM, D_MODEL), lambda i, j: (i, 0)),
            pl.BlockSpec((1, D_MODEL), lambda i, j: (0, 0)),
            pl.BlockSpec((D_MODEL, IN_TN), lambda i, j: (0, j)),
            pl.BlockSpec((D_MODEL, LANES), lambda i, j: (0, 0)),
            pl.BlockSpec((1, IN_TN), lambda i, j: (0, jnp.maximum(j - 12, 0))),
            pl.BlockSpec((IN_TM, LANES), lambda i, j: (i % s_blocks, 0)),
            pl.BlockSpec((IN_TM, LANES), lambda i, j: (i % s_blocks, 0)),
        ],
        out_specs=[
            pl.BlockSpec((IN_TM, IN_TN), lambda i, j: (i, j)),
            pl.BlockSpec((IN_TM, LANES), lambda i, j: (i, 0)),
        ],
        out_shape=[
            jax.ShapeDtypeStruct((t_rows, N_MAIN), BF16),
            jax.ShapeDtypeStruct((t_rows, LANES), F32),
        ],
        scratch_shapes=[pltpu.VMEM((IN_TM, D_MODEL), BF16)],
        compiler_params=_cparams(("parallel", "arbitrary")),
    )(x2, norm_w, w_main, w_gate, b_gate, cos_t, sin_t)


L = M_CHUNK
DV_EXT = M_DV + LANES


def _softcap(t):
    return GATE_CAP * jnp.tanh(t / GATE_CAP)


def _log_sigmoid(t):
    return jnp.minimum(t, 0.0) - jnp.log(1.0 + jnp.exp(-jnp.abs(t)))


def _gate_act(pre, is_forget):
    c = _softcap(pre)
    return jnp.where(is_forget, _log_sigmoid(c), c)


def _split_dot(a, b, a_is_exact):
    if a_is_exact:
        hi = b.astype(BF16)
        lo = (b - hi.astype(F32)).astype(BF16)
        ab = a.astype(BF16)
        return _dot(ab, hi) + _dot(ab, lo)
    hi = a.astype(BF16)
    lo = (a - hi.astype(F32)).astype(BF16)
    bb = b.astype(BF16)
    return _dot(hi, bb) + _dot(lo, bb)


def _mlstm_kernel(qf_ref, kf_ref, vf_ref, gf_ref, qb_ref, kb_ref, vb_ref, gb_ref,
                  brow_ref, bcol_ref, hf_ref, hb_ref, c_ref, m_ref):
    step = pl.program_id(1)

    @pl.when(step == 0)
    def _():
        c_ref[...] = jnp.zeros_like(c_ref)
        m_ref[...] = jnp.zeros_like(m_ref)

    row = lax.broadcasted_iota(jnp.int32, (L, L), 0)
    col = lax.broadcasted_iota(jnp.int32, (L, L), 1)
    lane_id = lax.broadcasted_iota(jnp.int32, (1, LANES), 1)
    sub_id = lax.broadcasted_iota(jnp.int32, (LANES, 1), 0)
    forget_lane = (lane_id % 8) >= 4
    forget_sub = (sub_id % 8) >= 4
    ones_ext = jnp.ones((L, LANES), BF16)

    for d, (q_ref, k_ref, v_ref, g_ref, h_ref) in enumerate(
            ((qf_ref, kf_ref, vf_ref, gf_ref, hf_ref), (qb_ref, kb_ref, vb_ref, gb_ref, hb_ref))):
        visible = (row >= col) if d == 0 else (col >= row)
        vis_f = visible.astype(F32)

        g = g_ref[...]
        g_t = g.T
        act_c = _gate_act(g + brow_ref[...], forget_lane)
        act_r = _gate_act(g_t + bcol_ref[...], forget_sub)
        cum_c = _split_dot(vis_f, act_c, True)
        cum_r = _split_dot(act_r, vis_f.T, False)

        for h in range(M_HEADS):
            idx = d * M_HEADS + h
            ci = d * 8 + h
            cf = d * 8 + 4 + h
            bc = cum_c[:, cf:cf + 1]
            br = cum_r[cf:cf + 1, :]
            igc = act_c[:, ci:ci + 1]
            igr = act_r[ci:ci + 1, :]
            b_last = br[:, L - 1:L] if d == 0 else br[:, 0:1]
            m_old = m_ref[idx][0:1, 0:1]

            q = q_ref[:, h * M_DQK:(h + 1) * M_DQK]
            k = k_ref[:, h * M_DQK:(h + 1) * M_DQK]
            v_ext = jnp.concatenate([v_ref[:, h * M_DV:(h + 1) * M_DV], ones_ext], axis=1)

            dmat = jnp.where(visible, bc - br + igr, -jnp.inf)
            inter = bc + m_old
            m_t = jnp.maximum(inter, jnp.max(dmat, axis=1, keepdims=True))
            wts = jnp.exp(dmat - m_t)
            a = jnp.exp(inter - m_t)
            s = _dot_nt(q, k) * wts
            c_old = c_ref[idx]
            comb = a * _dot(q, c_old.astype(BF16)) + _dot(s.astype(BF16), v_ext)
            num = comb[:, :M_DV]
            den = comb[:, M_DV:M_DV + 1]
            hval = num / jnp.maximum(jnp.abs(den), jnp.exp(-m_t))
            h_ref[:, h * M_DV:(h + 1) * M_DV] = hval

            g_row = b_last - br + igr
            m_new = jnp.maximum(b_last + m_old, jnp.max(g_row, axis=1, keepdims=True))
            decay = jnp.exp(b_last + m_old - m_new)
            wk = jnp.exp(b_last - bc + igc - m_new)
            kw = (k.astype(F32) * wk).astype(BF16)
            c_ref[idx] = decay * c_old + _dot_tn(kw, v_ext)
            m_ref[idx] = jnp.broadcast_to(m_new, (8, LANES))


def _mlstm(proj, gates, bias_row, bias_col, batch, seq):
    t_rows = proj.shape[0]
    nc = seq // L
    fwd = lambda b, c: b * nc + c
    bwd = lambda b, c: b * nc + (nc - 1 - c)
    qk_w = M_HEADS * M_DQK
    in_specs = []
    for ch in (fwd, bwd):
        in_specs += [
            pl.BlockSpec((L, qk_w), lambda b, c, ch=ch: (ch(b, c), 0)),
            pl.BlockSpec((L, qk_w), lambda b, c, ch=ch: (ch(b, c), 1)),
            pl.BlockSpec((L, M_WIDTH), lambda b, c, ch=ch: (ch(b, c), 1)),
            pl.BlockSpec((L, LANES), lambda b, c, ch=ch: (ch(b, c), 0)),
        ]
    in_specs += [pl.BlockSpec((1, LANES), lambda b, c: (0, 0)),
                 pl.BlockSpec((LANES, 1), lambda b, c: (0, 0))]
    return pl.pallas_call(
        _mlstm_kernel,
        name="mlstm",
        grid=(batch, nc),
        in_specs=in_specs,
        out_specs=[pl.BlockSpec((L, M_WIDTH), lambda b, c: (fwd(b, c), 0)),
                   pl.BlockSpec((L, M_WIDTH), lambda b, c: (bwd(b, c), 0))],
        out_shape=[jax.ShapeDtypeStruct((t_rows, M_WIDTH), F32)] * 2,
        scratch_shapes=[pltpu.VMEM((2 * M_HEADS, M_DQK, DV_EXT), F32),
                        pltpu.VMEM((2 * M_HEADS, 8, LANES), F32)],
        compiler_params=_cparams(("parallel", "arbitrary")),
    )(proj, proj, proj, gates, proj, proj, proj, gates, bias_row, bias_col)


AT_TQ = 512
AT_TK = 512
AQ_BLK = 3072 // LANES
AK_BLK = 4096 // LANES
AV_BLK = 5120 // LANES


def _attn_kernel(q_ref, k_ref, v_ref, lq1_ref, lk1_ref, lq2_ref, lk2_ref, nw_ref,
                 o_ref, acc1_ref, acc2_ref):
    seq = k_ref.shape[0]
    q = q_ref[...]
    lane = lax.broadcasted_iota(jnp.int32, (1, LANES), 1)
    zero = jnp.zeros_like(q)
    q1 = jnp.where(lane < A_DH, q, zero)
    q2 = jnp.where(lane >= A_DH, q, zero)
    acc1_ref[...] = jnp.zeros_like(acc1_ref)
    acc2_ref[...] = jnp.zeros_like(acc2_ref)

    def one_map(qz, kblk, vblk, m, l, acc_ref):
        s = _dot_nt(qz, kblk)
        m_new = jnp.maximum(m, jnp.max(s, axis=1, keepdims=True))
        alpha = jnp.exp(m - m_new)
        p = jnp.exp(s - m_new)
        l_new = alpha * l + jnp.sum(p, axis=1, keepdims=True)
        acc_ref[...] = alpha * acc_ref[...] + _dot(p.astype(BF16), vblk)
        return m_new, l_new

    def body(i, carry):
        m1, l1, m2, l2 = carry
        off = pl.multiple_of(i * AT_TK, AT_TK)
        kblk = k_ref[pl.ds(off, AT_TK), :]
        vblk = v_ref[pl.ds(off, AT_TK), :]
        m1, l1 = one_map(q1, kblk, vblk, m1, l1, acc1_ref)
        m2, l2 = one_map(q2, kblk, vblk, m2, l2, acc2_ref)
        return m1, l1, m2, l2

    neg = jnp.full((AT_TQ, 1), -jnp.inf, F32)
    zer = jnp.zeros((AT_TQ, 1), F32)
    m1, l1, m2, l2 = lax.fori_loop(0, seq // AT_TK, body, (neg, zer, neg, zer))

    lam = (jnp.exp(jnp.sum(lq1_ref[...] * lk1_ref[...], axis=1, keepdims=True))
           - jnp.exp(jnp.sum(lq2_ref[...] * lk2_ref[...], axis=1, keepdims=True))
           + LAM_INIT)
    o = acc1_ref[...] / l1 - lam * (acc2_ref[...] / l2)
    ms = jnp.mean(o * o, axis=-1, keepdims=True)
    y = o * lax.rsqrt(ms + EPS) * nw_ref[...]
    o_ref[...] = (y * (1.0 - LAM_INIT)).astype(BF16)


def _attention(proj, lq1, lk1, lq2, lk2, norm_w, batch, seq):
    t_rows = proj.shape[0]
    nq = seq // AT_TQ
    small = pl.BlockSpec((1, A_DH), lambda b, h, i: (0, 0))
    return pl.pallas_call(
        _attn_kernel,
        name="attention",
        grid=(batch, A_HEADS, nq),
        in_specs=[
            pl.BlockSpec((AT_TQ, LANES), lambda b, h, i: (b * nq + i, AQ_BLK + h)),
            pl.BlockSpec((seq, LANES), lambda b, h, i: (b, AK_BLK + h)),
            pl.BlockSpec((seq, LANES), lambda b, h, i: (b, AV_BLK + h)),
            small, small, small, small,
            pl.BlockSpec((1, A_DV), lambda b, h, i: (0, 0)),
        ],
        out_specs=pl.BlockSpec((AT_TQ, LANES), lambda b, h, i: (b * nq + i, h)),
        out_shape=jax.ShapeDtypeStruct((t_rows, A_WIDTH), BF16),
        scratch_shapes=[pltpu.VMEM((AT_TQ, A_DV), F32), pltpu.VMEM((AT_TQ, A_DV), F32)],
        compiler_params=_cparams(("parallel", "parallel", "arbitrary")),
    )(proj, proj, proj, lq1, lk1, lq2, lk2, norm_w)


MG_TM = 256


def _merge_kernel(hf_ref, hb_ref, mo_ref, ha_ref, gm_ref, ga_ref, nw_ref, wm_ref, wa_ref, out_ref):
    hm = hf_ref[...] + hb_ref[...]
    parts = []
    for h in range(M_HEADS):
        seg = hm[:, h * M_DV:(h + 1) * M_DV]
        ms = jnp.mean(seg * seg, axis=-1, keepdims=True)
        parts.append(seg * lax.rsqrt(ms + EPS))
    hn = jnp.concatenate(parts, axis=1) * nw_ref[...]
    hn = (hn * mo_ref[...].astype(F32)).astype(BF16)
    branch_m = _dot(hn, wm_ref[...])
    branch_a = _dot(ha_ref[...], wa_ref[...])
    mixed = gm_ref[...].astype(F32) * branch_m + ga_ref[...].astype(F32) * branch_a
    out_ref[...] = mixed.astype(BF16)


def _merge(hf, hb, proj, ha, norm_w, w_m, w_a):
    t_rows = hf.shape[0]
    row = lambda i: (i, 0)
    const = lambda i: (0, 0)
    return pl.pallas_call(
        _merge_kernel,
        name="merge",
        grid=(t_rows // MG_TM,),
        in_specs=[
            pl.BlockSpec((MG_TM, M_WIDTH), row),
            pl.BlockSpec((MG_TM, M_WIDTH), row),
            pl.BlockSpec((MG_TM, M_WIDTH), lambda i: (i, 2)),
            pl.BlockSpec((MG_TM, A_WIDTH), row),
            pl.BlockSpec((MG_TM, D_MODEL), lambda i: (i, 3)),
            pl.BlockSpec((MG_TM, D_MODEL), lambda i: (i, 4)),
            pl.BlockSpec((1, M_WIDTH), const),
            pl.BlockSpec((M_WIDTH, D_MODEL), const),
            pl.BlockSpec((A_WIDTH, D_MODEL), const),
        ],
        out_specs=pl.BlockSpec((MG_TM, D_MODEL), row),
        out_shape=jax.ShapeDtypeStruct((t_rows, D_MODEL), BF16),
        compiler_params=_cparams(("parallel",)),
    )(hf, hb, proj, ha, proj, proj, norm_w, w_m, w_a)


OP_TM = 256


def _outproj_kernel(mixed_ref, x_ref, w_ref, nw_ref, x1_ref, h2_ref):
    x1 = x_ref[...] + _dot(mixed_ref[...], w_ref[...])
    x1_ref[...] = x1
    ms = jnp.mean(x1 * x1, axis=-1, keepdims=True)
    h2_ref[...] = (x1 * lax.rsqrt(ms + EPS) * nw_ref[...]).astype(BF16)


def _outproj(mixed, x2, w_out, norm_w):
    t_rows = x2.shape[0]
    row = lambda i: (i, 0)
    const = lambda i: (0, 0)
    return pl.pallas_call(
        _outproj_kernel,
        name="outproj",
        grid=(t_rows // OP_TM,),
        in_specs=[
            pl.BlockSpec((OP_TM, D_MODEL), row),
            pl.BlockSpec((OP_TM, D_MODEL), row),
            pl.BlockSpec((D_MODEL, D_MODEL), const),
            pl.BlockSpec((1, D_MODEL), const),
        ],
        out_specs=[pl.BlockSpec((OP_TM, D_MODEL), row), pl.BlockSpec((OP_TM, D_MODEL), row)],
        out_shape=[jax.ShapeDtypeStruct((t_rows, D_MODEL), F32),
                   jax.ShapeDtypeStruct((t_rows, D_MODEL), BF16)],
        compiler_params=_cparams(("parallel",)),
    )(mixed, x2, w_out, norm_w)


FI_TM = 1024
FI_TN = 512


def _ffn_in_kernel(h_ref, wg_ref, wu_ref, out_ref):
    h = h_ref[...]
    gate = _dot(h, wg_ref[...])
    up = _dot(h, wu_ref[...])
    out_ref[...] = (gate * _sigmoid(gate) * up).astype(BF16)


def _ffn_in(h2, w_ffn_in):
    t_rows = h2.shape[0]
    nj = D_FF // FI_TN
    return pl.pallas_call(
        _ffn_in_kernel,
        name="ffn_in",
        grid=(t_rows // FI_TM, nj),
        in_specs=[
            pl.BlockSpec((FI_TM, D_MODEL), lambda i, j: (i, 0)),
            pl.BlockSpec((D_MODEL, FI_TN), lambda i, j: (0, j)),
            pl.BlockSpec((D_MODEL, FI_TN), lambda i, j: (0, nj + j)),
        ],
        out_specs=pl.BlockSpec((FI_TM, FI_TN), lambda i, j: (i, j)),
        out_shape=jax.ShapeDtypeStruct((t_rows, D_FF), BF16),
        compiler_params=_cparams(("parallel", "arbitrary")),
    )(h2, w_ffn_in, w_ffn_in)


FO_TM = 512
FO_TK = 512


def _ffn_out_kernel(act_ref, w_ref, x1_ref, nw_ref, out_ref, acc_ref):
    kk = pl.program_id(1)

    @pl.when(kk == 0)
    def _():
        acc_ref[...] = x1_ref[...]

    acc_ref[...] += _dot(act_ref[...], w_ref[...])

    @pl.when(kk == pl.num_programs(1) - 1)
    def _():
        x2 = acc_ref[...]
        ms = jnp.mean(x2 * x2, axis=-1, keepdims=True)
        out_ref[...] = x2 * lax.rsqrt(ms + EPS) * nw_ref[...]


def _ffn_out(act, w_ffn_out, x1, norm_w):
    t_rows = x1.shape[0]
    return pl.pallas_call(
        _ffn_out_kernel,
        name="ffn_out",
        grid=(t_rows // FO_TM, D_FF // FO_TK),
        in_specs=[
            pl.BlockSpec((FO_TM, FO_TK), lambda i, k: (i, k)),
            pl.BlockSpec((FO_TK, D_MODEL), lambda i, k: (k, 0)),
            pl.BlockSpec((FO_TM, D_MODEL), lambda i, k: (i, 0)),
            pl.BlockSpec((1, D_MODEL), lambda i, k: (0, 0)),
        ],
        out_specs=pl.BlockSpec((FO_TM, D_MODEL), lambda i, k: (i, 0)),
        out_shape=jax.ShapeDtypeStruct((t_rows, D_MODEL), F32),
        scratch_shapes=[pltpu.VMEM((FO_TM, D_MODEL), F32)],
        compiler_params=_cparams(("parallel", "arbitrary")),
    )(act, w_ffn_out, x1, norm_w)


def _rope_tables(seq):
    inv = ROPE_THETA ** (-jnp.arange(0, A_DH, 2, dtype=F32) / A_DH)
    ang = jnp.arange(seq, dtype=F32)[:, None] * inv[None, :]
    cos = jnp.cos(ang)
    sin = jnp.sin(ang)
    reps = LANES // A_DH
    cos_t = jnp.tile(jnp.concatenate([cos, cos], axis=1), (1, reps))
    sin_t = jnp.tile(jnp.concatenate([-sin, sin], axis=1), (1, reps))
    return cos_t, sin_t


def kernel(x, norm1_w, w_in, b_igate, b_fgate, b_branch_gate, mlstm_norm_w, lam_q1, lam_k1, lam_q2, lam_k2, attn_norm_w, w_branch_m, w_branch_a, w_out, norm2_w, w_ffn_in, w_ffn_out, final_norm_w):
    batch, seq, d = x.shape
    depth = w_in.shape[0]
    assert d == D_MODEL and depth == 1 and seq % IN_TM == 0 and seq % AT_TQ == 0
    t_rows = batch * seq
    x2 = x.reshape(t_rows, d)
    cos_t, sin_t = _rope_tables(seq)

    l = 0
    w = w_in[l]
    w_main = jnp.concatenate([w[:, :GATE_OFF], w[:, GATE_OFF + N_GATE:]], axis=1).astype(BF16)
    w_gate = jnp.pad(w[:, GATE_OFF:GATE_OFF + N_GATE], ((0, 0), (0, LANES - N_GATE))).astype(BF16)
    gate_bias = jnp.stack([b_igate[l], b_fgate[l]], axis=1).reshape(N_GATE).astype(F32)
    gate_bias = jnp.pad(gate_bias, (0, LANES - N_GATE))

    proj, gates = _inproj(x2, norm1_w[l].reshape(1, d), w_main, w_gate,
                          b_branch_gate[l].reshape(1, -1), cos_t, sin_t, seq)
    hf, hb = _mlstm(proj, gates, gate_bias.reshape(1, LANES), gate_bias.reshape(LANES, 1), batch, seq)
    ha = _attention(proj, lam_q1[l].reshape(1, A_DH), lam_k1[l].reshape(1, A_DH),
                    lam_q2[l].reshape(1, A_DH), lam_k2[l].reshape(1, A_DH),
                    attn_norm_w[l].reshape(1, A_DV), batch, seq)
    mixed = _merge(hf, hb, proj, ha, mlstm_norm_w[l].reshape(1, M_WIDTH),
                   w_branch_m[l].astype(BF16), w_branch_a[l].astype(BF16))
    x1, h2 = _outproj(mixed, x2, w_out[l].astype(BF16), norm2_w[l].reshape(1, d))
    act = _ffn_in(h2, w_ffn_in[l].astype(BF16))
    out = _ffn_out(act, w_ffn_out[l].astype(BF16), x1, final_norm_w.reshape(1, d))
    return out.reshape(batch, seq, d)
```

```python
import functools
import math

import jax
import jax.numpy as jnp
from jax import lax
from jax.experimental import pallas as pl
from jax.experimental.pallas import tpu as pltpu

F32 = jnp.float32
BF16 = jnp.bfloat16

D_MODEL = 2048
M_HEADS = 4
M_DQK = 128
M_DV = 256
M_CHUNK = 128
GATE_CAP = 15.0
A_HEADS = 8
A_DH = 64
A_DV = 2 * A_DH
ROPE_THETA = 10000.0
D_FF = 5632
EPS = 1e-6
M_WIDTH = M_HEADS * M_DV
A_WIDTH = A_HEADS * A_DV
LAM_INIT = 0.8 - 0.6 * math.exp(-0.3 * 0)

N_MAIN = 10240
GATE_OFF = 3072
N_GATE = 4 * M_HEADS
LANES = 128

VMEM_LIMIT = 56 * 1024 * 1024


def _cparams(sem, flags=None):
    return pltpu.CompilerParams(dimension_semantics=sem, vmem_limit_bytes=VMEM_LIMIT, flags=flags)


def _dot(a, b):
    return jnp.dot(a, b, preferred_element_type=F32)


def _dot_nt(a, b):
    return lax.dot_general(a, b, (((1,), (1,)), ((), ())), preferred_element_type=F32)


def _dot_tn(a, b):
    return lax.dot_general(a, b, (((0,), (0,)), ((), ())), preferred_element_type=F32)


def _sigmoid(x):
    return 1.0 / (1.0 + jnp.exp(-x))


IN_TM = 1024
IN_TN = 512
AV_J0 = 10
Q_SCALE = (A_DH ** -0.5) * math.log2(math.e)


def _rope_tile(acc, cos, sin_signed):
    lane = lax.broadcasted_iota(jnp.int32, (1, LANES), 1)
    first_half = (lane % A_DH) < (A_DH // 2)
    outs = []
    for c in range(IN_TN // LANES):
        t = acc[:, c * LANES:(c + 1) * LANES]
        partner = jnp.where(first_half,
                            pltpu.roll(t, LANES - A_DH // 2, axis=1),
                            pltpu.roll(t, A_DH // 2, axis=1))
        outs.append(t * cos + partner * sin_signed)
    return jnp.concatenate(outs, axis=1)


def _inproj_kernel(x_ref, nw_ref, w_ref, wg_ref, bg_ref, cos_ref, sin_ref,
                   out_ref, gate_ref, vt_ref, hn_ref):
    j = pl.program_id(1)

    @pl.when(j == 0)
    def _():
        x = x_ref[...]
        ms = jnp.mean(x * x, axis=-1, keepdims=True)
        hn = (x * lax.rsqrt(ms + EPS) * nw_ref[...]).astype(BF16)
        hn_ref[...] = hn
        gate_ref[...] = _dot(hn, wg_ref[...])

    acc = _dot(hn_ref[...], w_ref[...])

    is_plain = (j == 0) | (j == 2) | (j == 3)

    @pl.when(is_plain)
    def _():
        out_ref[...] = acc.astype(BF16)

    @pl.when((j == 10) | (j == 11))
    def _():
        out_ref[...] = acc.astype(BF16)
        vt_ref[0] = acc.T.astype(BF16)

    @pl.when(j == 1)
    def _():
        out_ref[...] = (acc * (M_DQK ** -0.5)).astype(BF16)

    @pl.when((j == 4) | (j == 5))
    def _():
        out_ref[...] = _sigmoid(acc).astype(BF16)

    @pl.when((j == 6) | (j == 7))
    def _():
        r = _rope_tile(acc, cos_ref[...], sin_ref[...])
        out_ref[...] = (r * Q_SCALE).astype(BF16)

    @pl.when((j == 8) | (j == 9))
    def _():
        out_ref[...] = _rope_tile(acc, cos_ref[...], sin_ref[...]).astype(BF16)

    @pl.when(j >= 12)
    def _():
        out_ref[...] = _sigmoid(acc + bg_ref[...]).astype(BF16)


def _inproj(x2, norm_w, w_main, w_gate, b_gate, cos_t, sin_t, seq):
    t_rows = x2.shape[0]
    nj = N_MAIN // IN_TN
    s_blocks = seq // IN_TM
    return pl.pallas_call(
        _inproj_kernel,
        name="inproj",
        grid=(t_rows // IN_TM, nj),
        in_specs=[
            pl.BlockSpec((IN_TM, D_MODEL), lambda i, j: (i, 0)),
            pl.BlockSpec((1, D_MODEL), lambda i, j: (0, 0)),
            pl.BlockSpec((D_MODEL, IN_TN), lambda i, j: (0, j)),
            pl.BlockSpec((D_MODEL, LANES), lambda i, j: (0, 0)),
            pl.BlockSpec((1, IN_TN), lambda i, j: (0, jnp.maximum(j - 12, 0))),
            pl.BlockSpec((IN_TM, LANES), lambda i, j: (i % s_blocks, 0)),
            pl.BlockSpec((IN_TM, LANES), lambda i, j: (i % s_blocks, 0)),
        ],
        out_specs=[
            pl.BlockSpec((IN_TM, IN_TN), lambda i, j: (i, j)),
            pl.BlockSpec((IN_TM, LANES), lambda i, j: (i, 0)),
            pl.BlockSpec((1, IN_TN, IN_TM),
                         lambda i, j: (i // s_blocks, jnp.clip(j - AV_J0, 0, 1), i % s_blocks)),
        ],
        out_shape=[
            jax.ShapeDtypeStruct((t_rows, N_MAIN), BF16),
            jax.ShapeDtypeStruct((t_rows, LANES), F32),
            jax.ShapeDtypeStruct((t_rows // seq, A_WIDTH, seq), BF16),
        ],
        scratch_shapes=[pltpu.VMEM((IN_TM, D_MODEL), BF16)],
        compiler_params=_cparams(("parallel", "arbitrary")),
    )(x2, norm_w, w_main, w_gate, b_gate, cos_t, sin_t)


L = M_CHUNK
DV_EXT = M_DV + LANES


def _softcap(t):
    return GATE_CAP * jnp.tanh(t / GATE_CAP)


def _log_sigmoid(t):
    return jnp.minimum(t, 0.0) - jnp.log(1.0 + jnp.exp(-jnp.abs(t)))


def _gate_act(pre, is_forget):
    c = _softcap(pre)
    return jnp.where(is_forget, _log_sigmoid(c), c)


def _split_dot(a, b, a_is_exact):
    if a_is_exact:
        hi = b.astype(BF16)
        lo = (b - hi.astype(F32)).astype(BF16)
        ab = a.astype(BF16)
        return _dot(ab, hi) + _dot(ab, lo)
    hi = a.astype(BF16)
    lo = (a - hi.astype(F32)).astype(BF16)
    bb = b.astype(BF16)
    return _dot(hi, bb) + _dot(lo, bb)


def _mlstm_kernel(qf_ref, kf_ref, vf_ref, gf_ref, qb_ref, kb_ref, vb_ref, gb_ref,
                  brow_ref, bcol_ref, hf_ref, hb_ref, c_ref, m_ref):
    step = pl.program_id(1)

    @pl.when(step == 0)
    def _():
        c_ref[...] = jnp.zeros_like(c_ref)
        m_ref[...] = jnp.zeros_like(m_ref)

    row = lax.broadcasted_iota(jnp.int32, (L, L), 0)
    col = lax.broadcasted_iota(jnp.int32, (L, L), 1)
    lane_id = lax.broadcasted_iota(jnp.int32, (1, LANES), 1)
    sub_id = lax.broadcasted_iota(jnp.int32, (LANES, 1), 0)
    forget_lane = (lane_id % 8) >= 4
    forget_sub = (sub_id % 8) >= 4
    ones_ext = jnp.ones((L, LANES), BF16)

    for d, (q_ref, k_ref, v_ref, g_ref, h_ref) in enumerate(
            ((qf_ref, kf_ref, vf_ref, gf_ref, hf_ref), (qb_ref, kb_ref, vb_ref, gb_ref, hb_ref))):
        visible = (row >= col) if d == 0 else (col >= row)
        vis_f = visible.astype(F32)

        g = g_ref[...]
        g_t = g.T
        act_c = _gate_act(g + brow_ref[...], forget_lane)
        act_r = _gate_act(g_t + bcol_ref[...], forget_sub)
        cum_c = _split_dot(vis_f, act_c, True)
        cum_r = _split_dot(act_r, vis_f.T, False)

        for h in range(M_HEADS):
            idx = d * M_HEADS + h
            ci = d * 8 + h
            cf = d * 8 + 4 + h
            bc = cum_c[:, cf:cf + 1]
            br = cum_r[cf:cf + 1, :]
            igc = act_c[:, ci:ci + 1]
            igr = act_r[ci:ci + 1, :]
            b_last = br[:, L - 1:L] if d == 0 else br[:, 0:1]
            m_old = m_ref[idx][0:1, 0:1]

            q = q_ref[:, h * M_DQK:(h + 1) * M_DQK]
            k = k_ref[:, h * M_DQK:(h + 1) * M_DQK]
            v_ext = jnp.concatenate([v_ref[:, h * M_DV:(h + 1) * M_DV], ones_ext], axis=1)

            dmat = jnp.where(visible, bc - br + igr, -jnp.inf)
            inter = bc + m_old
            m_t = jnp.maximum(inter, jnp.max(dmat, axis=1, keepdims=True))
            wts = jnp.exp(dmat - m_t)
            a = jnp.exp(inter - m_t)
            s = _dot_nt(q, k) * wts
            c_old = c_ref[idx]
            comb = a * _dot(q, c_old.astype(BF16)) + _dot(s.astype(BF16), v_ext)
            num = comb[:, :M_DV]
            den = comb[:, M_DV:M_DV + 1]
            hval = num / jnp.maximum(jnp.abs(den), jnp.exp(-m_t))
            h_ref[:, h * M_DV:(h + 1) * M_DV] = hval

            g_row = b_last - br + igr
            m_new = jnp.maximum(b_last + m_old, jnp.max(g_row, axis=1, keepdims=True))
            decay = jnp.exp(b_last + m_old - m_new)
            wk = jnp.exp(b_last - bc + igc - m_new)
            kw = (k.astype(F32) * wk).astype(BF16)
            c_ref[idx] = decay * c_old + _dot_tn(kw, v_ext)
            m_ref[idx] = jnp.broadcast_to(m_new, (8, LANES))


def _mlstm(proj, gates, bias_row, bias_col, batch, seq):
    t_rows = proj.shape[0]
    nc = seq // L
    fwd = lambda b, c: b * nc + c
    bwd = lambda b, c: b * nc + (nc - 1 - c)
    qk_w = M_HEADS * M_DQK
    in_specs = []
    for ch in (fwd, bwd):
        in_specs += [
            pl.BlockSpec((L, qk_w), lambda b, c, ch=ch: (ch(b, c), 0)),
            pl.BlockSpec((L, qk_w), lambda b, c, ch=ch: (ch(b, c), 1)),
            pl.BlockSpec((L, M_WIDTH), lambda b, c, ch=ch: (ch(b, c), 1)),
            pl.BlockSpec((L, LANES), lambda b, c, ch=ch: (ch(b, c), 0)),
        ]
    in_specs += [pl.BlockSpec((1, LANES), lambda b, c: (0, 0)),
                 pl.BlockSpec((LANES, 1), lambda b, c: (0, 0))]
    return pl.pallas_call(
        _mlstm_kernel,
        name="mlstm",
        grid=(batch, nc),
        in_specs=in_specs,
        out_specs=[pl.BlockSpec((L, M_WIDTH), lambda b, c: (fwd(b, c), 0)),
                   pl.BlockSpec((L, M_WIDTH), lambda b, c: (bwd(b, c), 0))],
        out_shape=[jax.ShapeDtypeStruct((t_rows, M_WIDTH), F32)] * 2,
        scratch_shapes=[pltpu.VMEM((2 * M_HEADS, M_DQK, DV_EXT), F32),
                        pltpu.VMEM((2 * M_HEADS, 8, LANES), F32)],
        compiler_params=_cparams(("parallel", "arbitrary")),
    )(proj, proj, proj, gates, proj, proj, proj, gates, bias_row, bias_col)


AT_TQ = 512
AT_TK = 512
AQ_BLK = 3072 // LANES
AK_BLK = 4096 // LANES


def _attn_kernel(q_ref, k_ref, vt_ref, lq1_ref, lk1_ref, lq2_ref, lk2_ref, nw_ref,
                 o_ref, acc1_ref, acc2_ref, sa1_ref, sa2_ref, sb1_ref, sb2_ref):
    seq = k_ref.shape[0]
    nblk = seq // AT_TK
    q = q_ref[...]
    lane = lax.broadcasted_iota(jnp.int32, (1, LANES), 1)
    zero = jnp.zeros_like(q)
    q1 = jnp.where(lane < A_DH, q, zero)
    q2 = jnp.where(lane >= A_DH, q, zero)
    acc1_ref[...] = jnp.zeros_like(acc1_ref)
    acc2_ref[...] = jnp.zeros_like(acc2_ref)

    def produce(i, s1_ref, s2_ref):
        off = pl.multiple_of(i * AT_TK, AT_TK)
        kblk = k_ref[pl.ds(off, AT_TK), :]
        s1 = _dot_nt(kblk, q1)
        s1_ref[...] = s1
        s2 = _dot_nt(kblk, q2)
        s2_ref[...] = s2
        return jnp.max(s1, axis=0, keepdims=True), jnp.max(s2, axis=0, keepdims=True)

    def consume_map(s_ref, vtblk, mb, m, l, acc_ref):
        m_new = jnp.maximum(m, mb)
        alpha = jnp.exp2(m - m_new)
        p = jnp.exp2(s_ref[...] - m_new)
        l_new = alpha * l + jnp.sum(p, axis=0, keepdims=True)
        acc_ref[...] = alpha * acc_ref[...] + _dot(vtblk, p.astype(BF16))
        return m_new, l_new

    def consume(i, s1_ref, s2_ref, mb, stats):
        m1, l1, m2, l2 = stats
        off = pl.multiple_of(i * AT_TK, AT_TK)
        vtblk = vt_ref[0, :, pl.ds(off, AT_TK)]
        m1, l1 = consume_map(s1_ref, vtblk, mb[0], m1, l1, acc1_ref)
        m2, l2 = consume_map(s2_ref, vtblk, mb[1], m2, l2, acc2_ref)
        return m1, l1, m2, l2

    def body(j, carry):
        mb_a, stats = carry[:2], carry[2:]
        mb_b = produce(2 * j + 1, sb1_ref, sb2_ref)
        stats = consume(2 * j, sa1_ref, sa2_ref, mb_a, stats)
        mb_a = produce(2 * j + 2, sa1_ref, sa2_ref)
        stats = consume(2 * j + 1, sb1_ref, sb2_ref, mb_b, stats)
        return (*mb_a, *stats)

    neg = jnp.full((1, AT_TQ), -jnp.inf, F32)
    zer = jnp.zeros((1, AT_TQ), F32)
    mb_a = produce(0, sa1_ref, sa2_ref)
    carry = lax.fori_loop(0, nblk // 2 - 1, body, (*mb_a, neg, zer, neg, zer))
    mb_a, stats = carry[:2], carry[2:]
    mb_b = produce(nblk - 1, sb1_ref, sb2_ref)
    stats = consume(nblk - 2, sa1_ref, sa2_ref, mb_a, stats)
    m1, l1, m2, l2 = consume(nblk - 1, sb1_ref, sb2_ref, mb_b, stats)

    lam = (jnp.exp(jnp.sum(lq1_ref[...] * lk1_ref[...], axis=1, keepdims=True))
           - jnp.exp(jnp.sum(lq2_ref[...] * lk2_ref[...], axis=1, keepdims=True))
           + LAM_INIT)
    o = acc1_ref[...] / l1 - lam * (acc2_ref[...] / l2)
    ms = jnp.mean(o * o, axis=0, keepdims=True)
    y = o * lax.rsqrt(ms + EPS) * nw_ref[...] * (1.0 - LAM_INIT)
    o_ref[...] = y.T.astype(BF16)


def _attention(proj, vt, lq1, lk1, lq2, lk2, norm_w, batch, seq):
    t_rows = proj.shape[0]
    nq = seq // AT_TQ
    small = pl.BlockSpec((1, A_DH), lambda b, h, i: (0, 0))
    return pl.pallas_call(
        _attn_kernel,
        name="attention",
        grid=(batch, A_HEADS, nq),
        in_specs=[
            pl.BlockSpec((AT_TQ, LANES), lambda b, h, i: (b * nq + i, AQ_BLK + h)),
            pl.BlockSpec((seq, LANES), lambda b, h, i: (b, AK_BLK + h)),
            pl.BlockSpec((1, A_DV, seq), lambda b, h, i: (b, h, 0)),
            small, small, small, small,
            pl.BlockSpec((A_DV, 1), lambda b, h, i: (0, 0)),
        ],
        out_specs=pl.BlockSpec((AT_TQ, LANES), lambda b, h, i: (b * nq + i, h)),
        out_shape=jax.ShapeDtypeStruct((t_rows, A_WIDTH), BF16),
        scratch_shapes=[pltpu.VMEM((A_DV, AT_TQ), F32)] * 2 + [pltpu.VMEM((AT_TK, AT_TQ), F32)] * 4,
        compiler_params=_cparams(("parallel", "parallel", "arbitrary")),
    )(proj, proj, vt, lq1, lk1, lq2, lk2, norm_w)


MG_TM = 256


def _merge_kernel(hf_ref, hb_ref, mo_ref, ha_ref, gm_ref, ga_ref, nw_ref, wm_ref, wa_ref, out_ref):
    hm = hf_ref[...] + hb_ref[...]
    parts = []
    for h in range(M_HEADS):
        seg = hm[:, h * M_DV:(h + 1) * M_DV]
        ms = jnp.mean(seg * seg, axis=-1, keepdims=True)
        parts.append(seg * lax.rsqrt(ms + EPS))
    hn = jnp.concatenate(parts, axis=1) * nw_ref[...]
    hn = (hn * mo_ref[...].astype(F32)).astype(BF16)
    branch_m = _dot(hn, wm_ref[...])
    branch_a = _dot(ha_ref[...], wa_ref[...])
    mixed = gm_ref[...].astype(F32) * branch_m + ga_ref[...].astype(F32) * branch_a
    out_ref[...] = mixed.astype(BF16)


def _merge(hf, hb, proj, ha, norm_w, w_m, w_a):
    t_rows = hf.shape[0]
    row = lambda i: (i, 0)
    const = lambda i: (0, 0)
    return pl.pallas_call(
        _merge_kernel,
        name="merge",
        grid=(t_rows // MG_TM,),
        in_specs=[
            pl.BlockSpec((MG_TM, M_WIDTH), row),
            pl.BlockSpec((MG_TM, M_WIDTH), row),
            pl.BlockSpec((MG_TM, M_WIDTH), lambda i: (i, 2)),
            pl.BlockSpec((MG_TM, A_WIDTH), row),
            pl.BlockSpec((MG_TM, D_MODEL), lambda i: (i, 3)),
            pl.BlockSpec((MG_TM, D_MODEL), lambda i: (i, 4)),
            pl.BlockSpec((1, M_WIDTH), const),
            pl.BlockSpec((M_WIDTH, D_MODEL), const),
            pl.BlockSpec((A_WIDTH, D_MODEL), const),
        ],
        out_specs=pl.BlockSpec((MG_TM, D_MODEL), row),
        out_shape=jax.ShapeDtypeStruct((t_rows, D_MODEL), BF16),
        compiler_params=_cparams(("parallel",)),
    )(hf, hb, proj, ha, proj, proj, norm_w, w_m, w_a)


OP_TM = 256


def _outproj_kernel(mixed_ref, x_ref, w_ref, nw_ref, x1_ref, h2_ref):
    x1 = x_ref[...] + _dot(mixed_ref[...], w_ref[...])
    x1_ref[...] = x1
    ms = jnp.mean(x1 * x1, axis=-1, keepdims=True)
    h2_ref[...] = (x1 * lax.rsqrt(ms + EPS) * nw_ref[...]).astype(BF16)


def _outproj(mixed, x2, w_out, norm_w):
    t_rows = x2.shape[0]
    row = lambda i: (i, 0)
    const = lambda i: (0, 0)
    return pl.pallas_call(
        _outproj_kernel,
        name="outproj",
        grid=(t_rows // OP_TM,),
        in_specs=[
            pl.BlockSpec((OP_TM, D_MODEL), row),
            pl.BlockSpec((OP_TM, D_MODEL), row),
            pl.BlockSpec((D_MODEL, D_MODEL), const),
            pl.BlockSpec((1, D_MODEL), const),
        ],
        out_specs=[pl.BlockSpec((OP_TM, D_MODEL), row), pl.BlockSpec((OP_TM, D_MODEL), row)],
        out_shape=[jax.ShapeDtypeStruct((t_rows, D_MODEL), F32),
                   jax.ShapeDtypeStruct((t_rows, D_MODEL), BF16)],
        compiler_params=_cparams(("parallel",)),
    )(mixed, x2, w_out, norm_w)


FI_TM = 1024
FI_TN = 512


def _ffn_in_kernel(h_ref, wg_ref, wu_ref, out_ref):
    h = h_ref[...]
    gate = _dot(h, wg_ref[...])
    up = _dot(h, wu_ref[...])
    out_ref[...] = (gate * _sigmoid(gate) * up).astype(BF16)


def _ffn_in(h2, w_ffn_in):
    t_rows = h2.shape[0]
    nj = D_FF // FI_TN
    return pl.pallas_call(
        _ffn_in_kernel,
        name="ffn_in",
        grid=(t_rows // FI_TM, nj),
        in_specs=[
            pl.BlockSpec((FI_TM, D_MODEL), lambda i, j: (i, 0)),
            pl.BlockSpec((D_MODEL, FI_TN), lambda i, j: (0, j)),
            pl.BlockSpec((D_MODEL, FI_TN), lambda i, j: (0, nj + j)),
        ],
        out_specs=pl.BlockSpec((FI_TM, FI_TN), lambda i, j: (i, j)),
        out_shape=jax.ShapeDtypeStruct((t_rows, D_FF), BF16),
        compiler_params=_cparams(("parallel", "arbitrary")),
    )(h2, w_ffn_in, w_ffn_in)


FO_TM = 512
FO_TK = 512


def _ffn_out_kernel(act_ref, w_ref, x1_ref, nw_ref, out_ref, acc_ref):
    kk = pl.program_id(1)

    @pl.when(kk == 0)
    def _():
        acc_ref[...] = x1_ref[...]

    acc_ref[...] += _dot(act_ref[...], w_ref[...])

    @pl.when(kk == pl.num_programs(1) - 1)
    def _():
        x2 = acc_ref[...]
        ms = jnp.mean(x2 * x2, axis=-1, keepdims=True)
        out_ref[...] = x2 * lax.rsqrt(ms + EPS) * nw_ref[...]


def _ffn_out(act, w_ffn_out, x1, norm_w):
    t_rows = x1.shape[0]
    return pl.pallas_call(
        _ffn_out_kernel,
        name="ffn_out",
        grid=(t_rows // FO_TM, D_FF // FO_TK),
        in_specs=[
            pl.BlockSpec((FO_TM, FO_TK), lambda i, k: (i, k)),
            pl.BlockSpec((FO_TK, D_MODEL), lambda i, k: (k, 0)),
            pl.BlockSpec((FO_TM, D_MODEL), lambda i, k: (i, 0)),
            pl.BlockSpec((1, D_MODEL), lambda i, k: (0, 0)),
        ],
        out_specs=pl.BlockSpec((FO_TM, D_MODEL), lambda i, k: (i, 0)),
        out_shape=jax.ShapeDtypeStruct((t_rows, D_MODEL), F32),
        scratch_shapes=[pltpu.VMEM((FO_TM, D_MODEL), F32)],
        compiler_params=_cparams(("parallel", "arbitrary")),
    )(act, w_ffn_out, x1, norm_w)


def _rope_tables(seq):
    inv = ROPE_THETA ** (-jnp.arange(0, A_DH, 2, dtype=F32) / A_DH)
    ang = jnp.arange(seq, dtype=F32)[:, None] * inv[None, :]
    cos = jnp.cos(ang)
    sin = jnp.sin(ang)
    reps = LANES // A_DH
    cos_t = jnp.tile(jnp.concatenate([cos, cos], axis=1), (1, reps))
    sin_t = jnp.tile(jnp.concatenate([-sin, sin], axis=1), (1, reps))
    return cos_t, sin_t


def kernel(x, norm1_w, w_in, b_igate, b_fgate, b_branch_gate, mlstm_norm_w, lam_q1, lam_k1, lam_q2, lam_k2, attn_norm_w, w_branch_m, w_branch_a, w_out, norm2_w, w_ffn_in, w_ffn_out, final_norm_w):
    batch, seq, d = x.shape
    depth = w_in.shape[0]
    assert d == D_MODEL and depth == 1 and seq % IN_TM == 0 and seq % AT_TQ == 0
    t_rows = batch * seq
    x2 = x.reshape(t_rows, d)
    cos_t, sin_t = _rope_tables(seq)

    l = 0
    w = w_in[l]
    w_main = jnp.concatenate([w[:, :GATE_OFF], w[:, GATE_OFF + N_GATE:]], axis=1).astype(BF16)
    w_gate = jnp.pad(w[:, GATE_OFF:GATE_OFF + N_GATE], ((0, 0), (0, LANES - N_GATE))).astype(BF16)
    gate_bias = jnp.stack([b_igate[l], b_fgate[l]], axis=1).reshape(N_GATE).astype(F32)
    gate_bias = jnp.pad(gate_bias, (0, LANES - N_GATE))

    proj, gates, vt = _inproj(x2, norm1_w[l].reshape(1, d), w_main, w_gate,
                              b_branch_gate[l].reshape(1, -1), cos_t, sin_t, seq)
    hf, hb = _mlstm(proj, gates, gate_bias.reshape(1, LANES), gate_bias.reshape(LANES, 1), batch, seq)
    ha = _attention(proj, vt, lam_q1[l].reshape(1, A_DH), lam_k1[l].reshape(1, A_DH),
                    lam_q2[l].reshape(1, A_DH), lam_k2[l].reshape(1, A_DH),
                    attn_norm_w[l].reshape(A_DV, 1), batch, seq)
    mixed = _merge(hf, hb, proj, ha, mlstm_norm_w[l].reshape(1, M_WIDTH),
                   w_branch_m[l].astype(BF16), w_branch_a[l].astype(BF16))
    x1, h2 = _outproj(mixed, x2, w_out[l].astype(BF16), norm2_w[l].reshape(1, d))
    act = _ffn_in(h2, w_ffn_in[l].astype(BF16))
    out = _ffn_out(act, w_ffn_out[l].astype(BF16), x1, final_norm_w.reshape(1, d))
    return out.reshape(batch, seq, d)
```

```python
import functools
import math

import jax
import jax.numpy as jnp
from jax import lax
from jax.experimental import pallas as pl
from jax.experimental.pallas import tpu as pltpu

F32 = jnp.float32
BF16 = jnp.bfloat16

D_MODEL = 2048
M_HEADS = 4
M_DQK = 128
M_DV = 256
M_CHUNK = 128
GATE_CAP = 15.0
A_HEADS = 8
A_DH = 64
A_DV = 2 * A_DH
ROPE_THETA = 10000.0
D_FF = 5632
EPS = 1e-6
M_WIDTH = M_HEADS * M_DV
A_WIDTH = A_HEADS * A_DV
LAM_INIT = 0.8 - 0.6 * math.exp(-0.3 * 0)

N_MAIN = 10240
GATE_OFF = 3072
N_GATE = 4 * M_HEADS
LANES = 128

VMEM_LIMIT = 56 * 1024 * 1024


def _cparams(sem, flags=None):
    return pltpu.CompilerParams(dimension_semantics=sem, vmem_limit_bytes=VMEM_LIMIT, flags=flags)


def _dot(a, b):
    return jnp.dot(a, b, preferred_element_type=F32)


def _dot_nt(a, b):
    return lax.dot_general(a, b, (((1,), (1,)), ((), ())), preferred_element_type=F32)


def _dot_tn(a, b):
    return lax.dot_general(a, b, (((0,), (0,)), ((), ())), preferred_element_type=F32)


def _sigmoid(x):
    return 1.0 / (1.0 + jnp.exp(-x))


IN_TM = 1024
IN_TN = 512
AV_J0 = 10
Q_SCALE = (A_DH ** -0.5) * math.log2(math.e)


def _rope_tile(acc, cos, sin_signed):
    lane = lax.broadcasted_iota(jnp.int32, (1, LANES), 1)
    first_half = (lane % A_DH) < (A_DH // 2)
    outs = []
    for c in range(IN_TN // LANES):
        t = acc[:, c * LANES:(c + 1) * LANES]
        partner = jnp.where(first_half,
                            pltpu.roll(t, LANES - A_DH // 2, axis=1),
                            pltpu.roll(t, A_DH // 2, axis=1))
        outs.append(t * cos + partner * sin_signed)
    return jnp.concatenate(outs, axis=1)


def _inproj_kernel(x_ref, nw_ref, w_ref, wg_ref, bg_ref, cos_ref, sin_ref,
                   out_ref, gate_ref, vt_ref, hn_ref):
    j = pl.program_id(1)

    @pl.when(j == 0)
    def _():
        x = x_ref[...]
        ms = jnp.mean(x * x, axis=-1, keepdims=True)
        hn = (x * lax.rsqrt(ms + EPS) * nw_ref[...]).astype(BF16)
        hn_ref[...] = hn
        gate_ref[...] = _dot(hn, wg_ref[...])

    acc = _dot(hn_ref[...], w_ref[...])

    is_plain = (j == 0) | (j == 2) | (j == 3)

    @pl.when(is_plain)
    def _():
        out_ref[...] = acc.astype(BF16)

    @pl.when((j == 10) | (j == 11))
    def _():
        out_ref[...] = acc.astype(BF16)
        vt_ref[0] = acc.T.astype(BF16)

    @pl.when(j == 1)
    def _():
        out_ref[...] = (acc * (M_DQK ** -0.5)).astype(BF16)

    @pl.when((j == 4) | (j == 5))
    def _():
        out_ref[...] = _sigmoid(acc).astype(BF16)

    @pl.when((j == 6) | (j == 7))
    def _():
        r = _rope_tile(acc, cos_ref[...], sin_ref[...])
        out_ref[...] = (r * Q_SCALE).astype(BF16)

    @pl.when((j == 8) | (j == 9))
    def _():
        out_ref[...] = _rope_tile(acc, cos_ref[...], sin_ref[...]).astype(BF16)

    @pl.when(j >= 12)
    def _():
        out_ref[...] = _sigmoid(acc + bg_ref[...]).astype(BF16)


def _inproj(x2, norm_w, w_main, w_gate, b_gate, cos_t, sin_t, seq):
    t_rows = x2.shape[0]
    nj = N_MAIN // IN_TN
    s_blocks = seq // IN_TM
    return pl.pallas_call(
        _inproj_kernel,
        name="inproj",
        grid=(t_rows // IN_TM, nj),
        in_specs=[
            pl.BlockSpec((IN_TM, D_MODEL), lambda i, j: (i, 0)),
            pl.BlockSpec((1, D_MODEL), lambda i, j: (0, 0)),
            pl.BlockSpec((D_MODEL, IN_TN), lambda i, j: (0, j)),
            pl.BlockSpec((D_MODEL, LANES), lambda i, j: (0, 0)),
            pl.BlockSpec((1, IN_TN), lambda i, j: (0, jnp.maximum(j - 12, 0))),
            pl.BlockSpec((IN_TM, LANES), lambda i, j: (i % s_blocks, 0)),
            pl.BlockSpec((IN_TM, LANES), lambda i, j: (i % s_blocks, 0)),
        ],
        out_specs=[
            pl.BlockSpec((IN_TM, IN_TN), lambda i, j: (i, j)),
            pl.BlockSpec((IN_TM, LANES), lambda i, j: (i, 0)),
            pl.BlockSpec((1, IN_TN, IN_TM),
                         lambda i, j: (i // s_blocks, jnp.clip(j - AV_J0, 0, 1), i % s_blocks)),
        ],
        out_shape=[
            jax.ShapeDtypeStruct((t_rows, N_MAIN), BF16),
            jax.ShapeDtypeStruct((t_rows, LANES), F32),
            jax.ShapeDtypeStruct((t_rows // seq, A_WIDTH, seq), BF16),
        ],
        scratch_shapes=[pltpu.VMEM((IN_TM, D_MODEL), BF16)],
        compiler_params=_cparams(("parallel", "arbitrary")),
    )(x2, norm_w, w_main, w_gate, b_gate, cos_t, sin_t)


L = M_CHUNK
DV_EXT = M_DV + LANES


def _softcap(t):
    return GATE_CAP * jnp.tanh(t / GATE_CAP)


def _log_sigmoid(t):
    return jnp.minimum(t, 0.0) - jnp.log(1.0 + jnp.exp(-jnp.abs(t)))


def _gate_act(pre, is_forget):
    c = _softcap(pre)
    return jnp.where(is_forget, _log_sigmoid(c), c)


def _split_dot(a, b, a_is_exact):
    if a_is_exact:
        hi = b.astype(BF16)
        lo = (b - hi.astype(F32)).astype(BF16)
        ab = a.astype(BF16)
        return _dot(ab, hi) + _dot(ab, lo)
    hi = a.astype(BF16)
    lo = (a - hi.astype(F32)).astype(BF16)
    bb = b.astype(BF16)
    return _dot(hi, bb) + _dot(lo, bb)


def _mlstm_kernel(qf_ref, kf_ref, vf_ref, gf_ref, qb_ref, kb_ref, vb_ref, gb_ref,
                  brow_ref, bcol_ref, hf_ref, hb_ref, c_ref, m_ref):
    step = pl.program_id(1)

    @pl.when(step == 0)
    def _():
        c_ref[...] = jnp.zeros_like(c_ref)
        m_ref[...] = jnp.zeros_like(m_ref)

    row = lax.broadcasted_iota(jnp.int32, (L, L), 0)
    col = lax.broadcasted_iota(jnp.int32, (L, L), 1)
    lane_id = lax.broadcasted_iota(jnp.int32, (1, LANES), 1)
    sub_id = lax.broadcasted_iota(jnp.int32, (LANES, 1), 0)
    forget_lane = (lane_id % 8) >= 4
    forget_sub = (sub_id % 8) >= 4
    ones_ext = jnp.ones((L, LANES), BF16)

    for d, (q_ref, k_ref, v_ref, g_ref, h_ref) in enumerate(
            ((qf_ref, kf_ref, vf_ref, gf_ref, hf_ref), (qb_ref, kb_ref, vb_ref, gb_ref, hb_ref))):
        visible = (row >= col) if d == 0 else (col >= row)
        vis_f = visible.astype(F32)

        g = g_ref[...]
        g_t = g.T
        act_c = _gate_act(g + brow_ref[...], forget_lane)
        act_r = _gate_act(g_t + bcol_ref[...], forget_sub)
        cum_c = _split_dot(vis_f, act_c, True)
        cum_r = _split_dot(act_r, vis_f.T, False)

        for h in range(M_HEADS):
            idx = d * M_HEADS + h
            ci = d * 8 + h
            cf = d * 8 + 4 + h
            bc = cum_c[:, cf:cf + 1]
            br = cum_r[cf:cf + 1, :]
            igc = act_c[:, ci:ci + 1]
            igr = act_r[ci:ci + 1, :]
            b_last = br[:, L - 1:L] if d == 0 else br[:, 0:1]
            m_old = m_ref[idx][0:1, 0:1]

            q = q_ref[:, h * M_DQK:(h + 1) * M_DQK]
            k = k_ref[:, h * M_DQK:(h + 1) * M_DQK]
            v_ext = jnp.concatenate([v_ref[:, h * M_DV:(h + 1) * M_DV], ones_ext], axis=1)

            dmat = jnp.where(visible, bc - br + igr, -jnp.inf)
            inter = bc + m_old
            m_t = jnp.maximum(inter, jnp.max(dmat, axis=1, keepdims=True))
            wts = jnp.exp(dmat - m_t)
            a = jnp.exp(inter - m_t)
            s = _dot_nt(q, k) * wts
            c_old = c_ref[idx]
            comb = a * _dot(q, c_old.astype(BF16)) + _dot(s.astype(BF16), v_ext)
            num = comb[:, :M_DV]
            den = comb[:, M_DV:M_DV + 1]
            hval = num / jnp.maximum(jnp.abs(den), jnp.exp(-m_t))
            h_ref[:, h * M_DV:(h + 1) * M_DV] = hval

            g_row = b_last - br + igr
            m_new = jnp.maximum(b_last + m_old, jnp.max(g_row, axis=1, keepdims=True))
            decay = jnp.exp(b_last + m_old - m_new)
            wk = jnp.exp(b_last - bc + igc - m_new)
            kw = (k.astype(F32) * wk).astype(BF16)
            c_ref[idx] = decay * c_old + _dot_tn(kw, v_ext)
            m_ref[idx] = jnp.broadcast_to(m_new, (8, LANES))


def _mlstm(proj, gates, bias_row, bias_col, batch, seq):
    t_rows = proj.shape[0]
    nc = seq // L
    fwd = lambda b, c: b * nc + c
    bwd = lambda b, c: b * nc + (nc - 1 - c)
    qk_w = M_HEADS * M_DQK
    in_specs = []
    for ch in (fwd, bwd):
        in_specs += [
            pl.BlockSpec((L, qk_w), lambda b, c, ch=ch: (ch(b, c), 0)),
            pl.BlockSpec((L, qk_w), lambda b, c, ch=ch: (ch(b, c), 1)),
            pl.BlockSpec((L, M_WIDTH), lambda b, c, ch=ch: (ch(b, c), 1)),
            pl.BlockSpec((L, LANES), lambda b, c, ch=ch: (ch(b, c), 0)),
        ]
    in_specs += [pl.BlockSpec((1, LANES), lambda b, c: (0, 0)),
                 pl.BlockSpec((LANES, 1), lambda b, c: (0, 0))]
    return pl.pallas_call(
        _mlstm_kernel,
        name="mlstm",
        grid=(batch, nc),
        in_specs=in_specs,
        out_specs=[pl.BlockSpec((L, M_WIDTH), lambda b, c: (fwd(b, c), 0)),
                   pl.BlockSpec((L, M_WIDTH), lambda b, c: (bwd(b, c), 0))],
        out_shape=[jax.ShapeDtypeStruct((t_rows, M_WIDTH), F32)] * 2,
        scratch_shapes=[pltpu.VMEM((2 * M_HEADS, M_DQK, DV_EXT), F32),
                        pltpu.VMEM((2 * M_HEADS, 8, LANES), F32)],
        compiler_params=_cparams(("parallel", "arbitrary")),
    )(proj, proj, proj, gates, proj, proj, proj, gates, bias_row, bias_col)


AT_TQ = 512
AT_TK = 512
AQ_BLK = 3072 // LANES
AK_BLK = 4096 // LANES


def _attn_kernel(q_ref, qn_ref, k_ref, vt_ref, lq1_ref, lk1_ref, lq2_ref, lk2_ref, nw_ref,
                 o_ref, acc1_ref, acc2_ref, sa1_ref, sa2_ref, sb1_ref, sb2_ref, mba_ref):
    seq = k_ref.shape[0]
    nblk = seq // AT_TK
    qi = pl.program_id(2)
    lane = lax.broadcasted_iota(jnp.int32, (1, LANES), 1)

    def split_maps(q):
        zero = jnp.zeros_like(q)
        return jnp.where(lane < A_DH, q, zero), jnp.where(lane >= A_DH, q, zero)

    q_cur = split_maps(q_ref[...])
    acc1_ref[...] = jnp.zeros_like(acc1_ref)
    acc2_ref[...] = jnp.zeros_like(acc2_ref)

    def produce(i, qs, s1_ref, s2_ref):
        off = pl.multiple_of(i * AT_TK, AT_TK)
        kblk = k_ref[pl.ds(off, AT_TK), :]
        s1 = _dot_nt(kblk, qs[0])
        s1_ref[...] = s1
        s2 = _dot_nt(kblk, qs[1])
        s2_ref[...] = s2
        return jnp.max(s1, axis=0, keepdims=True), jnp.max(s2, axis=0, keepdims=True)

    def consume_map(s_ref, vtblk, mb, m, l, acc_ref):
        m_new = jnp.maximum(m, mb)
        alpha = jnp.exp2(m - m_new)
        p = jnp.exp2(s_ref[...] - m_new)
        l_new = alpha * l + jnp.sum(p, axis=0, keepdims=True)
        acc_ref[...] = alpha * acc_ref[...] + _dot(vtblk, p.astype(BF16))
        return m_new, l_new

    def consume(i, s1_ref, s2_ref, mb, stats):
        m1, l1, m2, l2 = stats
        off = pl.multiple_of(i * AT_TK, AT_TK)
        vtblk = vt_ref[0, :, pl.ds(off, AT_TK)]
        m1, l1 = consume_map(s1_ref, vtblk, mb[0], m1, l1, acc1_ref)
        m2, l2 = consume_map(s2_ref, vtblk, mb[1], m2, l2, acc2_ref)
        return m1, l1, m2, l2

    @pl.when(qi == 0)
    def _():
        mb = produce(0, q_cur, sa1_ref, sa2_ref)
        mba_ref[0:1, :] = mb[0]
        mba_ref[1:2, :] = mb[1]

    def body(j, carry):
        mb_a, stats = carry[:2], carry[2:]
        mb_b = produce(2 * j + 1, q_cur, sb1_ref, sb2_ref)
        stats = consume(2 * j, sa1_ref, sa2_ref, mb_a, stats)
        mb_a = produce(2 * j + 2, q_cur, sa1_ref, sa2_ref)
        stats = consume(2 * j + 1, sb1_ref, sb2_ref, mb_b, stats)
        return (*mb_a, *stats)

    neg = jnp.full((1, AT_TQ), -jnp.inf, F32)
    zer = jnp.zeros((1, AT_TQ), F32)
    carry = lax.fori_loop(0, nblk // 2 - 1, body,
                          (mba_ref[0:1, :], mba_ref[1:2, :], neg, zer, neg, zer))
    mb_a, stats = carry[:2], carry[2:]
    mb_b = produce(nblk - 1, q_cur, sb1_ref, sb2_ref)
    stats = consume(nblk - 2, sa1_ref, sa2_ref, mb_a, stats)
    mb_next = produce(0, split_maps(qn_ref[...]), sa1_ref, sa2_ref)
    mba_ref[0:1, :] = mb_next[0]
    mba_ref[1:2, :] = mb_next[1]
    m1, l1, m2, l2 = consume(nblk - 1, sb1_ref, sb2_ref, mb_b, stats)

    lam = (jnp.exp(jnp.sum(lq1_ref[...] * lk1_ref[...], axis=1, keepdims=True))
           - jnp.exp(jnp.sum(lq2_ref[...] * lk2_ref[...], axis=1, keepdims=True))
           + LAM_INIT)
    o = acc1_ref[...] / l1 - lam * (acc2_ref[...] / l2)
    ms = jnp.mean(o * o, axis=0, keepdims=True)
    y = o * lax.rsqrt(ms + EPS) * nw_ref[...] * (1.0 - LAM_INIT)
    o_ref[...] = y.T.astype(BF16)


def _attention(proj, vt, lq1, lk1, lq2, lk2, norm_w, batch, seq):
    t_rows = proj.shape[0]
    nq = seq // AT_TQ
    small = pl.BlockSpec((1, A_DH), lambda b, h, i: (0, 0))
    return pl.pallas_call(
        _attn_kernel,
        name="attention",
        grid=(batch, A_HEADS, nq),
        in_specs=[
            pl.BlockSpec((AT_TQ, LANES), lambda b, h, i: (b * nq + i, AQ_BLK + h)),
            pl.BlockSpec((AT_TQ, LANES),
                         lambda b, h, i: (b * nq + jnp.minimum(i + 1, nq - 1), AQ_BLK + h)),
            pl.BlockSpec((seq, LANES), lambda b, h, i: (b, AK_BLK + h)),
            pl.BlockSpec((1, A_DV, seq), lambda b, h, i: (b, h, 0)),
            small, small, small, small,
            pl.BlockSpec((A_DV, 1), lambda b, h, i: (0, 0)),
        ],
        out_specs=pl.BlockSpec((AT_TQ, LANES), lambda b, h, i: (b * nq + i, h)),
        out_shape=jax.ShapeDtypeStruct((t_rows, A_WIDTH), BF16),
        scratch_shapes=([pltpu.VMEM((A_DV, AT_TQ), F32)] * 2 + [pltpu.VMEM((AT_TK, AT_TQ), F32)] * 4
                        + [pltpu.VMEM((8, AT_TQ), F32)]),
        compiler_params=_cparams(("arbitrary", "arbitrary", "arbitrary")),
    )(proj, proj, proj, vt, lq1, lk1, lq2, lk2, norm_w)


MG_TM = 256


def _merge_kernel(hf_ref, hb_ref, mo_ref, ha_ref, gm_ref, ga_ref, nw_ref, wm_ref, wa_ref, out_ref):
    hm = hf_ref[...] + hb_ref[...]
    parts = []
    for h in range(M_HEADS):
        seg = hm[:, h * M_DV:(h + 1) * M_DV]
        ms = jnp.mean(seg * seg, axis=-1, keepdims=True)
        parts.append(seg * lax.rsqrt(ms + EPS))
    hn = jnp.concatenate(parts, axis=1) * nw_ref[...]
    hn = (hn * mo_ref[...].astype(F32)).astype(BF16)
    branch_m = _dot(hn, wm_ref[...])
    branch_a = _dot(ha_ref[...], wa_ref[...])
    mixed = gm_ref[...].astype(F32) * branch_m + ga_ref[...].astype(F32) * branch_a
    out_ref[...] = mixed.astype(BF16)


def _merge(hf, hb, proj, ha, norm_w, w_m, w_a):
    t_rows = hf.shape[0]
    row = lambda i: (i, 0)
    const = lambda i: (0, 0)
    return pl.pallas_call(
        _merge_kernel,
        name="merge",
        grid=(t_rows // MG_TM,),
        in_specs=[
            pl.BlockSpec((MG_TM, M_WIDTH), row),
            pl.BlockSpec((MG_TM, M_WIDTH), row),
            pl.BlockSpec((MG_TM, M_WIDTH), lambda i: (i, 2)),
            pl.BlockSpec((MG_TM, A_WIDTH), row),
            pl.BlockSpec((MG_TM, D_MODEL), lambda i: (i, 3)),
            pl.BlockSpec((MG_TM, D_MODEL), lambda i: (i, 4)),
            pl.BlockSpec((1, M_WIDTH), const),
            pl.BlockSpec((M_WIDTH, D_MODEL), const),
            pl.BlockSpec((A_WIDTH, D_MODEL), const),
        ],
        out_specs=pl.BlockSpec((MG_TM, D_MODEL), row),
        out_shape=jax.ShapeDtypeStruct((t_rows, D_MODEL), BF16),
        compiler_params=_cparams(("parallel",)),
    )(hf, hb, proj, ha, proj, proj, norm_w, w_m, w_a)


OP_TM = 256


def _outproj_kernel(mixed_ref, x_ref, w_ref, nw_ref, x1_ref, h2_ref):
    x1 = x_ref[...] + _dot(mixed_ref[...], w_ref[...])
    x1_ref[...] = x1
    ms = jnp.mean(x1 * x1, axis=-1, keepdims=True)
    h2_ref[...] = (x1 * lax.rsqrt(ms + EPS) * nw_ref[...]).astype(BF16)


def _outproj(mixed, x2, w_out, norm_w):
    t_rows = x2.shape[0]
    row = lambda i: (i, 0)
    const = lambda i: (0, 0)
    return pl.pallas_call(
        _outproj_kernel,
        name="outproj",
        grid=(t_rows // OP_TM,),
        in_specs=[
            pl.BlockSpec((OP_TM, D_MODEL), row),
            pl.BlockSpec((OP_TM, D_MODEL), row),
            pl.BlockSpec((D_MODEL, D_MODEL), const),
            pl.BlockSpec((1, D_MODEL), const),
        ],
        out_specs=[pl.BlockSpec((OP_TM, D_MODEL), row), pl.BlockSpec((OP_TM, D_MODEL), row)],
        out_shape=[jax.ShapeDtypeStruct((t_rows, D_MODEL), F32),
                   jax.ShapeDtypeStruct((t_rows, D_MODEL), BF16)],
        compiler_params=_cparams(("parallel",)),
    )(mixed, x2, w_out, norm_w)


FI_TM = 1024
FI_TN = 512


def _ffn_in_kernel(h_ref, wg_ref, wu_ref, out_ref):
    h = h_ref[...]
    gate = _dot(h, wg_ref[...])
    up = _dot(h, wu_ref[...])
    out_ref[...] = (gate * _sigmoid(gate) * up).astype(BF16)


def _ffn_in(h2, w_ffn_in):
    t_rows = h2.shape[0]
    nj = D_FF // FI_TN
    return pl.pallas_call(
        _ffn_in_kernel,
        name="ffn_in",
        grid=(t_rows // FI_TM, nj),
        in_specs=[
            pl.BlockSpec((FI_TM, D_MODEL), lambda i, j: (i, 0)),
            pl.BlockSpec((D_MODEL, FI_TN), lambda i, j: (0, j)),
            pl.BlockSpec((D_MODEL, FI_TN), lambda i, j: (0, nj + j)),
        ],
        out_specs=pl.BlockSpec((FI_TM, FI_TN), lambda i, j: (i, j)),
        out_shape=jax.ShapeDtypeStruct((t_rows, D_FF), BF16),
        compiler_params=_cparams(("parallel", "arbitrary")),
    )(h2, w_ffn_in, w_ffn_in)


FO_TM = 512
FO_TK = 1408


def _ffn_out_kernel(act_ref, w_ref, x1_ref, nw_ref, out_ref, acc_ref):
    kk = pl.program_id(1)

    @pl.when(kk == 0)
    def _():
        acc_ref[...] = x1_ref[...]

    acc_ref[...] += _dot(act_ref[...], w_ref[...])

    @pl.when(kk == pl.num_programs(1) - 1)
    def _():
        x2 = acc_ref[...]
        ms = jnp.mean(x2 * x2, axis=-1, keepdims=True)
        out_ref[...] = x2 * lax.rsqrt(ms + EPS) * nw_ref[...]


def _ffn_out(act, w_ffn_out, x1, norm_w):
    t_rows = x1.shape[0]
    return pl.pallas_call(
        _ffn_out_kernel,
        name="ffn_out",
        grid=(t_rows // FO_TM, D_FF // FO_TK),
        in_specs=[
            pl.BlockSpec((FO_TM, FO_TK), lambda i, k: (i, k)),
            pl.BlockSpec((FO_TK, D_MODEL), lambda i, k: (k, 0)),
            pl.BlockSpec((FO_TM, D_MODEL), lambda i, k: (i, 0)),
            pl.BlockSpec((1, D_MODEL), lambda i, k: (0, 0)),
        ],
        out_specs=pl.BlockSpec((FO_TM, D_MODEL), lambda i, k: (i, 0)),
        out_shape=jax.ShapeDtypeStruct((t_rows, D_MODEL), F32),
        scratch_shapes=[pltpu.VMEM((FO_TM, D_MODEL), F32)],
        compiler_params=_cparams(("parallel", "arbitrary")),
    )(act, w_ffn_out, x1, norm_w)


def _rope_tables(seq):
    inv = ROPE_THETA ** (-jnp.arange(0, A_DH, 2, dtype=F32) / A_DH)
    ang = jnp.arange(seq, dtype=F32)[:, None] * inv[None, :]
    cos = jnp.cos(ang)
    sin = jnp.sin(ang)
    reps = LANES // A_DH
    cos_t = jnp.tile(jnp.concatenate([cos, cos], axis=1), (1, reps))
    sin_t = jnp.tile(jnp.concatenate([-sin, sin], axis=1), (1, reps))
    return cos_t, sin_t


def kernel(x, norm1_w, w_in, b_igate, b_fgate, b_branch_gate, mlstm_norm_w, lam_q1, lam_k1, lam_q2, lam_k2, attn_norm_w, w_branch_m, w_branch_a, w_out, norm2_w, w_ffn_in, w_ffn_out, final_norm_w):
    batch, seq, d = x.shape
    depth = w_in.shape[0]
    assert d == D_MODEL and depth == 1 and seq % IN_TM == 0 and seq % AT_TQ == 0
    t_rows = batch * seq
    x2 = x.reshape(t_rows, d)
    cos_t, sin_t = _rope_tables(seq)

    l = 0
    w = w_in[l]
    w_main = jnp.concatenate([w[:, :GATE_OFF], w[:, GATE_OFF + N_GATE:]], axis=1).astype(BF16)
    w_gate = jnp.pad(w[:, GATE_OFF:GATE_OFF + N_GATE], ((0, 0), (0, LANES - N_GATE))).astype(BF16)
    gate_bias = jnp.stack([b_igate[l], b_fgate[l]], axis=1).reshape(N_GATE).astype(F32)
    gate_bias = jnp.pad(gate_bias, (0, LANES - N_GATE))

    proj, gates, vt = _inproj(x2, norm1_w[l].reshape(1, d), w_main, w_gate,
                              b_branch_gate[l].reshape(1, -1), cos_t, sin_t, seq)
    hf, hb = _mlstm(proj, gates, gate_bias.reshape(1, LANES), gate_bias.reshape(LANES, 1), batch, seq)
    ha = _attention(proj, vt, lam_q1[l].reshape(1, A_DH), lam_k1[l].reshape(1, A_DH),
                    lam_q2[l].reshape(1, A_DH), lam_k2[l].reshape(1, A_DH),
                    attn_norm_w[l].reshape(A_DV, 1), batch, seq)
    mixed = _merge(hf, hb, proj, ha, mlstm_norm_w[l].reshape(1, M_WIDTH),
                   w_branch_m[l].astype(BF16), w_branch_a[l].astype(BF16))
    x1, h2 = _outproj(mixed, x2, w_out[l].astype(BF16), norm2_w[l].reshape(1, d))
    act = _ffn_in(h2, w_ffn_in[l].astype(BF16))
    out = _ffn_out(act, w_ffn_out[l].astype(BF16), x1, final_norm_w.reshape(1, d))
    return out.reshape(batch, seq, d)
```

```python
import functools
import math

import jax
import jax.numpy as jnp
from jax import lax
from jax.experimental import pallas as pl
from jax.experimental.pallas import tpu as pltpu

F32 = jnp.float32
BF16 = jnp.bfloat16

D_MODEL = 2048
M_HEADS = 4
M_DQK = 128
M_DV = 256
M_CHUNK = 128
GATE_CAP = 15.0
A_HEADS = 8
A_DH = 64
A_DV = 2 * A_DH
ROPE_THETA = 10000.0
D_FF = 5632
EPS = 1e-6
M_WIDTH = M_HEADS * M_DV
A_WIDTH = A_HEADS * A_DV
N_BRANCH_GATES = 2 * D_MODEL
LAM_INIT = 0.8 - 0.6 * math.exp(-0.3 * 0)

OFF_MQ = 0
OFF_MO = 2 * M_HEADS * M_DQK + M_WIDTH
OFF_MG = OFF_MO + M_WIDTH
N_GATE = 4 * M_HEADS
OFF_AQ = OFF_MG + N_GATE
OFF_AK = OFF_AQ + A_WIDTH
OFF_AV = OFF_AK + A_WIDTH
OFF_GT = OFF_AV + A_WIDTH
LANES = 128

VMEM_LIMIT = 56 * 1024 * 1024


def _cparams(sem):
    return pltpu.CompilerParams(dimension_semantics=sem, vmem_limit_bytes=VMEM_LIMIT)


def _dot(a, b):
    return jnp.dot(a, b, preferred_element_type=F32)


def _dot_nt(a, b):
    return lax.dot_general(a, b, (((1,), (1,)), ((), ())), preferred_element_type=F32)


def _dot_tn(a, b):
    return lax.dot_general(a, b, (((0,), (0,)), ((), ())), preferred_element_type=F32)


def _sigmoid(x):
    return 0.5 * jnp.tanh(0.5 * x) + 0.5


NORM_TM = 512
PJ_TM = 1024
PJ_TN = 512
ROW_CHUNK = 256
Q_SCALE = (A_DH ** -0.5) * math.log2(math.e)


def _rmsnorm_kernel(x_ref, w_ref, o_ref):
    x = x_ref[...]
    ms = jnp.mean(x * x, axis=-1, keepdims=True)
    o_ref[...] = (x * lax.rsqrt(ms + EPS) * w_ref[...]).astype(BF16)


def _rmsnorm(x2, norm_w):
    t_rows, d = x2.shape
    return pl.pallas_call(
        _rmsnorm_kernel,
        name="rmsnorm",
        grid=(t_rows // NORM_TM,),
        in_specs=[pl.BlockSpec((NORM_TM, d), lambda i: (i, 0)),
                  pl.BlockSpec((1, d), lambda i: (0, 0))],
        out_specs=pl.BlockSpec((NORM_TM, d), lambda i: (i, 0)),
        out_shape=jax.ShapeDtypeStruct((t_rows, d), BF16),
        compiler_params=_cparams(("parallel",)),
    )(x2, norm_w)


def _rope(acc, cos, sin_signed):
    outs = []
    for c in range(acc.shape[1] // LANES):
        t = acc[:, c * LANES:(c + 1) * LANES]
        outs.append(t * cos + pltpu.roll(t, LANES // 2, axis=1) * sin_signed)
    return jnp.concatenate(outs, axis=1)


def _proj_kernel(mode, h_ref, w_ref, *refs):
    o_ref = refs[-1]
    for r in range(PJ_TM // ROW_CHUNK):
        rows = slice(r * ROW_CHUNK, (r + 1) * ROW_CHUNK)
        acc = _dot(h_ref[rows, :], w_ref[...])
        if mode == "scale":
            o_ref[rows, :] = (acc * refs[0][...]).astype(BF16)
        elif mode == "sigmoid":
            o_ref[rows, :] = _sigmoid(acc + refs[0][...]).astype(BF16)
        elif mode == "rope":
            cos_ref, sin_ref, cs_ref = refs[:3]
            o_ref[rows, :] = (_rope(acc, cos_ref[rows, :], sin_ref[rows, :]) * cs_ref[...]).astype(BF16)
        elif mode == "transpose":
            o_ref[0, :, rows] = acc.T.astype(BF16)
        else:
            assert mode == "f32"
            o_ref[rows, :] = acc


def _proj(mode, hn, w, aux, seq):
    t_rows, d = hn.shape
    n = w.shape[1]
    tn = min(PJ_TN, n)
    s_blocks = seq // PJ_TM
    col = pl.BlockSpec((1, tn), lambda i, j: (0, j))
    pos = pl.BlockSpec((PJ_TM, LANES), lambda i, j: (i % s_blocks, 0))
    aux_specs = {"scale": [col], "sigmoid": [col], "rope": [pos, pos, col], "transpose": [], "f32": []}[mode]
    if mode == "transpose":
        out_spec = pl.BlockSpec((1, tn, PJ_TM), lambda i, j: (i // s_blocks, j, i % s_blocks))
        out_shape = jax.ShapeDtypeStruct((t_rows // seq, n, seq), BF16)
    else:
        out_spec = pl.BlockSpec((PJ_TM, tn), lambda i, j: (i, j))
        out_shape = jax.ShapeDtypeStruct((t_rows, n), F32 if mode == "f32" else BF16)
    return pl.pallas_call(
        functools.partial(_proj_kernel, mode),
        name="proj_" + mode,
        grid=(t_rows // PJ_TM, n // tn),
        in_specs=[pl.BlockSpec((PJ_TM, d), lambda i, j: (i, 0)),
                  pl.BlockSpec((d, tn), lambda i, j: (0, j))] + aux_specs,
        out_specs=out_spec,
        out_shape=out_shape,
        compiler_params=_cparams(("parallel", "arbitrary")),
    )(hn, w, *aux)


L = M_CHUNK
DV_EXT = M_DV + LANES


def _softcap(t):
    return GATE_CAP * jnp.tanh(t / GATE_CAP)


def _log_sigmoid(t):
    return jnp.minimum(t, 0.0) - jnp.log(1.0 + jnp.exp(-jnp.abs(t)))


def _gate_act(pre, is_forget):
    c = _softcap(pre)
    return jnp.where(is_forget, _log_sigmoid(c), c)


def _split_dot(a, b, a_is_exact):
    if a_is_exact:
        hi = b.astype(BF16)
        lo = (b - hi.astype(F32)).astype(BF16)
        ab = a.astype(BF16)
        return _dot(ab, hi) + _dot(ab, lo)
    hi = a.astype(BF16)
    lo = (a - hi.astype(F32)).astype(BF16)
    bb = b.astype(BF16)
    return _dot(hi, bb) + _dot(lo, bb)


def _mlstm_kernel(qf_ref, kf_ref, vf_ref, gf_ref, qb_ref, kb_ref, vb_ref, gb_ref,
                  brow_ref, bcol_ref, hf_ref, hb_ref, c_ref, m_ref):
    step = pl.program_id(1)

    @pl.when(step == 0)
    def _():
        c_ref[...] = jnp.zeros_like(c_ref)
        m_ref[...] = jnp.zeros_like(m_ref)

    row = lax.broadcasted_iota(jnp.int32, (L, L), 0)
    col = lax.broadcasted_iota(jnp.int32, (L, L), 1)
    lane_id = lax.broadcasted_iota(jnp.int32, (1, LANES), 1)
    sub_id = lax.broadcasted_iota(jnp.int32, (LANES, 1), 0)
    forget_lane = (lane_id % 8) >= 4
    forget_sub = (sub_id % 8) >= 4
    ones_ext = jnp.ones((L, LANES), BF16)

    for d, (q_ref, k_ref, v_ref, g_ref, h_ref) in enumerate(
            ((qf_ref, kf_ref, vf_ref, gf_ref, hf_ref), (qb_ref, kb_ref, vb_ref, gb_ref, hb_ref))):
        visible = (row >= col) if d == 0 else (col >= row)
        vis_f = visible.astype(F32)

        g = g_ref[...]
        g_t = g.T
        act_c = _gate_act(g + brow_ref[...], forget_lane)
        act_r = _gate_act(g_t + bcol_ref[...], forget_sub)
        cum_c = _split_dot(vis_f, act_c, True)
        cum_r = _split_dot(act_r, vis_f.T, False)

        for h in range(M_HEADS):
            idx = d * M_HEADS + h
            ci = d * 8 + h
            cf = d * 8 + 4 + h
            bc = cum_c[:, cf:cf + 1]
            br = cum_r[cf:cf + 1, :]
            igc = act_c[:, ci:ci + 1]
            igr = act_r[ci:ci + 1, :]
            b_last = br[:, L - 1:L] if d == 0 else br[:, 0:1]
            m_old = m_ref[idx][0:1, 0:1]

            q = q_ref[:, h * M_DQK:(h + 1) * M_DQK]
            k = k_ref[:, h * M_DQK:(h + 1) * M_DQK]
            v_ext = jnp.concatenate([v_ref[:, h * M_DV:(h + 1) * M_DV], ones_ext], axis=1)

            dmat = jnp.where(visible, bc - br + igr, -jnp.inf)
            inter = bc + m_old
            m_t = jnp.maximum(inter, jnp.max(dmat, axis=1, keepdims=True))
            wts = jnp.exp(dmat - m_t)
            a = jnp.exp(inter - m_t)
            s = _dot_nt(q, k) * wts
            c_old = c_ref[idx]
            comb = a * _dot(q, c_old.astype(BF16)) + _dot(s.astype(BF16), v_ext)
            num = comb[:, :M_DV]
            den = comb[:, M_DV:M_DV + 1]
            hval = num / jnp.maximum(jnp.abs(den), jnp.exp(-m_t))
            h_ref[:, h * M_DV:(h + 1) * M_DV] = hval

            g_row = b_last - br + igr
            m_new = jnp.maximum(b_last + m_old, jnp.max(g_row, axis=1, keepdims=True))
            decay = jnp.exp(b_last + m_old - m_new)
            wk = jnp.exp(b_last - bc + igc - m_new)
            kw = (k.astype(F32) * wk).astype(BF16)
            c_ref[idx] = decay * c_old + _dot_tn(kw, v_ext)
            m_ref[idx] = jnp.broadcast_to(m_new, (8, LANES))


def _mlstm(qkv, gates, bias_row, bias_col, batch, seq):
    t_rows = qkv.shape[0]
    nc = seq // L
    fwd = lambda b, c: b * nc + c
    bwd = lambda b, c: b * nc + (nc - 1 - c)
    qk_w = M_HEADS * M_DQK
    in_specs = []
    for ch in (fwd, bwd):
        in_specs += [
            pl.BlockSpec((L, qk_w), lambda b, c, ch=ch: (ch(b, c), 0)),
            pl.BlockSpec((L, qk_w), lambda b, c, ch=ch: (ch(b, c), 1)),
            pl.BlockSpec((L, M_WIDTH), lambda b, c, ch=ch: (ch(b, c), 1)),
            pl.BlockSpec((L, LANES), lambda b, c, ch=ch: (ch(b, c), 0)),
        ]
    in_specs += [pl.BlockSpec((1, LANES), lambda b, c: (0, 0)),
                 pl.BlockSpec((LANES, 1), lambda b, c: (0, 0))]
    return pl.pallas_call(
        _mlstm_kernel,
        name="mlstm",
        grid=(batch, nc),
        in_specs=in_specs,
        out_specs=[pl.BlockSpec((L, M_WIDTH), lambda b, c: (fwd(b, c), 0)),
                   pl.BlockSpec((L, M_WIDTH), lambda b, c: (bwd(b, c), 0))],
        out_shape=[jax.ShapeDtypeStruct((t_rows, M_WIDTH), F32)] * 2,
        scratch_shapes=[pltpu.VMEM((2 * M_HEADS, M_DQK, DV_EXT), F32),
                        pltpu.VMEM((2 * M_HEADS, 8, LANES), F32)],
        compiler_params=_cparams(("parallel", "arbitrary")),
    )(qkv, qkv, qkv, gates, qkv, qkv, qkv, gates, bias_row, bias_col)


AT_TQ = 512
AT_TK = 512


def _attn_kernel(q_ref, qn_ref, k_ref, vt_ref, lq1_ref, lk1_ref, lq2_ref, lk2_ref, nw_ref,
                 o_ref, acc1_ref, acc2_ref, sa1_ref, sa2_ref, sb1_ref, sb2_ref, mba_ref):
    seq = k_ref.shape[0]
    nblk = seq // AT_TK
    qi = pl.program_id(2)
    lane = lax.broadcasted_iota(jnp.int32, (1, LANES), 1)
    in_map1 = (lane % A_DH) < (A_DH // 2)

    def split_maps(q):
        zero = jnp.zeros_like(q)
        return jnp.where(in_map1, q, zero), jnp.where(in_map1, zero, q)

    q_cur = split_maps(q_ref[...])
    acc1_ref[...] = jnp.zeros_like(acc1_ref)
    acc2_ref[...] = jnp.zeros_like(acc2_ref)

    def produce(i, qs, s1_ref, s2_ref):
        off = pl.multiple_of(i * AT_TK, AT_TK)
        kblk = k_ref[pl.ds(off, AT_TK), :]
        s1 = _dot_nt(kblk, qs[0])
        s1_ref[...] = s1
        s2 = _dot_nt(kblk, qs[1])
        s2_ref[...] = s2
        return jnp.max(s1, axis=0, keepdims=True), jnp.max(s2, axis=0, keepdims=True)

    def consume_map(s_ref, vtblk, mb, m, l, acc_ref):
        m_new = jnp.maximum(m, mb)
        alpha = jnp.exp2(m - m_new)
        p = jnp.exp2(s_ref[...] - m_new)
        l_new = alpha * l + jnp.sum(p, axis=0, keepdims=True)
        acc_ref[...] = alpha * acc_ref[...] + _dot(vtblk, p.astype(BF16))
        return m_new, l_new

    def consume(i, s1_ref, s2_ref, mb, stats):
        m1, l1, m2, l2 = stats
        off = pl.multiple_of(i * AT_TK, AT_TK)
        vtblk = vt_ref[0, :, pl.ds(off, AT_TK)]
        m1, l1 = consume_map(s1_ref, vtblk, mb[0], m1, l1, acc1_ref)
        m2, l2 = consume_map(s2_ref, vtblk, mb[1], m2, l2, acc2_ref)
        return m1, l1, m2, l2

    @pl.when(qi == 0)
    def _():
        mb = produce(0, q_cur, sa1_ref, sa2_ref)
        mba_ref[0:1, :] = mb[0]
        mba_ref[1:2, :] = mb[1]

    def body(j, carry):
        mb_a, stats = carry[:2], carry[2:]
        mb_b = produce(2 * j + 1, q_cur, sb1_ref, sb2_ref)
        stats = consume(2 * j, sa1_ref, sa2_ref, mb_a, stats)
        mb_a = produce(2 * j + 2, q_cur, sa1_ref, sa2_ref)
        stats = consume(2 * j + 1, sb1_ref, sb2_ref, mb_b, stats)
        return (*mb_a, *stats)

    neg = jnp.full((1, AT_TQ), -jnp.inf, F32)
    zer = jnp.zeros((1, AT_TQ), F32)
    carry = lax.fori_loop(0, nblk // 2 - 1, body,
                          (mba_ref[0:1, :], mba_ref[1:2, :], neg, zer, neg, zer))
    mb_a, stats = carry[:2], carry[2:]
    mb_b = produce(nblk - 1, q_cur, sb1_ref, sb2_ref)
    stats = consume(nblk - 2, sa1_ref, sa2_ref, mb_a, stats)
    mb_next = produce(0, split_maps(qn_ref[...]), sa1_ref, sa2_ref)
    mba_ref[0:1, :] = mb_next[0]
    mba_ref[1:2, :] = mb_next[1]
    m1, l1, m2, l2 = consume(nblk - 1, sb1_ref, sb2_ref, mb_b, stats)

    lam = (jnp.exp(jnp.sum(lq1_ref[...] * lk1_ref[...], axis=1, keepdims=True))
           - jnp.exp(jnp.sum(lq2_ref[...] * lk2_ref[...], axis=1, keepdims=True))
           + LAM_INIT)
    o = acc1_ref[...] / l1 - lam * (acc2_ref[...] / l2)
    ms = jnp.mean(o * o, axis=0, keepdims=True)
    y = o * lax.rsqrt(ms + EPS) * nw_ref[...] * (1.0 - LAM_INIT)
    o_ref[...] = y.T.astype(BF16)


def _attention(qk, vt, lq1, lk1, lq2, lk2, norm_w, batch, seq):
    t_rows = qk.shape[0]
    nq = seq // AT_TQ
    small = pl.BlockSpec((1, A_DH), lambda b, h, i: (0, 0))
    return pl.pallas_call(
        _attn_kernel,
        name="attention",
        grid=(batch, A_HEADS, nq),
        in_specs=[
            pl.BlockSpec((AT_TQ, LANES), lambda b, h, i: (b * nq + i, h)),
            pl.BlockSpec((AT_TQ, LANES), lambda b, h, i: (b * nq + jnp.minimum(i + 1, nq - 1), h)),
            pl.BlockSpec((seq, LANES), lambda b, h, i: (b, A_HEADS + h)),
            pl.BlockSpec((1, A_DV, seq), lambda b, h, i: (b, h, 0)),
            small, small, small, small,
            pl.BlockSpec((A_DV, 1), lambda b, h, i: (0, 0)),
        ],
        out_specs=pl.BlockSpec((AT_TQ, LANES), lambda b, h, i: (b * nq + i, h)),
        out_shape=jax.ShapeDtypeStruct((t_rows, A_WIDTH), BF16),
        scratch_shapes=([pltpu.VMEM((A_DV, AT_TQ), F32)] * 2 + [pltpu.VMEM((AT_TK, AT_TQ), F32)] * 4
                        + [pltpu.VMEM((8, AT_TQ), F32)]),
        compiler_params=_cparams(("arbitrary", "arbitrary", "arbitrary")),
    )(qk, qk, qk, vt, lq1, lk1, lq2, lk2, norm_w)


MG_TM = 256


def _merge_kernel(hf_ref, hb_ref, mo_ref, ha_ref, gm_ref, ga_ref, nw_ref, wm_ref, wa_ref, out_ref):
    hm = hf_ref[...] + hb_ref[...]
    parts = []
    for h in range(M_HEADS):
        seg = hm[:, h * M_DV:(h + 1) * M_DV]
        ms = jnp.mean(seg * seg, axis=-1, keepdims=True)
        parts.append(seg * lax.rsqrt(ms + EPS))
    hn = jnp.concatenate(parts, axis=1) * nw_ref[...]
    hn = (hn * mo_ref[...].astype(F32)).astype(BF16)
    branch_m = _dot(hn, wm_ref[...])
    branch_a = _dot(ha_ref[...], wa_ref[...])
    mixed = gm_ref[...].astype(F32) * branch_m + ga_ref[...].astype(F32) * branch_a
    out_ref[...] = mixed.astype(BF16)


def _merge(hf, hb, sig, ha, norm_w, w_m, w_a):
    t_rows = hf.shape[0]
    row = lambda i: (i, 0)
    const = lambda i: (0, 0)
    return pl.pallas_call(
        _merge_kernel,
        name="merge",
        grid=(t_rows // MG_TM,),
        in_specs=[
            pl.BlockSpec((MG_TM, M_WIDTH), row),
            pl.BlockSpec((MG_TM, M_WIDTH), row),
            pl.BlockSpec((MG_TM, M_WIDTH), lambda i: (i, N_BRANCH_GATES // M_WIDTH)),
            pl.BlockSpec((MG_TM, A_WIDTH), row),
            pl.BlockSpec((MG_TM, D_MODEL), lambda i: (i, 0)),
            pl.BlockSpec((MG_TM, D_MODEL), lambda i: (i, 1)),
            pl.BlockSpec((1, M_WIDTH), const),
            pl.BlockSpec((M_WIDTH, D_MODEL), const),
            pl.BlockSpec((A_WIDTH, D_MODEL), const),
        ],
        out_specs=pl.BlockSpec((MG_TM, D_MODEL), row),
        out_shape=jax.ShapeDtypeStruct((t_rows, D_MODEL), BF16),
        compiler_params=_cparams(("parallel",)),
    )(hf, hb, sig, ha, sig, sig, norm_w, w_m, w_a)


OP_TM = 256


def _outproj_kernel(mixed_ref, x_ref, w_ref, nw_ref, x1_ref, h2_ref):
    x1 = x_ref[...] + _dot(mixed_ref[...], w_ref[...])
    x1_ref[...] = x1
    ms = jnp.mean(x1 * x1, axis=-1, keepdims=True)
    h2_ref[...] = (x1 * lax.rsqrt(ms + EPS) * nw_ref[...]).astype(BF16)


def _outproj(mixed, x2, w_out, norm_w):
    t_rows = x2.shape[0]
    row = lambda i: (i, 0)
    const = lambda i: (0, 0)
    return pl.pallas_call(
        _outproj_kernel,
        name="outproj",
        grid=(t_rows // OP_TM,),
        in_specs=[
            pl.BlockSpec((OP_TM, D_MODEL), row),
            pl.BlockSpec((OP_TM, D_MODEL), row),
            pl.BlockSpec((D_MODEL, D_MODEL), const),
            pl.BlockSpec((1, D_MODEL), const),
        ],
        out_specs=[pl.BlockSpec((OP_TM, D_MODEL), row), pl.BlockSpec((OP_TM, D_MODEL), row)],
        out_shape=[jax.ShapeDtypeStruct((t_rows, D_MODEL), F32),
                   jax.ShapeDtypeStruct((t_rows, D_MODEL), BF16)],
        compiler_params=_cparams(("parallel",)),
    )(mixed, x2, w_out, norm_w)


FI_TM = 1024
FI_TN = 512


def _ffn_in_kernel(h_ref, wg_ref, wu_ref, out_ref):
    h = h_ref[...]
    gate = _dot(h, wg_ref[...])
    up = _dot(h, wu_ref[...])
    out_ref[...] = (gate * _sigmoid(gate) * up).astype(BF16)


def _ffn_in(h2, w_ffn_in):
    t_rows = h2.shape[0]
    nj = D_FF // FI_TN
    return pl.pallas_call(
        _ffn_in_kernel,
        name="ffn_in",
        grid=(t_rows // FI_TM, nj),
        in_specs=[
            pl.BlockSpec((FI_TM, D_MODEL), lambda i, j: (i, 0)),
            pl.BlockSpec((D_MODEL, FI_TN), lambda i, j: (0, j)),
            pl.BlockSpec((D_MODEL, FI_TN), lambda i, j: (0, nj + j)),
        ],
        out_specs=pl.BlockSpec((FI_TM, FI_TN), lambda i, j: (i, j)),
        out_shape=jax.ShapeDtypeStruct((t_rows, D_FF), BF16),
        compiler_params=_cparams(("parallel", "arbitrary")),
    )(h2, w_ffn_in, w_ffn_in)


FO_TM = 512
FO_TK = 1408


def _ffn_out_kernel(act_ref, w_ref, x1_ref, nw_ref, out_ref, acc_ref):
    kk = pl.program_id(1)

    @pl.when(kk == 0)
    def _():
        acc_ref[...] = x1_ref[...]

    acc_ref[...] += _dot(act_ref[...], w_ref[...])

    @pl.when(kk == pl.num_programs(1) - 1)
    def _():
        x2 = acc_ref[...]
        ms = jnp.mean(x2 * x2, axis=-1, keepdims=True)
        out_ref[...] = x2 * lax.rsqrt(ms + EPS) * nw_ref[...]


def _ffn_out(act, w_ffn_out, x1, norm_w):
    t_rows = x1.shape[0]
    return pl.pallas_call(
        _ffn_out_kernel,
        name="ffn_out",
        grid=(t_rows // FO_TM, D_FF // FO_TK),
        in_specs=[
            pl.BlockSpec((FO_TM, FO_TK), lambda i, k: (i, k)),
            pl.BlockSpec((FO_TK, D_MODEL), lambda i, k: (k, 0)),
            pl.BlockSpec((FO_TM, D_MODEL), lambda i, k: (i, 0)),
            pl.BlockSpec((1, D_MODEL), lambda i, k: (0, 0)),
        ],
        out_specs=pl.BlockSpec((FO_TM, D_MODEL), lambda i, k: (i, 0)),
        out_shape=jax.ShapeDtypeStruct((t_rows, D_MODEL), F32),
        scratch_shapes=[pltpu.VMEM((FO_TM, D_MODEL), F32)],
        compiler_params=_cparams(("parallel", "arbitrary")),
    )(act, w_ffn_out, x1, norm_w)


def _rope_tables(seq):
    inv = ROPE_THETA ** (-jnp.arange(0, A_DH, 2, dtype=F32) / A_DH)
    ang = jnp.arange(seq, dtype=F32)[:, None] * inv[None, :]
    cos = jnp.cos(ang)
    sin = jnp.sin(ang)
    cos_t = jnp.concatenate([cos, cos, cos, cos], axis=1)
    sin_t = jnp.concatenate([-sin, -sin, sin, sin], axis=1)
    return cos_t, sin_t


def _rotary_layout(w_seg):
    d = w_seg.shape[0]
    half = A_DH // 2
    return w_seg.reshape(d, A_HEADS, 2, 2, half).transpose(0, 1, 3, 2, 4).reshape(d, A_WIDTH)


def kernel(x, norm1_w, w_in, b_igate, b_fgate, b_branch_gate, mlstm_norm_w, lam_q1, lam_k1, lam_q2, lam_k2, attn_norm_w, w_branch_m, w_branch_a, w_out, norm2_w, w_ffn_in, w_ffn_out, final_norm_w):
    batch, seq, d = x.shape
    depth = w_in.shape[0]
    assert d == D_MODEL and depth == 1 and seq % PJ_TM == 0 and seq % AT_TQ == 0
    t_rows = batch * seq
    x2 = x.reshape(t_rows, d)
    cos_t, sin_t = _rope_tables(seq)

    l = 0
    w = w_in[l]
    w_qkv = w[:, OFF_MQ:OFF_MO].astype(BF16)
    w_sig = jnp.concatenate([w[:, OFF_GT:OFF_GT + N_BRANCH_GATES], w[:, OFF_MO:OFF_MG]], axis=1).astype(BF16)
    w_rot = jnp.concatenate([_rotary_layout(w[:, OFF_AQ:OFF_AK]),
                             _rotary_layout(w[:, OFF_AK:OFF_AV])], axis=1).astype(BF16)
    w_av = w[:, OFF_AV:OFF_GT].astype(BF16)
    w_gate = jnp.pad(w[:, OFF_MG:OFF_AQ], ((0, 0), (0, LANES - N_GATE))).astype(BF16)

    qk_w = M_HEADS * M_DQK
    scale_qkv = jnp.concatenate([jnp.ones((1, qk_w), F32), jnp.full((1, qk_w), M_DQK ** -0.5, F32),
                                 jnp.ones((1, M_WIDTH), F32)], axis=1)
    scale_rot = jnp.concatenate([jnp.full((1, A_WIDTH), Q_SCALE, F32), jnp.ones((1, A_WIDTH), F32)], axis=1)
    bias_sig = jnp.concatenate([b_branch_gate[l].astype(F32), jnp.zeros((M_WIDTH,), F32)]).reshape(1, -1)
    gate_bias = jnp.stack([b_igate[l], b_fgate[l]], axis=1).reshape(N_GATE).astype(F32)
    gate_bias = jnp.pad(gate_bias, (0, LANES - N_GATE))

    hn = _rmsnorm(x2, norm1_w[l].reshape(1, d))
    qkv = _proj("scale", hn, w_qkv, [scale_qkv], seq)
    sig = _proj("sigmoid", hn, w_sig, [bias_sig], seq)
    qk = _proj("rope", hn, w_rot, [cos_t, sin_t, scale_rot], seq)
    vt = _proj("transpose", hn, w_av, [], seq)
    gates = _proj("f32", hn, w_gate, [], seq)

    hf, hb = _mlstm(qkv, gates, gate_bias.reshape(1, LANES), gate_bias.reshape(LANES, 1), batch, seq)
    ha = _attention(qk, vt, lam_q1[l].reshape(1, A_DH), lam_k1[l].reshape(1, A_DH),
                    lam_q2[l].reshape(1, A_DH), lam_k2[l].reshape(1, A_DH),
                    attn_norm_w[l].reshape(A_DV, 1), batch, seq)
    mixed = _merge(hf, hb, sig, ha, mlstm_norm_w[l].reshape(1, M_WIDTH),
                   w_branch_m[l].astype(BF16), w_branch_a[l].astype(BF16))
    x1, h2 = _outproj(mixed, x2, w_out[l].astype(BF16), norm2_w[l].reshape(1, d))
    act = _ffn_in(h2, w_ffn_in[l].astype(BF16))
    out = _ffn_out(act, w_ffn_out[l].astype(BF16), x1, final_norm_w.reshape(1, d))
    return out.reshape(batch, seq, d)
```

```python
import functools
import math

import jax
import jax.numpy as jnp
from jax import lax
from jax.experimental import pallas as pl
from jax.experimental.pallas import tpu as pltpu

F32 = jnp.float32
BF16 = jnp.bfloat16

D_MODEL = 2048
M_HEADS = 4
M_DQK = 128
M_DV = 256
M_CHUNK = 128
GATE_CAP = 15.0
A_HEADS = 8
A_DH = 64
A_DV = 2 * A_DH
ROPE_THETA = 10000.0
D_FF = 5632
EPS = 1e-6
M_WIDTH = M_HEADS * M_DV
A_WIDTH = A_HEADS * A_DV
N_BRANCH_GATES = 2 * D_MODEL
LAM_INIT = 0.8 - 0.6 * math.exp(-0.3 * 0)

OFF_MQ = 0
OFF_MO = 2 * M_HEADS * M_DQK + M_WIDTH
OFF_MG = OFF_MO + M_WIDTH
N_GATE = 4 * M_HEADS
OFF_AQ = OFF_MG + N_GATE
OFF_AK = OFF_AQ + A_WIDTH
OFF_AV = OFF_AK + A_WIDTH
OFF_GT = OFF_AV + A_WIDTH
LANES = 128

VMEM_LIMIT = 56 * 1024 * 1024


def _cparams(sem):
    return pltpu.CompilerParams(dimension_semantics=sem, vmem_limit_bytes=VMEM_LIMIT)


def _dot(a, b):
    return jnp.dot(a, b, preferred_element_type=F32)


def _dot_nt(a, b):
    return lax.dot_general(a, b, (((1,), (1,)), ((), ())), preferred_element_type=F32)


def _dot_tn(a, b):
    return lax.dot_general(a, b, (((0,), (0,)), ((), ())), preferred_element_type=F32)


def _sigmoid(x):
    return 0.5 * jnp.tanh(0.5 * x) + 0.5


NORM_TM = 512
PJ_TM = 1024
PJ_TN = 1024
ROW_CHUNK = 256
Q_SCALE = (A_DH ** -0.5) * math.log2(math.e)


def _rmsnorm_kernel(x_ref, w_ref, o_ref):
    x = x_ref[...]
    ms = jnp.mean(x * x, axis=-1, keepdims=True)
    o_ref[...] = (x * lax.rsqrt(ms + EPS) * w_ref[...]).astype(BF16)


def _rmsnorm(x2, norm_w):
    t_rows, d = x2.shape
    return pl.pallas_call(
        _rmsnorm_kernel,
        name="rmsnorm",
        grid=(t_rows // NORM_TM,),
        in_specs=[pl.BlockSpec((NORM_TM, d), lambda i: (i, 0)),
                  pl.BlockSpec((1, d), lambda i: (0, 0))],
        out_specs=pl.BlockSpec((NORM_TM, d), lambda i: (i, 0)),
        out_shape=jax.ShapeDtypeStruct((t_rows, d), BF16),
        compiler_params=_cparams(("parallel",)),
    )(x2, norm_w)


def _rope(acc, cos, sin_signed):
    outs = []
    for c in range(acc.shape[1] // LANES):
        t = acc[:, c * LANES:(c + 1) * LANES]
        outs.append(t * cos + pltpu.roll(t, LANES // 2, axis=1) * sin_signed)
    return jnp.concatenate(outs, axis=1)


def _proj_kernel(mode, h_ref, w_ref, *refs):
    o_ref = refs[-1]
    for r in range(PJ_TM // ROW_CHUNK):
        rows = slice(r * ROW_CHUNK, (r + 1) * ROW_CHUNK)
        acc = _dot(h_ref[rows, :], w_ref[...])
        if mode == "scale":
            o_ref[rows, :] = (acc * refs[0][...]).astype(BF16)
        elif mode == "sigmoid":
            o_ref[rows, :] = _sigmoid(acc + refs[0][...]).astype(BF16)
        elif mode == "rope":
            cos_ref, sin_ref, cs_ref = refs[:3]
            o_ref[rows, :] = (_rope(acc, cos_ref[rows, :], sin_ref[rows, :]) * cs_ref[...]).astype(BF16)
        elif mode == "transpose":
            o_ref[0, :, rows] = (acc * refs[0][...]).T.astype(BF16)
        else:
            assert mode == "f32"
            o_ref[rows, :] = acc


def _proj(mode, hn, w, aux, seq, tn=PJ_TN):
    t_rows, d = hn.shape
    n = w.shape[1]
    tn = min(tn, n)
    s_blocks = seq // PJ_TM
    col = pl.BlockSpec((1, tn), lambda i, j: (0, j))
    pos = pl.BlockSpec((PJ_TM, LANES), lambda i, j: (i % s_blocks, 0))
    aux_specs = {"scale": [col], "sigmoid": [col], "rope": [pos, pos, col], "transpose": [col], "f32": []}[mode]
    if mode == "transpose":
        out_spec = pl.BlockSpec((1, tn, PJ_TM), lambda i, j: (i // s_blocks, j, i % s_blocks))
        out_shape = jax.ShapeDtypeStruct((t_rows // seq, n, seq), BF16)
    else:
        out_spec = pl.BlockSpec((PJ_TM, tn), lambda i, j: (i, j))
        out_shape = jax.ShapeDtypeStruct((t_rows, n), F32 if mode == "f32" else BF16)
    return pl.pallas_call(
        functools.partial(_proj_kernel, mode),
        name="proj_" + mode,
        grid=(t_rows // PJ_TM, n // tn),
        in_specs=[pl.BlockSpec((PJ_TM, d), lambda i, j: (i, 0)),
                  pl.BlockSpec((d, tn), lambda i, j: (0, j))] + aux_specs,
        out_specs=out_spec,
        out_shape=out_shape,
        compiler_params=_cparams(("parallel", "arbitrary")),
    )(hn, w, *aux)


L = M_CHUNK
MS_SUB = 2
DV_EXT = M_DV + LANES


def _softcap(t):
    return GATE_CAP * jnp.tanh(t / GATE_CAP)


def _log_sigmoid(t):
    return jnp.minimum(t, 0.0) - jnp.log(1.0 + jnp.exp(-jnp.abs(t)))


def _gate_act(pre, is_forget):
    c = _softcap(pre)
    return jnp.where(is_forget, _log_sigmoid(c), c)


def _split_dot(a, b, a_is_exact):
    if a_is_exact:
        hi = b.astype(BF16)
        lo = (b - hi.astype(F32)).astype(BF16)
        ab = a.astype(BF16)
        return _dot(ab, hi) + _dot(ab, lo)
    hi = a.astype(BF16)
    lo = (a - hi.astype(F32)).astype(BF16)
    bb = b.astype(BF16)
    return _dot(hi, bb) + _dot(lo, bb)


def _mlstm_kernel(qf_ref, kf_ref, vf_ref, gf_ref, qb_ref, kb_ref, vb_ref, gb_ref,
                  brow_ref, bcol_ref, hf_ref, hb_ref, c_ref, m_ref):
    step = pl.program_id(1)

    @pl.when(step == 0)
    def _():
        c_ref[...] = jnp.zeros_like(c_ref)
        m_ref[...] = jnp.zeros_like(m_ref)

    row = lax.broadcasted_iota(jnp.int32, (L, L), 0)
    col = lax.broadcasted_iota(jnp.int32, (L, L), 1)
    lane_id = lax.broadcasted_iota(jnp.int32, (1, LANES), 1)
    sub_id = lax.broadcasted_iota(jnp.int32, (LANES, 1), 0)
    forget_lane = (lane_id % 8) >= 4
    forget_sub = (sub_id % 8) >= 4
    ones_ext = jnp.ones((L, LANES), BF16)

    dirs = ((qf_ref, kf_ref, vf_ref, gf_ref, hf_ref), (qb_ref, kb_ref, vb_ref, gb_ref, hb_ref))
    for sub, d in [(sub, d) for sub in range(MS_SUB) for d in range(2)]:
        q_blk, k_blk, v_blk, g_blk, h_blk = dirs[d]
        r0 = (sub if d == 0 else MS_SUB - 1 - sub) * L
        q_ref, v_ref, g_ref, h_ref = (ref.at[r0:r0 + L, :] for ref in (q_blk, v_blk, g_blk, h_blk))
        kt_ref = k_blk.at[0, :, r0:r0 + L]
        visible = (row >= col) if d == 0 else (col >= row)
        vis_f = visible.astype(F32)

        g = g_ref[...]
        g_t = g.T
        act_c = _gate_act(g + brow_ref[...], forget_lane)
        act_r = _gate_act(g_t + bcol_ref[...], forget_sub)
        cum_c = _split_dot(vis_f, act_c, True)
        cum_r = _split_dot(act_r, vis_f.T, False)

        for h in range(M_HEADS):
            idx = d * M_HEADS + h
            ci = d * 8 + h
            cf = d * 8 + 4 + h
            bc = cum_c[:, cf:cf + 1]
            br = cum_r[cf:cf + 1, :]
            igr = act_r[ci:ci + 1, :]
            b_last = br[:, L - 1:L] if d == 0 else br[:, 0:1]
            m_old = m_ref[idx][0:1, 0:1]

            q = q_ref[:, h * M_DQK:(h + 1) * M_DQK]
            kt = kt_ref[h * M_DQK:(h + 1) * M_DQK, :]
            v_ext = jnp.concatenate([v_ref[:, h * M_DV:(h + 1) * M_DV], ones_ext], axis=1)

            dmat = jnp.where(visible, bc - br + igr, -jnp.inf)
            inter = bc + m_old
            m_t = jnp.maximum(inter, jnp.max(dmat, axis=1, keepdims=True))
            wts = jnp.exp(dmat - m_t)
            a = jnp.exp(inter - m_t)
            s = _dot(q, kt) * wts
            c_old = c_ref[idx]
            comb = a * _dot(q, c_old.astype(BF16)) + _dot(s.astype(BF16), v_ext)
            num = comb[:, :M_DV]
            den = comb[:, M_DV:M_DV + 1]
            hval = num / jnp.maximum(jnp.abs(den), jnp.exp(-m_t))
            h_ref[:, h * M_DV:(h + 1) * M_DV] = hval

            g_row = b_last - br + igr
            m_new = jnp.maximum(b_last + m_old, jnp.max(g_row, axis=1, keepdims=True))
            decay = jnp.exp(b_last + m_old - m_new)
            wk = jnp.exp(g_row - m_new)
            kw_t = (kt.astype(F32) * wk).astype(BF16)
            c_ref[idx] = decay * c_old + _dot(kw_t, v_ext)
            m_ref[idx] = jnp.broadcast_to(m_new, (8, LANES))


def _mlstm(vq, kvt, gates, bias_row, bias_col, batch, seq):
    t_rows = vq.shape[0]
    rows = MS_SUB * L
    nc = seq // rows
    fwd = lambda b, c: b * nc + c
    bwd = lambda b, c: b * nc + (nc - 1 - c)
    qk_w = M_HEADS * M_DQK
    in_specs = []
    for ch in (fwd, bwd):
        in_specs += [
            pl.BlockSpec((rows, qk_w), lambda b, c, ch=ch: (ch(b, c), M_WIDTH // qk_w)),
            pl.BlockSpec((1, qk_w, rows), lambda b, c, ch=ch: (b, 0, ch(0, c))),
            pl.BlockSpec((rows, M_WIDTH), lambda b, c, ch=ch: (ch(b, c), 0)),
            pl.BlockSpec((rows, LANES), lambda b, c, ch=ch: (ch(b, c), 0)),
        ]
    in_specs += [pl.BlockSpec((1, LANES), lambda b, c: (0, 0)),
                 pl.BlockSpec((LANES, 1), lambda b, c: (0, 0))]
    return pl.pallas_call(
        _mlstm_kernel,
        name="mlstm",
        grid=(batch, nc),
        in_specs=in_specs,
        out_specs=[pl.BlockSpec((rows, M_WIDTH), lambda b, c: (fwd(b, c), 0)),
                   pl.BlockSpec((rows, M_WIDTH), lambda b, c: (bwd(b, c), 0))],
        out_shape=[jax.ShapeDtypeStruct((t_rows, M_WIDTH), F32)] * 2,
        scratch_shapes=[pltpu.VMEM((2 * M_HEADS, M_DQK, DV_EXT), F32),
                        pltpu.VMEM((2 * M_HEADS, 8, LANES), F32)],
        compiler_params=_cparams(("parallel", "arbitrary")),
    )(vq, kvt, vq, gates, vq, kvt, vq, gates, bias_row, bias_col)


AT_TQ = 512
AT_TK = 1024


def _attn_kernel(q_ref, qn_ref, k_ref, vt_ref, lq1_ref, lk1_ref, lq2_ref, lk2_ref, nw_ref,
                 o_ref, acc1_ref, acc2_ref, sa1_ref, sa2_ref, sb1_ref, sb2_ref, mba_ref):
    seq = k_ref.shape[0]
    nblk = seq // AT_TK
    qi = pl.program_id(2)
    lane = lax.broadcasted_iota(jnp.int32, (1, LANES), 1)
    in_map1 = (lane % A_DH) < (A_DH // 2)

    def split_maps(q):
        zero = jnp.zeros_like(q)
        return jnp.where(in_map1, q, zero), jnp.where(in_map1, zero, q)

    q_cur = split_maps(q_ref[...])
    acc1_ref[...] = jnp.zeros_like(acc1_ref)
    acc2_ref[...] = jnp.zeros_like(acc2_ref)

    def produce(i, qs, s1_ref, s2_ref):
        off = pl.multiple_of(i * AT_TK, AT_TK)
        kblk = k_ref[pl.ds(off, AT_TK), :]
        s1 = _dot_nt(kblk, qs[0])
        s1_ref[...] = s1
        s2 = _dot_nt(kblk, qs[1])
        s2_ref[...] = s2
        return jnp.max(s1, axis=0, keepdims=True), jnp.max(s2, axis=0, keepdims=True)

    def consume_map(s_ref, vtblk, mb, m, l, acc_ref):
        m_new = jnp.maximum(m, mb)
        alpha = jnp.exp2(m - m_new)
        p = jnp.exp2(s_ref[...] - m_new)
        l_new = alpha * l + jnp.sum(p, axis=0, keepdims=True)
        acc_ref[...] = alpha * acc_ref[...] + _dot(vtblk, p.astype(BF16))
        return m_new, l_new

    def consume(i, s1_ref, s2_ref, mb, stats):
        m1, l1, m2, l2 = stats
        off = pl.multiple_of(i * AT_TK, AT_TK)
        vtblk = vt_ref[0, :, pl.ds(off, AT_TK)]
        m1, l1 = consume_map(s1_ref, vtblk, mb[0], m1, l1, acc1_ref)
        m2, l2 = consume_map(s2_ref, vtblk, mb[1], m2, l2, acc2_ref)
        return m1, l1, m2, l2

    @pl.when(qi == 0)
    def _():
        mb = produce(0, q_cur, sa1_ref, sa2_ref)
        mba_ref[0:1, :] = mb[0]
        mba_ref[1:2, :] = mb[1]

    def body(j, carry):
        mb_a, stats = carry[:2], carry[2:]
        mb_b = produce(2 * j + 1, q_cur, sb1_ref, sb2_ref)
        stats = consume(2 * j, sa1_ref, sa2_ref, mb_a, stats)
        mb_a = produce(2 * j + 2, q_cur, sa1_ref, sa2_ref)
        stats = consume(2 * j + 1, sb1_ref, sb2_ref, mb_b, stats)
        return (*mb_a, *stats)

    neg = jnp.full((1, AT_TQ), -jnp.inf, F32)
    zer = jnp.zeros((1, AT_TQ), F32)
    carry = lax.fori_loop(0, nblk // 2 - 1, body,
                          (mba_ref[0:1, :], mba_ref[1:2, :], neg, zer, neg, zer))
    mb_a, stats = carry[:2], carry[2:]
    mb_b = produce(nblk - 1, q_cur, sb1_ref, sb2_ref)
    stats = consume(nblk - 2, sa1_ref, sa2_ref, mb_a, stats)
    mb_next = produce(0, split_maps(qn_ref[...]), sa1_ref, sa2_ref)
    mba_ref[0:1, :] = mb_next[0]
    mba_ref[1:2, :] = mb_next[1]
    m1, l1, m2, l2 = consume(nblk - 1, sb1_ref, sb2_ref, mb_b, stats)

    lam = (jnp.exp(jnp.sum(lq1_ref[...] * lk1_ref[...], axis=1, keepdims=True))
           - jnp.exp(jnp.sum(lq2_ref[...] * lk2_ref[...], axis=1, keepdims=True))
           + LAM_INIT)
    o = acc1_ref[...] * (1.0 / l1) - acc2_ref[...] * (lam / l2)
    ms = jnp.mean(o * o, axis=0, keepdims=True)
    y = o * lax.rsqrt(ms + EPS) * nw_ref[...] * (1.0 - LAM_INIT)
    o_ref[...] = y.T.astype(BF16)


def _attention(qk, kvt, lq1, lk1, lq2, lk2, norm_w, batch, seq):
    v_blk0 = (M_HEADS * M_DQK) // A_DV
    t_rows = qk.shape[0]
    nq = seq // AT_TQ
    small = pl.BlockSpec((1, A_DH), lambda b, h, i: (0, 0))
    return pl.pallas_call(
        _attn_kernel,
        name="attention",
        grid=(batch, A_HEADS, nq),
        in_specs=[
            pl.BlockSpec((AT_TQ, LANES), lambda b, h, i: (b * nq + i, h)),
            pl.BlockSpec((AT_TQ, LANES), lambda b, h, i: (b * nq + jnp.minimum(i + 1, nq - 1), h)),
            pl.BlockSpec((seq, LANES), lambda b, h, i: (b, A_HEADS + h)),
            pl.BlockSpec((1, A_DV, seq), lambda b, h, i: (b, v_blk0 + h, 0)),
            small, small, small, small,
            pl.BlockSpec((A_DV, 1), lambda b, h, i: (0, 0)),
        ],
        out_specs=pl.BlockSpec((AT_TQ, LANES), lambda b, h, i: (b * nq + i, h)),
        out_shape=jax.ShapeDtypeStruct((t_rows, A_WIDTH), BF16),
        scratch_shapes=([pltpu.VMEM((A_DV, AT_TQ), F32)] * 2 + [pltpu.VMEM((AT_TK, AT_TQ), F32)] * 4
                        + [pltpu.VMEM((8, AT_TQ), F32)]),
        compiler_params=_cparams(("arbitrary", "arbitrary", "arbitrary")),
    )(qk, qk, qk, kvt, lq1, lk1, lq2, lk2, norm_w)


MG_TM = 256


def _merge_kernel(hf_ref, hb_ref, mo_ref, ha_ref, gm_ref, ga_ref, nw_ref, wm_ref, wa_ref, out_ref):
    hm = hf_ref[...] + hb_ref[...]
    parts = []
    for h in range(M_HEADS):
        seg = hm[:, h * M_DV:(h + 1) * M_DV]
        ms = jnp.mean(seg * seg, axis=-1, keepdims=True)
        parts.append(seg * lax.rsqrt(ms + EPS))
    hn = jnp.concatenate(parts, axis=1) * nw_ref[...]
    hn = (hn * mo_ref[...].astype(F32)).astype(BF16)
    branch_m = _dot(hn, wm_ref[...])
    branch_a = _dot(ha_ref[...], wa_ref[...])
    mixed = gm_ref[...].astype(F32) * branch_m + ga_ref[...].astype(F32) * branch_a
    out_ref[...] = mixed.astype(BF16)


def _merge(hf, hb, sig, ha, norm_w, w_m, w_a):
    t_rows = hf.shape[0]
    row = lambda i: (i, 0)
    const = lambda i: (0, 0)
    return pl.pallas_call(
        _merge_kernel,
        name="merge",
        grid=(t_rows // MG_TM,),
        in_specs=[
            pl.BlockSpec((MG_TM, M_WIDTH), row),
            pl.BlockSpec((MG_TM, M_WIDTH), row),
            pl.BlockSpec((MG_TM, M_WIDTH), lambda i: (i, N_BRANCH_GATES // M_WIDTH)),
            pl.BlockSpec((MG_TM, A_WIDTH), row),
            pl.BlockSpec((MG_TM, D_MODEL), lambda i: (i, 0)),
            pl.BlockSpec((MG_TM, D_MODEL), lambda i: (i, 1)),
            pl.BlockSpec((1, M_WIDTH), const),
            pl.BlockSpec((M_WIDTH, D_MODEL), const),
            pl.BlockSpec((A_WIDTH, D_MODEL), const),
        ],
        out_specs=pl.BlockSpec((MG_TM, D_MODEL), row),
        out_shape=jax.ShapeDtypeStruct((t_rows, D_MODEL), BF16),
        compiler_params=_cparams(("parallel",)),
    )(hf, hb, sig, ha, sig, sig, norm_w, w_m, w_a)


OP_TM = 256


def _outproj_kernel(mixed_ref, x_ref, w_ref, nw_ref, x1_ref, h2_ref):
    x1 = x_ref[...] + _dot(mixed_ref[...], w_ref[...])
    x1_ref[...] = x1
    ms = jnp.mean(x1 * x1, axis=-1, keepdims=True)
    h2_ref[...] = (x1 * lax.rsqrt(ms + EPS) * nw_ref[...]).astype(BF16)


def _outproj(mixed, x2, w_out, norm_w):
    t_rows = x2.shape[0]
    row = lambda i: (i, 0)
    const = lambda i: (0, 0)
    return pl.pallas_call(
        _outproj_kernel,
        name="outproj",
        grid=(t_rows // OP_TM,),
        in_specs=[
            pl.BlockSpec((OP_TM, D_MODEL), row),
            pl.BlockSpec((OP_TM, D_MODEL), row),
            pl.BlockSpec((D_MODEL, D_MODEL), const),
            pl.BlockSpec((1, D_MODEL), const),
        ],
        out_specs=[pl.BlockSpec((OP_TM, D_MODEL), row), pl.BlockSpec((OP_TM, D_MODEL), row)],
        out_shape=[jax.ShapeDtypeStruct((t_rows, D_MODEL), F32),
                   jax.ShapeDtypeStruct((t_rows, D_MODEL), BF16)],
        compiler_params=_cparams(("parallel",)),
    )(mixed, x2, w_out, norm_w)


FI_TM = 1024
FI_TN = 512


def _ffn_in_kernel(h_ref, wg_ref, wu_ref, out_ref):
    h = h_ref[...]
    gate = _dot(h, wg_ref[...])
    up = _dot(h, wu_ref[...])
    out_ref[...] = (gate * _sigmoid(gate) * up).astype(BF16)


def _ffn_in(h2, w_ffn_in):
    t_rows = h2.shape[0]
    nj = D_FF // FI_TN
    return pl.pallas_call(
        _ffn_in_kernel,
        name="ffn_in",
        grid=(t_rows // FI_TM, nj),
        in_specs=[
            pl.BlockSpec((FI_TM, D_MODEL), lambda i, j: (i, 0)),
            pl.BlockSpec((D_MODEL, FI_TN), lambda i, j: (0, j)),
            pl.BlockSpec((D_MODEL, FI_TN), lambda i, j: (0, nj + j)),
        ],
        out_specs=pl.BlockSpec((FI_TM, FI_TN), lambda i, j: (i, j)),
        out_shape=jax.ShapeDtypeStruct((t_rows, D_FF), BF16),
        compiler_params=_cparams(("parallel", "arbitrary")),
    )(h2, w_ffn_in, w_ffn_in)


FO_TM = 512
FO_TK = 1408


def _ffn_out_kernel(act_ref, w_ref, x1_ref, nw_ref, out_ref, acc_ref):
    kk = pl.program_id(1)

    @pl.when(kk == 0)
    def _():
        acc_ref[...] = x1_ref[...]

    acc_ref[...] += _dot(act_ref[...], w_ref[...])

    @pl.when(kk == pl.num_programs(1) - 1)
    def _():
        x2 = acc_ref[...]
        ms = jnp.mean(x2 * x2, axis=-1, keepdims=True)
        out_ref[...] = x2 * lax.rsqrt(ms + EPS) * nw_ref[...]


def _ffn_out(act, w_ffn_out, x1, norm_w):
    t_rows = x1.shape[0]
    return pl.pallas_call(
        _ffn_out_kernel,
        name="ffn_out",
        grid=(t_rows // FO_TM, D_FF // FO_TK),
        in_specs=[
            pl.BlockSpec((FO_TM, FO_TK), lambda i, k: (i, k)),
            pl.BlockSpec((FO_TK, D_MODEL), lambda i, k: (k, 0)),
            pl.BlockSpec((FO_TM, D_MODEL), lambda i, k: (i, 0)),
            pl.BlockSpec((1, D_MODEL), lambda i, k: (0, 0)),
        ],
        out_specs=pl.BlockSpec((FO_TM, D_MODEL), lambda i, k: (i, 0)),
        out_shape=jax.ShapeDtypeStruct((t_rows, D_MODEL), F32),
        scratch_shapes=[pltpu.VMEM((FO_TM, D_MODEL), F32)],
        compiler_params=_cparams(("parallel", "arbitrary")),
    )(act, w_ffn_out, x1, norm_w)


def _rope_tables(seq):
    inv = ROPE_THETA ** (-jnp.arange(0, A_DH, 2, dtype=F32) / A_DH)
    ang = jnp.arange(seq, dtype=F32)[:, None] * inv[None, :]
    cos = jnp.cos(ang)
    sin = jnp.sin(ang)
    cos_t = jnp.concatenate([cos, cos, cos, cos], axis=1)
    sin_t = jnp.concatenate([-sin, -sin, sin, sin], axis=1)
    return cos_t, sin_t


def _rotary_layout(w_seg):
    d = w_seg.shape[0]
    half = A_DH // 2
    return w_seg.reshape(d, A_HEADS, 2, 2, half).transpose(0, 1, 3, 2, 4).reshape(d, A_WIDTH)


def kernel(x, norm1_w, w_in, b_igate, b_fgate, b_branch_gate, mlstm_norm_w, lam_q1, lam_k1, lam_q2, lam_k2, attn_norm_w, w_branch_m, w_branch_a, w_out, norm2_w, w_ffn_in, w_ffn_out, final_norm_w):
    batch, seq, d = x.shape
    depth = w_in.shape[0]
    assert d == D_MODEL and depth == 1 and seq % PJ_TM == 0 and seq % AT_TQ == 0
    t_rows = batch * seq
    x2 = x.reshape(t_rows, d)
    cos_t, sin_t = _rope_tables(seq)

    l = 0
    w = w_in[l]
    qk_w = M_HEADS * M_DQK
    w_vq = jnp.concatenate([w[:, 2 * qk_w:OFF_MO], w[:, OFF_MQ:qk_w]], axis=1).astype(BF16)
    w_kvt = jnp.concatenate([w[:, qk_w:2 * qk_w], w[:, OFF_AV:OFF_GT]], axis=1).astype(BF16)
    w_sig = jnp.concatenate([w[:, OFF_GT:OFF_GT + N_BRANCH_GATES], w[:, OFF_MO:OFF_MG]], axis=1).astype(BF16)
    w_rot = jnp.concatenate([_rotary_layout(w[:, OFF_AQ:OFF_AK]),
                             _rotary_layout(w[:, OFF_AK:OFF_AV])], axis=1).astype(BF16)
    w_gate = jnp.pad(w[:, OFF_MG:OFF_AQ], ((0, 0), (0, LANES - N_GATE))).astype(BF16)

    scale_vq = jnp.ones((1, M_WIDTH + qk_w), F32)
    scale_kvt = jnp.concatenate([jnp.full((1, qk_w), M_DQK ** -0.5, F32), jnp.ones((1, A_WIDTH), F32)], axis=1)
    scale_rot = jnp.concatenate([jnp.full((1, A_WIDTH), Q_SCALE, F32), jnp.ones((1, A_WIDTH), F32)], axis=1)
    bias_sig = jnp.concatenate([b_branch_gate[l].astype(F32), jnp.zeros((M_WIDTH,), F32)]).reshape(1, -1)
    gate_bias = jnp.stack([b_igate[l], b_fgate[l]], axis=1).reshape(N_GATE).astype(F32)
    gate_bias = jnp.pad(gate_bias, (0, LANES - N_GATE))

    hn = _rmsnorm(x2, norm1_w[l].reshape(1, d))
    vq = _proj("scale", hn, w_vq, [scale_vq], seq, tn=qk_w)
    sig = _proj("sigmoid", hn, w_sig, [bias_sig], seq)
    qk = _proj("rope", hn, w_rot, [cos_t, sin_t, scale_rot], seq)
    kvt = _proj("transpose", hn, w_kvt, [scale_kvt], seq, tn=qk_w)
    gates = _proj("f32", hn, w_gate, [], seq)

    hf, hb = _mlstm(vq, kvt, gates, gate_bias.reshape(1, LANES), gate_bias.reshape(LANES, 1), batch, seq)
    ha = _attention(qk, kvt, lam_q1[l].reshape(1, A_DH), lam_k1[l].reshape(1, A_DH),
                    lam_q2[l].reshape(1, A_DH), lam_k2[l].reshape(1, A_DH),
                    attn_norm_w[l].reshape(A_DV, 1), batch, seq)
    mixed = _merge(hf, hb, sig, ha, mlstm_norm_w[l].reshape(1, M_WIDTH),
                   w_branch_m[l].astype(BF16), w_branch_a[l].astype(BF16))
    x1, h2 = _outproj(mixed, x2, w_out[l].astype(BF16), norm2_w[l].reshape(1, d))
    act = _ffn_in(h2, w_ffn_in[l].astype(BF16))
    out = _ffn_out(act, w_ffn_out[l].astype(BF16), x1, final_norm_w.reshape(1, d))
    return out.reshape(batch, seq, d)
```

```python
import functools
import math

import jax
import jax.numpy as jnp
from jax import lax
from jax.experimental import pallas as pl
from jax.experimental.pallas import tpu as pltpu

F32 = jnp.float32
BF16 = jnp.bfloat16

D_MODEL = 2048
M_HEADS = 4
M_DQK = 128
M_DV = 256
M_CHUNK = 128
GATE_CAP = 15.0
A_HEADS = 8
A_DH = 64
A_DV = 2 * A_DH
ROPE_THETA = 10000.0
D_FF = 5632
EPS = 1e-6
M_WIDTH = M_HEADS * M_DV
A_WIDTH = A_HEADS * A_DV
N_BRANCH_GATES = 2 * D_MODEL
LAM_INIT = 0.8 - 0.6 * math.exp(-0.3 * 0)

OFF_MQ = 0
OFF_MO = 2 * M_HEADS * M_DQK + M_WIDTH
OFF_MG = OFF_MO + M_WIDTH
N_GATE = 4 * M_HEADS
OFF_AQ = OFF_MG + N_GATE
OFF_AK = OFF_AQ + A_WIDTH
OFF_AV = OFF_AK + A_WIDTH
OFF_GT = OFF_AV + A_WIDTH
LANES = 128

VMEM_LIMIT = 56 * 1024 * 1024


def _cparams(sem):
    return pltpu.CompilerParams(dimension_semantics=sem, vmem_limit_bytes=VMEM_LIMIT)


def _dot(a, b):
    return jnp.dot(a, b, preferred_element_type=F32)


def _dot_nt(a, b):
    return lax.dot_general(a, b, (((1,), (1,)), ((), ())), preferred_element_type=F32)


def _dot_tn(a, b):
    return lax.dot_general(a, b, (((0,), (0,)), ((), ())), preferred_element_type=F32)


def _sigmoid(x):
    return 0.5 * jnp.tanh(0.5 * x) + 0.5


NORM_TM = 512
PJ_TM = 1024
PJ_TN = 1024
ROW_CHUNK = 256
Q_SCALE = (A_DH ** -0.5) * math.log2(math.e)


def _rmsnorm_kernel(x_ref, w_ref, o_ref):
    x = x_ref[...]
    ms = jnp.mean(x * x, axis=-1, keepdims=True)
    o_ref[...] = (x * lax.rsqrt(ms + EPS) * w_ref[...]).astype(BF16)


def _rmsnorm(x2, norm_w):
    t_rows, d = x2.shape
    return pl.pallas_call(
        _rmsnorm_kernel,
        name="rmsnorm",
        grid=(t_rows // NORM_TM,),
        in_specs=[pl.BlockSpec((NORM_TM, d), lambda i: (i, 0)),
                  pl.BlockSpec((1, d), lambda i: (0, 0))],
        out_specs=pl.BlockSpec((NORM_TM, d), lambda i: (i, 0)),
        out_shape=jax.ShapeDtypeStruct((t_rows, d), BF16),
        compiler_params=_cparams(("parallel",)),
    )(x2, norm_w)


def _rope(acc, cos, sin_signed):
    outs = []
    for c in range(acc.shape[1] // LANES):
        t = acc[:, c * LANES:(c + 1) * LANES]
        outs.append(t * cos + pltpu.roll(t, LANES // 2, axis=1) * sin_signed)
    return jnp.concatenate(outs, axis=1)


def _proj_kernel(mode, h_ref, w_ref, *refs):
    o_ref = refs[-1]
    for r in range(PJ_TM // ROW_CHUNK):
        rows = slice(r * ROW_CHUNK, (r + 1) * ROW_CHUNK)
        acc = _dot(h_ref[rows, :], w_ref[...])
        if mode == "scale":
            o_ref[rows, :] = (acc * refs[0][...]).astype(BF16)
        elif mode == "sigmoid":
            o_ref[rows, :] = _sigmoid(acc + refs[0][...]).astype(BF16)
        elif mode == "rope":
            cos_ref, sin_ref, cs_ref = refs[:3]
            o_ref[rows, :] = (_rope(acc, cos_ref[rows, :], sin_ref[rows, :]) * cs_ref[...]).astype(BF16)
        elif mode == "transpose":
            o_ref[0, :, rows] = (acc * refs[0][...]).T.astype(BF16)
        else:
            assert mode == "f32"
            o_ref[rows, :] = acc


def _proj(mode, hn, w, aux, seq, tn=PJ_TN):
    t_rows, d = hn.shape
    n = w.shape[1]
    tn = min(tn, n)
    s_blocks = seq // PJ_TM
    col = pl.BlockSpec((1, tn), lambda i, j: (0, j))
    pos = pl.BlockSpec((PJ_TM, LANES), lambda i, j: (i % s_blocks, 0))
    aux_specs = {"scale": [col], "sigmoid": [col], "rope": [pos, pos, col], "transpose": [col], "f32": []}[mode]
    if mode == "transpose":
        out_spec = pl.BlockSpec((1, tn, PJ_TM), lambda i, j: (i // s_blocks, j, i % s_blocks))
        out_shape = jax.ShapeDtypeStruct((t_rows // seq, n, seq), BF16)
    else:
        out_spec = pl.BlockSpec((PJ_TM, tn), lambda i, j: (i, j))
        out_shape = jax.ShapeDtypeStruct((t_rows, n), F32 if mode == "f32" else BF16)
    return pl.pallas_call(
        functools.partial(_proj_kernel, mode),
        name="proj_" + mode,
        grid=(t_rows // PJ_TM, n // tn),
        in_specs=[pl.BlockSpec((PJ_TM, d), lambda i, j: (i, 0)),
                  pl.BlockSpec((d, tn), lambda i, j: (0, j))] + aux_specs,
        out_specs=out_spec,
        out_shape=out_shape,
        compiler_params=_cparams(("parallel", "arbitrary")),
    )(hn, w, *aux)


L = M_CHUNK
MS_SUB = 4
DV_EXT = M_DV + LANES


def _softcap(t):
    return GATE_CAP * jnp.tanh(t / GATE_CAP)


def _log_sigmoid(t):
    return jnp.minimum(t, 0.0) - jnp.log(1.0 + jnp.exp(-jnp.abs(t)))


def _gate_act(pre, is_forget):
    c = _softcap(pre)
    return jnp.where(is_forget, _log_sigmoid(c), c)


def _split_dot(a, b, a_is_exact):
    if a_is_exact:
        hi = b.astype(BF16)
        lo = (b - hi.astype(F32)).astype(BF16)
        ab = a.astype(BF16)
        return _dot(ab, hi) + _dot(ab, lo)
    hi = a.astype(BF16)
    lo = (a - hi.astype(F32)).astype(BF16)
    bb = b.astype(BF16)
    return _dot(hi, bb) + _dot(lo, bb)


def _mlstm_kernel(qf_ref, kf_ref, vf_ref, gf_ref, qb_ref, kb_ref, vb_ref, gb_ref,
                  brow_ref, bcol_ref, hf_ref, hb_ref, c_ref, m_ref):
    step = pl.program_id(1)

    @pl.when(step == 0)
    def _():
        c_ref[...] = jnp.zeros_like(c_ref)
        m_ref[...] = jnp.zeros_like(m_ref)

    row = lax.broadcasted_iota(jnp.int32, (L, L), 0)
    col = lax.broadcasted_iota(jnp.int32, (L, L), 1)
    lane_id = lax.broadcasted_iota(jnp.int32, (1, LANES), 1)
    sub_id = lax.broadcasted_iota(jnp.int32, (LANES, 1), 0)
    forget_lane = (lane_id % 8) >= 4
    forget_sub = (sub_id % 8) >= 4
    ones_ext = jnp.ones((L, LANES), BF16)

    dirs = ((qf_ref, kf_ref, vf_ref, gf_ref, hf_ref), (qb_ref, kb_ref, vb_ref, gb_ref, hb_ref))
    for sub, d in [(sub, d) for sub in range(MS_SUB) for d in range(2)]:
        q_blk, k_blk, v_blk, g_blk, h_blk = dirs[d]
        r0 = (sub if d == 0 else MS_SUB - 1 - sub) * L
        q_ref, v_ref, g_ref, h_ref = (ref.at[r0:r0 + L, :] for ref in (q_blk, v_blk, g_blk, h_blk))
        kt_ref = k_blk.at[0, :, r0:r0 + L]
        visible = (row >= col) if d == 0 else (col >= row)
        vis_f = visible.astype(F32)

        g = g_ref[...]
        g_t = g.T
        act_c = _gate_act(g + brow_ref[...], forget_lane)
        act_r = _gate_act(g_t + bcol_ref[...], forget_sub)
        cum_c = _split_dot(vis_f, act_c, True)
        cum_r = _split_dot(act_r, vis_f.T, False)

        for h in range(M_HEADS):
            idx = d * M_HEADS + h
            ci = d * 8 + h
            cf = d * 8 + 4 + h
            bc = cum_c[:, cf:cf + 1]
            br = cum_r[cf:cf + 1, :]
            igr = act_r[ci:ci + 1, :]
            b_last = br[:, L - 1:L] if d == 0 else br[:, 0:1]
            m_old = m_ref[idx][0:1, 0:1]

            q = q_ref[:, h * M_DQK:(h + 1) * M_DQK]
            kt = kt_ref[h * M_DQK:(h + 1) * M_DQK, :]
            v_ext = jnp.concatenate([v_ref[:, h * M_DV:(h + 1) * M_DV], ones_ext], axis=1)

            dmat = jnp.where(visible, bc - br + igr, -jnp.inf)
            m_loc = jnp.max(dmat, axis=1, keepdims=True)
            s = _dot(q, kt) * jnp.exp(dmat - m_loc)
            sv = _dot(s.astype(BF16), v_ext)
            g_row = b_last - br + igr
            mg = jnp.max(g_row, axis=1, keepdims=True)
            kw_t = (kt.astype(F32) * jnp.exp(g_row - mg)).astype(BF16)
            u = _dot(kw_t, v_ext)

            c_old = c_ref[idx]
            inter = bc + m_old
            m_t = jnp.maximum(inter, m_loc)
            comb = jnp.exp(inter - m_t) * _dot(q, c_old.astype(BF16)) + jnp.exp(m_loc - m_t) * sv
            num = comb[:, :M_DV]
            den = comb[:, M_DV:M_DV + 1]
            hval = num / jnp.maximum(jnp.abs(den), jnp.exp(-m_t))
            h_ref[:, h * M_DV:(h + 1) * M_DV] = hval

            m_new = jnp.maximum(b_last + m_old, mg)
            c_ref[idx] = jnp.exp(b_last + m_old - m_new) * c_old + jnp.exp(mg - m_new) * u
            m_ref[idx] = jnp.broadcast_to(m_new, (8, LANES))


def _mlstm(vq, kvt, gates, bias_row, bias_col, batch, seq):
    t_rows = vq.shape[0]
    rows = MS_SUB * L
    nc = seq // rows
    fwd = lambda b, c: b * nc + c
    bwd = lambda b, c: b * nc + (nc - 1 - c)
    qk_w = M_HEADS * M_DQK
    in_specs = []
    for ch in (fwd, bwd):
        in_specs += [
            pl.BlockSpec((rows, qk_w), lambda b, c, ch=ch: (ch(b, c), M_WIDTH // qk_w)),
            pl.BlockSpec((1, qk_w, rows), lambda b, c, ch=ch: (b, 0, ch(0, c))),
            pl.BlockSpec((rows, M_WIDTH), lambda b, c, ch=ch: (ch(b, c), 0)),
            pl.BlockSpec((rows, LANES), lambda b, c, ch=ch: (ch(b, c), 0)),
        ]
    in_specs += [pl.BlockSpec((1, LANES), lambda b, c: (0, 0)),
                 pl.BlockSpec((LANES, 1), lambda b, c: (0, 0))]
    return pl.pallas_call(
        _mlstm_kernel,
        name="mlstm",
        grid=(batch, nc),
        in_specs=in_specs,
        out_specs=[pl.BlockSpec((rows, M_WIDTH), lambda b, c: (fwd(b, c), 0)),
                   pl.BlockSpec((rows, M_WIDTH), lambda b, c: (bwd(b, c), 0))],
        out_shape=[jax.ShapeDtypeStruct((t_rows, M_WIDTH), F32)] * 2,
        scratch_shapes=[pltpu.VMEM((2 * M_HEADS, M_DQK, DV_EXT), F32),
                        pltpu.VMEM((2 * M_HEADS, 8, LANES), F32)],
        compiler_params=_cparams(("parallel", "arbitrary")),
    )(vq, kvt, vq, gates, vq, kvt, vq, gates, bias_row, bias_col)


AT_TQ = 1024
AT_TK = 1024


def _attn_kernel(q_ref, qn_ref, k_ref, vt_ref, lq1_ref, lk1_ref, lq2_ref, lk2_ref, nw_ref,
                 o_ref, acc1_ref, acc2_ref, sa1_ref, sa2_ref, sb1_ref, sb2_ref, mba_ref):
    seq = k_ref.shape[0]
    nblk = seq // AT_TK
    qi = pl.program_id(2)
    lane = lax.broadcasted_iota(jnp.int32, (1, LANES), 1)
    in_map1 = (lane % A_DH) < (A_DH // 2)

    def split_maps(q):
        zero = jnp.zeros_like(q)
        return jnp.where(in_map1, q, zero), jnp.where(in_map1, zero, q)

    q_cur = split_maps(q_ref[...])
    acc1_ref[...] = jnp.zeros_like(acc1_ref)
    acc2_ref[...] = jnp.zeros_like(acc2_ref)

    def produce(i, qs, s1_ref, s2_ref):
        off = pl.multiple_of(i * AT_TK, AT_TK)
        kblk = k_ref[pl.ds(off, AT_TK), :]
        s1 = _dot_nt(kblk, qs[0])
        s1_ref[...] = s1
        s2 = _dot_nt(kblk, qs[1])
        s2_ref[...] = s2
        return jnp.max(s1, axis=0, keepdims=True), jnp.max(s2, axis=0, keepdims=True)

    def consume_map(s_ref, vtblk, mb, m, l, acc_ref):
        m_new = jnp.maximum(m, mb)
        alpha = jnp.exp2(m - m_new)
        p = jnp.exp2(s_ref[...] - m_new)
        l_new = alpha * l + jnp.sum(p, axis=0, keepdims=True)
        acc_ref[...] = alpha * acc_ref[...] + _dot(vtblk, p.astype(BF16))
        return m_new, l_new

    def consume(i, s1_ref, s2_ref, mb, stats):
        m1, l1, m2, l2 = stats
        off = pl.multiple_of(i * AT_TK, AT_TK)
        vtblk = vt_ref[0, :, pl.ds(off, AT_TK)]
        m1, l1 = consume_map(s1_ref, vtblk, mb[0], m1, l1, acc1_ref)
        m2, l2 = consume_map(s2_ref, vtblk, mb[1], m2, l2, acc2_ref)
        return m1, l1, m2, l2

    @pl.when(qi == 0)
    def _():
        mb = produce(0, q_cur, sa1_ref, sa2_ref)
        mba_ref[0:1, :] = mb[0]
        mba_ref[1:2, :] = mb[1]

    def body(j, carry):
        mb_a, stats = carry[:2], carry[2:]
        mb_b = produce(2 * j + 1, q_cur, sb1_ref, sb2_ref)
        stats = consume(2 * j, sa1_ref, sa2_ref, mb_a, stats)
        mb_a = produce(2 * j + 2, q_cur, sa1_ref, sa2_ref)
        stats = consume(2 * j + 1, sb1_ref, sb2_ref, mb_b, stats)
        return (*mb_a, *stats)

    neg = jnp.full((1, AT_TQ), -jnp.inf, F32)
    zer = jnp.zeros((1, AT_TQ), F32)
    carry = lax.fori_loop(0, nblk // 2 - 1, body,
                          (mba_ref[0:1, :], mba_ref[1:2, :], neg, zer, neg, zer))
    mb_a, stats = carry[:2], carry[2:]
    mb_b = produce(nblk - 1, q_cur, sb1_ref, sb2_ref)
    stats = consume(nblk - 2, sa1_ref, sa2_ref, mb_a, stats)
    mb_next = produce(0, split_maps(qn_ref[...]), sa1_ref, sa2_ref)
    mba_ref[0:1, :] = mb_next[0]
    mba_ref[1:2, :] = mb_next[1]
    m1, l1, m2, l2 = consume(nblk - 1, sb1_ref, sb2_ref, mb_b, stats)

    lam = (jnp.exp(jnp.sum(lq1_ref[...] * lk1_ref[...], axis=1, keepdims=True))
           - jnp.exp(jnp.sum(lq2_ref[...] * lk2_ref[...], axis=1, keepdims=True))
           + LAM_INIT)
    o = acc1_ref[...] * (1.0 / l1) - acc2_ref[...] * (lam / l2)
    ms = jnp.mean(o * o, axis=0, keepdims=True)
    y = o * lax.rsqrt(ms + EPS) * nw_ref[...] * (1.0 - LAM_INIT)
    o_ref[...] = y.T.astype(BF16)


def _attention(qk, kvt, lq1, lk1, lq2, lk2, norm_w, batch, seq):
    v_blk0 = (M_HEADS * M_DQK) // A_DV
    t_rows = qk.shape[0]
    nq = seq // AT_TQ
    small = pl.BlockSpec((1, A_DH), lambda b, h, i: (0, 0))
    return pl.pallas_call(
        _attn_kernel,
        name="attention",
        grid=(batch, A_HEADS, nq),
        in_specs=[
            pl.BlockSpec((AT_TQ, LANES), lambda b, h, i: (b * nq + i, h)),
            pl.BlockSpec((AT_TQ, LANES), lambda b, h, i: (b * nq + jnp.minimum(i + 1, nq - 1), h)),
            pl.BlockSpec((seq, LANES), lambda b, h, i: (b, A_HEADS + h)),
            pl.BlockSpec((1, A_DV, seq), lambda b, h, i: (b, v_blk0 + h, 0)),
            small, small, small, small,
            pl.BlockSpec((A_DV, 1), lambda b, h, i: (0, 0)),
        ],
        out_specs=pl.BlockSpec((AT_TQ, LANES), lambda b, h, i: (b * nq + i, h)),
        out_shape=jax.ShapeDtypeStruct((t_rows, A_WIDTH), BF16),
        scratch_shapes=([pltpu.VMEM((A_DV, AT_TQ), F32)] * 2 + [pltpu.VMEM((AT_TK, AT_TQ), F32)] * 4
                        + [pltpu.VMEM((8, AT_TQ), F32)]),
        compiler_params=_cparams(("arbitrary", "arbitrary", "arbitrary")),
    )(qk, qk, qk, kvt, lq1, lk1, lq2, lk2, norm_w)


MG_TM = 256


def _merge_kernel(hf_ref, hb_ref, mo_ref, ha_ref, gm_ref, ga_ref, nw_ref, wm_ref, wa_ref, out_ref):
    hm = hf_ref[...] + hb_ref[...]
    parts = []
    for h in range(M_HEADS):
        seg = hm[:, h * M_DV:(h + 1) * M_DV]
        ms = jnp.mean(seg * seg, axis=-1, keepdims=True)
        parts.append(seg * lax.rsqrt(ms + EPS))
    hn = jnp.concatenate(parts, axis=1) * nw_ref[...]
    hn = (hn * mo_ref[...].astype(F32)).astype(BF16)
    branch_m = _dot(hn, wm_ref[...])
    branch_a = _dot(ha_ref[...], wa_ref[...])
    mixed = gm_ref[...].astype(F32) * branch_m + ga_ref[...].astype(F32) * branch_a
    out_ref[...] = mixed.astype(BF16)


def _merge(hf, hb, sig, ha, norm_w, w_m, w_a):
    t_rows = hf.shape[0]
    row = lambda i: (i, 0)
    const = lambda i: (0, 0)
    return pl.pallas_call(
        _merge_kernel,
        name="merge",
        grid=(t_rows // MG_TM,),
        in_specs=[
            pl.BlockSpec((MG_TM, M_WIDTH), row),
            pl.BlockSpec((MG_TM, M_WIDTH), row),
            pl.BlockSpec((MG_TM, M_WIDTH), lambda i: (i, N_BRANCH_GATES // M_WIDTH)),
            pl.BlockSpec((MG_TM, A_WIDTH), row),
            pl.BlockSpec((MG_TM, D_MODEL), lambda i: (i, 0)),
            pl.BlockSpec((MG_TM, D_MODEL), lambda i: (i, 1)),
            pl.BlockSpec((1, M_WIDTH), const),
            pl.BlockSpec((M_WIDTH, D_MODEL), const),
            pl.BlockSpec((A_WIDTH, D_MODEL), const),
        ],
        out_specs=pl.BlockSpec((MG_TM, D_MODEL), row),
        out_shape=jax.ShapeDtypeStruct((t_rows, D_MODEL), BF16),
        compiler_params=_cparams(("parallel",)),
    )(hf, hb, sig, ha, sig, sig, norm_w, w_m, w_a)


OP_TM = 256


def _outproj_kernel(mixed_ref, x_ref, w_ref, nw_ref, x1_ref, h2_ref):
    x1 = x_ref[...] + _dot(mixed_ref[...], w_ref[...])
    x1_ref[...] = x1
    ms = jnp.mean(x1 * x1, axis=-1, keepdims=True)
    h2_ref[...] = (x1 * lax.rsqrt(ms + EPS) * nw_ref[...]).astype(BF16)


def _outproj(mixed, x2, w_out, norm_w):
    t_rows = x2.shape[0]
    row = lambda i: (i, 0)
    const = lambda i: (0, 0)
    return pl.pallas_call(
        _outproj_kernel,
        name="outproj",
        grid=(t_rows // OP_TM,),
        in_specs=[
            pl.BlockSpec((OP_TM, D_MODEL), row),
            pl.BlockSpec((OP_TM, D_MODEL), row),
            pl.BlockSpec((D_MODEL, D_MODEL), const),
            pl.BlockSpec((1, D_MODEL), const),
        ],
        out_specs=[pl.BlockSpec((OP_TM, D_MODEL), row), pl.BlockSpec((OP_TM, D_MODEL), row)],
        out_shape=[jax.ShapeDtypeStruct((t_rows, D_MODEL), F32),
                   jax.ShapeDtypeStruct((t_rows, D_MODEL), BF16)],
        compiler_params=_cparams(("parallel",)),
    )(mixed, x2, w_out, norm_w)


FI_TM = 1024
FI_TN = 512


def _ffn_in_kernel(h_ref, wg_ref, wu_ref, out_ref):
    h = h_ref[...]
    gate = _dot(h, wg_ref[...])
    up = _dot(h, wu_ref[...])
    out_ref[...] = (gate * _sigmoid(gate) * up).astype(BF16)


def _ffn_in(h2, w_ffn_in):
    t_rows = h2.shape[0]
    nj = D_FF // FI_TN
    return pl.pallas_call(
        _ffn_in_kernel,
        name="ffn_in",
        grid=(t_rows // FI_TM, nj),
        in_specs=[
            pl.BlockSpec((FI_TM, D_MODEL), lambda i, j: (i, 0)),
            pl.BlockSpec((D_MODEL, FI_TN), lambda i, j: (0, j)),
            pl.BlockSpec((D_MODEL, FI_TN), lambda i, j: (0, nj + j)),
        ],
        out_specs=pl.BlockSpec((FI_TM, FI_TN), lambda i, j: (i, j)),
        out_shape=jax.ShapeDtypeStruct((t_rows, D_FF), BF16),
        compiler_params=_cparams(("parallel", "arbitrary")),
    )(h2, w_ffn_in, w_ffn_in)


FO_TM = 512
FO_TK = 1408


def _ffn_out_kernel(act_ref, w_ref, x1_ref, nw_ref, out_ref, acc_ref):
    kk = pl.program_id(1)

    @pl.when(kk == 0)
    def _():
        acc_ref[...] = x1_ref[...]

    acc_ref[...] += _dot(act_ref[...], w_ref[...])

    @pl.when(kk == pl.num_programs(1) - 1)
    def _():
        x2 = acc_ref[...]
        ms = jnp.mean(x2 * x2, axis=-1, keepdims=True)
        out_ref[...] = x2 * lax.rsqrt(ms + EPS) * nw_ref[...]


def _ffn_out(act, w_ffn_out, x1, norm_w):
    t_rows = x1.shape[0]
    return pl.pallas_call(
        _ffn_out_kernel,
        name="ffn_out",
        grid=(t_rows // FO_TM, D_FF // FO_TK),
        in_specs=[
            pl.BlockSpec((FO_TM, FO_TK), lambda i, k: (i, k)),
            pl.BlockSpec((FO_TK, D_MODEL), lambda i, k: (k, 0)),
            pl.BlockSpec((FO_TM, D_MODEL), lambda i, k: (i, 0)),
            pl.BlockSpec((1, D_MODEL), lambda i, k: (0, 0)),
        ],
        out_specs=pl.BlockSpec((FO_TM, D_MODEL), lambda i, k: (i, 0)),
        out_shape=jax.ShapeDtypeStruct((t_rows, D_MODEL), F32),
        scratch_shapes=[pltpu.VMEM((FO_TM, D_MODEL), F32)],
        compiler_params=_cparams(("parallel", "arbitrary")),
    )(act, w_ffn_out, x1, norm_w)


def _rope_tables(seq):
    inv = ROPE_THETA ** (-jnp.arange(0, A_DH, 2, dtype=F32) / A_DH)
    ang = jnp.arange(seq, dtype=F32)[:, None] * inv[None, :]
    cos = jnp.cos(ang)
    sin = jnp.sin(ang)
    cos_t = jnp.concatenate([cos, cos, cos, cos], axis=1)
    sin_t = jnp.concatenate([-sin, -sin, sin, sin], axis=1)
    return cos_t, sin_t


def _rotary_layout(w_seg):
    d = w_seg.shape[0]
    half = A_DH // 2
    return w_seg.reshape(d, A_HEADS, 2, 2, half).transpose(0, 1, 3, 2, 4).reshape(d, A_WIDTH)


def kernel(x, norm1_w, w_in, b_igate, b_fgate, b_branch_gate, mlstm_norm_w, lam_q1, lam_k1, lam_q2, lam_k2, attn_norm_w, w_branch_m, w_branch_a, w_out, norm2_w, w_ffn_in, w_ffn_out, final_norm_w):
    batch, seq, d = x.shape
    depth = w_in.shape[0]
    assert d == D_MODEL and depth == 1 and seq % PJ_TM == 0 and seq % AT_TQ == 0
    t_rows = batch * seq
    x2 = x.reshape(t_rows, d)
    cos_t, sin_t = _rope_tables(seq)

    l = 0
    w = w_in[l]
    qk_w = M_HEADS * M_DQK
    w_vq = jnp.concatenate([w[:, 2 * qk_w:OFF_MO], w[:, OFF_MQ:qk_w]], axis=1).astype(BF16)
    w_kvt = jnp.concatenate([w[:, qk_w:2 * qk_w], w[:, OFF_AV:OFF_GT]], axis=1).astype(BF16)
    w_sig = jnp.concatenate([w[:, OFF_GT:OFF_GT + N_BRANCH_GATES], w[:, OFF_MO:OFF_MG]], axis=1).astype(BF16)
    w_rot = jnp.concatenate([_rotary_layout(w[:, OFF_AQ:OFF_AK]),
                             _rotary_layout(w[:, OFF_AK:OFF_AV])], axis=1).astype(BF16)
    w_gate = jnp.pad(w[:, OFF_MG:OFF_AQ], ((0, 0), (0, LANES - N_GATE))).astype(BF16)

    scale_vq = jnp.ones((1, M_WIDTH + qk_w), F32)
    scale_kvt = jnp.concatenate([jnp.full((1, qk_w), M_DQK ** -0.5, F32), jnp.ones((1, A_WIDTH), F32)], axis=1)
    scale_rot = jnp.concatenate([jnp.full((1, A_WIDTH), Q_SCALE, F32), jnp.ones((1, A_WIDTH), F32)], axis=1)
    bias_sig = jnp.concatenate([b_branch_gate[l].astype(F32), jnp.zeros((M_WIDTH,), F32)]).reshape(1, -1)
    gate_bias = jnp.stack([b_igate[l], b_fgate[l]], axis=1).reshape(N_GATE).astype(F32)
    gate_bias = jnp.pad(gate_bias, (0, LANES - N_GATE))

    hn = _rmsnorm(x2, norm1_w[l].reshape(1, d))
    vq = _proj("scale", hn, w_vq, [scale_vq], seq, tn=qk_w)
    sig = _proj("sigmoid", hn, w_sig, [bias_sig], seq)
    qk = _proj("rope", hn, w_rot, [cos_t, sin_t, scale_rot], seq)
    kvt = _proj("transpose", hn, w_kvt, [scale_kvt], seq, tn=qk_w)
    gates = _proj("f32", hn, w_gate, [], seq)

    hf, hb = _mlstm(vq, kvt, gates, gate_bias.reshape(1, LANES), gate_bias.reshape(LANES, 1), batch, seq)
    ha = _attention(qk, kvt, lam_q1[l].reshape(1, A_DH), lam_k1[l].reshape(1, A_DH),
                    lam_q2[l].reshape(1, A_DH), lam_k2[l].reshape(1, A_DH),
                    attn_norm_w[l].reshape(A_DV, 1), batch, seq)
    mixed = _merge(hf, hb, sig, ha, mlstm_norm_w[l].reshape(1, M_WIDTH),
                   w_branch_m[l].astype(BF16), w_branch_a[l].astype(BF16))
    x1, h2 = _outproj(mixed, x2, w_out[l].astype(BF16), norm2_w[l].reshape(1, d))
    act = _ffn_in(h2, w_ffn_in[l].astype(BF16))
    out = _ffn_out(act, w_ffn_out[l].astype(BF16), x1, final_norm_w.reshape(1, d))
    return out.reshape(batch, seq, d)
```

```python
import functools
import math

import jax
import jax.numpy as jnp
from jax import lax
from jax.experimental import pallas as pl
from jax.experimental.pallas import tpu as pltpu

F32 = jnp.float32
BF16 = jnp.bfloat16

D_MODEL = 2048
M_HEADS = 4
M_DQK = 128
M_DV = 256
M_CHUNK = 128
GATE_CAP = 15.0
A_HEADS = 8
A_DH = 64
A_DV = 2 * A_DH
ROPE_THETA = 10000.0
D_FF = 5632
EPS = 1e-6
M_WIDTH = M_HEADS * M_DV
A_WIDTH = A_HEADS * A_DV
N_BRANCH_GATES = 2 * D_MODEL
LAM_INIT = 0.8 - 0.6 * math.exp(-0.3 * 0)

OFF_MQ = 0
OFF_MO = 2 * M_HEADS * M_DQK + M_WIDTH
OFF_MG = OFF_MO + M_WIDTH
N_GATE = 4 * M_HEADS
OFF_AQ = OFF_MG + N_GATE
OFF_AK = OFF_AQ + A_WIDTH
OFF_AV = OFF_AK + A_WIDTH
OFF_GT = OFF_AV + A_WIDTH
LANES = 128

VMEM_LIMIT = 56 * 1024 * 1024


def _cparams(sem):
    return pltpu.CompilerParams(dimension_semantics=sem, vmem_limit_bytes=VMEM_LIMIT)


def _dot(a, b):
    return jnp.dot(a, b, preferred_element_type=F32)


def _dot_nt(a, b):
    return lax.dot_general(a, b, (((1,), (1,)), ((), ())), preferred_element_type=F32)


def _dot_tn(a, b):
    return lax.dot_general(a, b, (((0,), (0,)), ((), ())), preferred_element_type=F32)


def _sigmoid(x):
    return 0.5 * jnp.tanh(0.5 * x) + 0.5


NORM_TM = 512
PJ_TM = 1024
PJ_TN = 1024
PJ_TN_NARROW = 768
ROW_CHUNK = 256
Q_SCALE = (A_DH ** -0.5) * math.log2(math.e)


def _rmsnorm_kernel(x_ref, w_ref, wg_ref, o_ref, g_ref):
    x = x_ref[...]
    ms = jnp.mean(x * x, axis=-1, keepdims=True)
    hn = (x * lax.rsqrt(ms + EPS) * w_ref[...]).astype(BF16)
    o_ref[...] = hn
    g_ref[...] = _dot(hn, wg_ref[...])


def _rmsnorm(x2, norm_w, w_gate):
    t_rows, d = x2.shape
    return pl.pallas_call(
        _rmsnorm_kernel,
        name="rmsnorm",
        grid=(t_rows // NORM_TM,),
        in_specs=[pl.BlockSpec((NORM_TM, d), lambda i: (i, 0)),
                  pl.BlockSpec((1, d), lambda i: (0, 0)),
                  pl.BlockSpec((d, LANES), lambda i: (0, 0))],
        out_specs=[pl.BlockSpec((NORM_TM, d), lambda i: (i, 0)),
                   pl.BlockSpec((NORM_TM, LANES), lambda i: (i, 0))],
        out_shape=[jax.ShapeDtypeStruct((t_rows, d), BF16),
                   jax.ShapeDtypeStruct((t_rows, LANES), F32)],
        compiler_params=_cparams(("parallel",)),
    )(x2, norm_w, w_gate)


def _rope(acc, cos, sin_signed):
    outs = []
    for c in range(acc.shape[1] // LANES):
        t = acc[:, c * LANES:(c + 1) * LANES]
        outs.append(t * cos + pltpu.roll(t, LANES // 2, axis=1) * sin_signed)
    return jnp.concatenate(outs, axis=1)


def _proj_kernel(mode, h_ref, w_ref, *refs):
    o_ref = refs[-1]
    for r in range(PJ_TM // ROW_CHUNK):
        rows = slice(r * ROW_CHUNK, (r + 1) * ROW_CHUNK)
        acc = _dot(h_ref[rows, :], w_ref[...])
        if mode == "scale":
            o_ref[rows, :] = (acc * refs[0][...]).astype(BF16)
        elif mode == "sigmoid":
            o_ref[rows, :] = _sigmoid(acc + refs[0][...]).astype(BF16)
        elif mode == "rope":
            cos_ref, sin_ref, cs_ref = refs[:3]
            o_ref[rows, :] = (_rope(acc, cos_ref[rows, :], sin_ref[rows, :]) * cs_ref[...]).astype(BF16)
        else:
            assert mode == "transpose"
            o_ref[0, :, rows] = (acc * refs[0][...]).T.astype(BF16)


def _proj(mode, hn, w, aux, seq, tn=PJ_TN):
    t_rows, d = hn.shape
    n = w.shape[1]
    tn = min(tn, n)
    s_blocks = seq // PJ_TM
    col = pl.BlockSpec((1, tn), lambda i, j: (0, j))
    pos = pl.BlockSpec((PJ_TM, LANES), lambda i, j: (i % s_blocks, 0))
    aux_specs = {"scale": [col], "sigmoid": [col], "rope": [pos, pos, col], "transpose": [col]}[mode]
    if mode == "transpose":
        out_spec = pl.BlockSpec((1, tn, PJ_TM), lambda i, j: (i // s_blocks, j, i % s_blocks))
        out_shape = jax.ShapeDtypeStruct((t_rows // seq, n, seq), BF16)
    else:
        out_spec = pl.BlockSpec((PJ_TM, tn), lambda i, j: (i, j))
        out_shape = jax.ShapeDtypeStruct((t_rows, n), BF16)
    return pl.pallas_call(
        functools.partial(_proj_kernel, mode),
        name="proj_" + mode,
        grid=(t_rows // PJ_TM, n // tn),
        in_specs=[pl.BlockSpec((PJ_TM, d), lambda i, j: (i, 0)),
                  pl.BlockSpec((d, tn), lambda i, j: (0, j))] + aux_specs,
        out_specs=out_spec,
        out_shape=out_shape,
        compiler_params=_cparams(("parallel", "arbitrary")),
    )(hn, w, *aux)


L = M_CHUNK
MS_SUB = 4
DV_EXT = M_DV + LANES


def _softcap(t):
    return GATE_CAP * jnp.tanh(t / GATE_CAP)


def _log_sigmoid(t):
    return jnp.minimum(t, 0.0) - jnp.log(1.0 + jnp.exp(-jnp.abs(t)))


def _gate_act(pre, is_forget):
    c = _softcap(pre)
    return jnp.where(is_forget, _log_sigmoid(c), c)


def _split_dot(a, b, a_is_exact):
    if a_is_exact:
        hi = b.astype(BF16)
        lo = (b - hi.astype(F32)).astype(BF16)
        ab = a.astype(BF16)
        return _dot(ab, hi) + _dot(ab, lo)
    hi = a.astype(BF16)
    lo = (a - hi.astype(F32)).astype(BF16)
    bb = b.astype(BF16)
    return _dot(hi, bb) + _dot(lo, bb)


def _mlstm_kernel(qf_ref, kf_ref, vf_ref, gf_ref, qb_ref, kb_ref, vb_ref, gb_ref,
                  brow_ref, bcol_ref, hf_ref, hb_ref, c_ref, m_ref):
    step = pl.program_id(1)

    @pl.when(step == 0)
    def _():
        c_ref[...] = jnp.zeros_like(c_ref)
        m_ref[...] = jnp.zeros_like(m_ref)

    row = lax.broadcasted_iota(jnp.int32, (L, L), 0)
    col = lax.broadcasted_iota(jnp.int32, (L, L), 1)
    lane_id = lax.broadcasted_iota(jnp.int32, (1, LANES), 1)
    sub_id = lax.broadcasted_iota(jnp.int32, (LANES, 1), 0)
    forget_lane = (lane_id % 8) >= 4
    forget_sub = (sub_id % 8) >= 4
    ones_ext = jnp.ones((L, LANES), BF16)

    dirs = ((qf_ref, kf_ref, vf_ref, gf_ref, hf_ref), (qb_ref, kb_ref, vb_ref, gb_ref, hb_ref))
    for sub, d in [(sub, d) for sub in range(MS_SUB) for d in range(2)]:
        q_blk, k_blk, v_blk, g_blk, h_blk = dirs[d]
        r0 = (sub if d == 0 else MS_SUB - 1 - sub) * L
        q_ref, v_ref, g_ref, h_ref = (ref.at[r0:r0 + L, :] for ref in (q_blk, v_blk, g_blk, h_blk))
        kt_ref = k_blk.at[0, :, r0:r0 + L]
        visible = (row >= col) if d == 0 else (col >= row)
        vis_f = visible.astype(F32)

        g = g_ref[...]
        g_t = g.T
        act_c = _gate_act(g + brow_ref[...], forget_lane)
        act_r = _gate_act(g_t + bcol_ref[...], forget_sub)
        cum_c = _split_dot(vis_f, act_c, True)
        cum_r = _split_dot(act_r, vis_f.T, False)

        for h in range(M_HEADS):
            idx = d * M_HEADS + h
            ci = d * 8 + h
            cf = d * 8 + 4 + h
            bc = cum_c[:, cf:cf + 1]
            br = cum_r[cf:cf + 1, :]
            igr = act_r[ci:ci + 1, :]
            b_last = br[:, L - 1:L] if d == 0 else br[:, 0:1]
            m_old = m_ref[idx][0:1, 0:1]

            q = q_ref[:, h * M_DQK:(h + 1) * M_DQK]
            kt = kt_ref[h * M_DQK:(h + 1) * M_DQK, :]
            v_ext = jnp.concatenate([v_ref[:, h * M_DV:(h + 1) * M_DV], ones_ext], axis=1)

            dmat = jnp.where(visible, bc - br + igr, -jnp.inf)
            m_loc = jnp.max(dmat, axis=1, keepdims=True)
            s = _dot(q, kt) * jnp.exp(dmat - m_loc)
            sv = _dot(s.astype(BF16), v_ext)
            g_row = b_last - br + igr
            mg = jnp.max(g_row, axis=1, keepdims=True)
            kw_t = (kt.astype(F32) * jnp.exp(g_row - mg)).astype(BF16)
            u = _dot(kw_t, v_ext)

            c_old = c_ref[idx]
            inter = bc + m_old
            m_t = jnp.maximum(inter, m_loc)
            comb = jnp.exp(inter - m_t) * _dot(q, c_old.astype(BF16)) + jnp.exp(m_loc - m_t) * sv
            num = comb[:, :M_DV]
            den = comb[:, M_DV:M_DV + 1]
            hval = num / jnp.maximum(jnp.abs(den), jnp.exp(-m_t))
            h_ref[:, h * M_DV:(h + 1) * M_DV] = hval

            m_new = jnp.maximum(b_last + m_old, mg)
            c_ref[idx] = jnp.exp(b_last + m_old - m_new) * c_old + jnp.exp(mg - m_new) * u
            m_ref[idx] = jnp.broadcast_to(m_new, (8, LANES))


def _mlstm(vq, kvt, gates, bias_row, bias_col, batch, seq):
    t_rows = vq.shape[0]
    rows = MS_SUB * L
    nc = seq // rows
    fwd = lambda b, c: b * nc + c
    bwd = lambda b, c: b * nc + (nc - 1 - c)
    qk_w = M_HEADS * M_DQK
    in_specs = []
    for ch in (fwd, bwd):
        in_specs += [
            pl.BlockSpec((rows, qk_w), lambda b, c, ch=ch: (ch(b, c), M_WIDTH // qk_w)),
            pl.BlockSpec((1, qk_w, rows), lambda b, c, ch=ch: (b, 0, ch(0, c))),
            pl.BlockSpec((rows, M_WIDTH), lambda b, c, ch=ch: (ch(b, c), 0)),
            pl.BlockSpec((rows, LANES), lambda b, c, ch=ch: (ch(b, c), 0)),
        ]
    in_specs += [pl.BlockSpec((1, LANES), lambda b, c: (0, 0)),
                 pl.BlockSpec((LANES, 1), lambda b, c: (0, 0))]
    return pl.pallas_call(
        _mlstm_kernel,
        name="mlstm",
        grid=(batch, nc),
        in_specs=in_specs,
        out_specs=[pl.BlockSpec((rows, M_WIDTH), lambda b, c: (fwd(b, c), 0)),
                   pl.BlockSpec((rows, M_WIDTH), lambda b, c: (bwd(b, c), 0))],
        out_shape=[jax.ShapeDtypeStruct((t_rows, M_WIDTH), F32)] * 2,
        scratch_shapes=[pltpu.VMEM((2 * M_HEADS, M_DQK, DV_EXT), F32),
                        pltpu.VMEM((2 * M_HEADS, 8, LANES), F32)],
        compiler_params=_cparams(("parallel", "arbitrary")),
    )(vq, kvt, vq, gates, vq, kvt, vq, gates, bias_row, bias_col)


AT_TQ = 1024
AT_TK = 1024


def _attn_kernel(q_ref, qn_ref, k_ref, vt_ref, lq1_ref, lk1_ref, lq2_ref, lk2_ref, nw_ref,
                 o_ref, acc1_ref, acc2_ref, sa1_ref, sa2_ref, sb1_ref, sb2_ref, mba_ref):
    seq = k_ref.shape[0]
    nblk = seq // AT_TK
    qi = pl.program_id(2)
    lane = lax.broadcasted_iota(jnp.int32, (1, LANES), 1)
    in_map1 = (lane % A_DH) < (A_DH // 2)

    def split_maps(q):
        zero = jnp.zeros_like(q)
        return jnp.where(in_map1, q, zero), jnp.where(in_map1, zero, q)

    q_cur = split_maps(q_ref[...])
    acc1_ref[...] = jnp.zeros_like(acc1_ref)
    acc2_ref[...] = jnp.zeros_like(acc2_ref)

    def produce(i, qs, s1_ref, s2_ref):
        off = pl.multiple_of(i * AT_TK, AT_TK)
        kblk = k_ref[pl.ds(off, AT_TK), :]
        s1 = _dot_nt(kblk, qs[0])
        s1_ref[...] = s1
        s2 = _dot_nt(kblk, qs[1])
        s2_ref[...] = s2
        return jnp.max(s1, axis=0, keepdims=True), jnp.max(s2, axis=0, keepdims=True)

    def consume_map(s_ref, vtblk, mb, m, l, acc_ref):
        m_new = jnp.maximum(m, mb)
        alpha = jnp.exp2(m - m_new)
        p = jnp.exp2(s_ref[...] - m_new)
        l_new = alpha * l + jnp.sum(p, axis=0, keepdims=True)
        acc_ref[...] = alpha * acc_ref[...] + _dot(vtblk, p.astype(BF16))
        return m_new, l_new

    def consume(i, s1_ref, s2_ref, mb, stats):
        m1, l1, m2, l2 = stats
        off = pl.multiple_of(i * AT_TK, AT_TK)
        vtblk = vt_ref[0, :, pl.ds(off, AT_TK)]
        m1, l1 = consume_map(s1_ref, vtblk, mb[0], m1, l1, acc1_ref)
        m2, l2 = consume_map(s2_ref, vtblk, mb[1], m2, l2, acc2_ref)
        return m1, l1, m2, l2

    @pl.when(qi == 0)
    def _():
        mb = produce(0, q_cur, sa1_ref, sa2_ref)
        mba_ref[0:1, :] = mb[0]
        mba_ref[1:2, :] = mb[1]

    def body(j, carry):
        mb_a, stats = carry[:2], carry[2:]
        mb_b = produce(2 * j + 1, q_cur, sb1_ref, sb2_ref)
        stats = consume(2 * j, sa1_ref, sa2_ref, mb_a, stats)
        mb_a = produce(2 * j + 2, q_cur, sa1_ref, sa2_ref)
        stats = consume(2 * j + 1, sb1_ref, sb2_ref, mb_b, stats)
        return (*mb_a, *stats)

    neg = jnp.full((1, AT_TQ), -jnp.inf, F32)
    zer = jnp.zeros((1, AT_TQ), F32)
    carry = lax.fori_loop(0, nblk // 2 - 1, body,
                          (mba_ref[0:1, :], mba_ref[1:2, :], neg, zer, neg, zer))
    mb_a, stats = carry[:2], carry[2:]
    mb_b = produce(nblk - 1, q_cur, sb1_ref, sb2_ref)
    stats = consume(nblk - 2, sa1_ref, sa2_ref, mb_a, stats)
    mb_next = produce(0, split_maps(qn_ref[...]), sa1_ref, sa2_ref)
    mba_ref[0:1, :] = mb_next[0]
    mba_ref[1:2, :] = mb_next[1]
    m1, l1, m2, l2 = consume(nblk - 1, sb1_ref, sb2_ref, mb_b, stats)

    lam = (jnp.exp(jnp.sum(lq1_ref[...] * lk1_ref[...], axis=1, keepdims=True))
           - jnp.exp(jnp.sum(lq2_ref[...] * lk2_ref[...], axis=1, keepdims=True))
           + LAM_INIT)
    o = acc1_ref[...] * (1.0 / l1) - acc2_ref[...] * (lam / l2)
    ms = jnp.mean(o * o, axis=0, keepdims=True)
    y = o * lax.rsqrt(ms + EPS) * nw_ref[...] * (1.0 - LAM_INIT)
    o_ref[...] = y.T.astype(BF16)


def _attention(qk, kvt, lq1, lk1, lq2, lk2, norm_w, batch, seq):
    v_blk0 = (M_HEADS * M_DQK) // A_DV
    t_rows = qk.shape[0]
    nq = seq // AT_TQ
    small = pl.BlockSpec((1, A_DH), lambda b, h, i: (0, 0))
    return pl.pallas_call(
        _attn_kernel,
        name="attention",
        grid=(batch, A_HEADS, nq),
        in_specs=[
            pl.BlockSpec((AT_TQ, LANES), lambda b, h, i: (b * nq + i, h)),
            pl.BlockSpec((AT_TQ, LANES), lambda b, h, i: (b * nq + jnp.minimum(i + 1, nq - 1), h)),
            pl.BlockSpec((seq, LANES), lambda b, h, i: (b, A_HEADS + h)),
            pl.BlockSpec((1, A_DV, seq), lambda b, h, i: (b, v_blk0 + h, 0)),
            small, small, small, small,
            pl.BlockSpec((A_DV, 1), lambda b, h, i: (0, 0)),
        ],
        out_specs=pl.BlockSpec((AT_TQ, LANES), lambda b, h, i: (b * nq + i, h)),
        out_shape=jax.ShapeDtypeStruct((t_rows, A_WIDTH), BF16),
        scratch_shapes=([pltpu.VMEM((A_DV, AT_TQ), F32)] * 2 + [pltpu.VMEM((AT_TK, AT_TQ), F32)] * 4
                        + [pltpu.VMEM((8, AT_TQ), F32)]),
        compiler_params=_cparams(("arbitrary", "arbitrary", "arbitrary")),
    )(qk, qk, qk, kvt, lq1, lk1, lq2, lk2, norm_w)


MG_TM = 512


def _merge_kernel(hf_ref, hb_ref, mo_ref, ha_ref, gm_ref, ga_ref, nw_ref, wm_ref, wa_ref, out_ref):
    hm = hf_ref[...] + hb_ref[...]
    parts = []
    for h in range(M_HEADS):
        seg = hm[:, h * M_DV:(h + 1) * M_DV]
        ms = jnp.mean(seg * seg, axis=-1, keepdims=True)
        parts.append(seg * lax.rsqrt(ms + EPS))
    hn = jnp.concatenate(parts, axis=1) * nw_ref[...]
    hn = (hn * mo_ref[...].astype(F32)).astype(BF16)
    branch_m = _dot(hn, wm_ref[...])
    branch_a = _dot(ha_ref[...], wa_ref[...])
    mixed = gm_ref[...].astype(F32) * branch_m + ga_ref[...].astype(F32) * branch_a
    out_ref[...] = mixed.astype(BF16)


def _merge(hf, hb, sig, ha, norm_w, w_m, w_a):
    t_rows = hf.shape[0]
    row = lambda i: (i, 0)
    const = lambda i: (0, 0)
    return pl.pallas_call(
        _merge_kernel,
        name="merge",
        grid=(t_rows // MG_TM,),
        in_specs=[
            pl.BlockSpec((MG_TM, M_WIDTH), row),
            pl.BlockSpec((MG_TM, M_WIDTH), row),
            pl.BlockSpec((MG_TM, M_WIDTH), lambda i: (i, N_BRANCH_GATES // M_WIDTH)),
            pl.BlockSpec((MG_TM, A_WIDTH), row),
            pl.BlockSpec((MG_TM, D_MODEL), lambda i: (i, 0)),
            pl.BlockSpec((MG_TM, D_MODEL), lambda i: (i, 1)),
            pl.BlockSpec((1, M_WIDTH), const),
            pl.BlockSpec((M_WIDTH, D_MODEL), const),
            pl.BlockSpec((A_WIDTH, D_MODEL), const),
        ],
        out_specs=pl.BlockSpec((MG_TM, D_MODEL), row),
        out_shape=jax.ShapeDtypeStruct((t_rows, D_MODEL), BF16),
        compiler_params=_cparams(("parallel",)),
    )(hf, hb, sig, ha, sig, sig, norm_w, w_m, w_a)


OP_TM = 512


def _outproj_kernel(mixed_ref, x_ref, w_ref, nw_ref, x1_ref, h2_ref):
    x1 = x_ref[...] + _dot(mixed_ref[...], w_ref[...])
    x1_ref[...] = x1
    ms = jnp.mean(x1 * x1, axis=-1, keepdims=True)
    h2_ref[...] = (x1 * lax.rsqrt(ms + EPS) * nw_ref[...]).astype(BF16)


def _outproj(mixed, x2, w_out, norm_w):
    t_rows = x2.shape[0]
    row = lambda i: (i, 0)
    const = lambda i: (0, 0)
    return pl.pallas_call(
        _outproj_kernel,
        name="outproj",
        grid=(t_rows // OP_TM,),
        in_specs=[
            pl.BlockSpec((OP_TM, D_MODEL), row),
            pl.BlockSpec((OP_TM, D_MODEL), row),
            pl.BlockSpec((D_MODEL, D_MODEL), const),
            pl.BlockSpec((1, D_MODEL), const),
        ],
        out_specs=[pl.BlockSpec((OP_TM, D_MODEL), row), pl.BlockSpec((OP_TM, D_MODEL), row)],
        out_shape=[jax.ShapeDtypeStruct((t_rows, D_MODEL), F32),
                   jax.ShapeDtypeStruct((t_rows, D_MODEL), BF16)],
        compiler_params=_cparams(("parallel",)),
    )(mixed, x2, w_out, norm_w)


FI_TM = 1024
FI_TN = 512


def _ffn_in_kernel(h_ref, wg_ref, wu_ref, out_ref):
    h = h_ref[...]
    gate = _dot(h, wg_ref[...])
    up = _dot(h, wu_ref[...])
    out_ref[...] = (gate * _sigmoid(gate) * up).astype(BF16)


def _ffn_in(h2, w_ffn_in):
    t_rows = h2.shape[0]
    nj = D_FF // FI_TN
    return pl.pallas_call(
        _ffn_in_kernel,
        name="ffn_in",
        grid=(t_rows // FI_TM, nj),
        in_specs=[
            pl.BlockSpec((FI_TM, D_MODEL), lambda i, j: (i, 0)),
            pl.BlockSpec((D_MODEL, FI_TN), lambda i, j: (0, j)),
            pl.BlockSpec((D_MODEL, FI_TN), lambda i, j: (0, nj + j)),
        ],
        out_specs=pl.BlockSpec((FI_TM, FI_TN), lambda i, j: (i, j)),
        out_shape=jax.ShapeDtypeStruct((t_rows, D_FF), BF16),
        compiler_params=_cparams(("parallel", "arbitrary")),
    )(h2, w_ffn_in, w_ffn_in)


FO_TM = 512
FO_TN = 512


def _ffn_out_kernel(act_ref, w_ref, x1_ref, nw_ref, out_ref):
    j = pl.program_id(1)
    cols = pl.ds(pl.multiple_of(j * FO_TN, FO_TN), FO_TN)
    out_ref[:, cols] = x1_ref[:, cols] + _dot(act_ref[...], w_ref[...])

    @pl.when(j == pl.num_programs(1) - 1)
    def _():
        x2 = out_ref[...]
        ms = jnp.mean(x2 * x2, axis=-1, keepdims=True)
        out_ref[...] = x2 * lax.rsqrt(ms + EPS) * nw_ref[...]


def _ffn_out(act, w_ffn_out, x1, norm_w):
    t_rows = x1.shape[0]
    return pl.pallas_call(
        _ffn_out_kernel,
        name="ffn_out",
        grid=(t_rows // FO_TM, D_MODEL // FO_TN),
        in_specs=[
            pl.BlockSpec((FO_TM, D_FF), lambda i, j: (i, 0)),
            pl.BlockSpec((D_FF, FO_TN), lambda i, j: (0, j)),
            pl.BlockSpec((FO_TM, D_MODEL), lambda i, j: (i, 0)),
            pl.BlockSpec((1, D_MODEL), lambda i, j: (0, 0)),
        ],
        out_specs=pl.BlockSpec((FO_TM, D_MODEL), lambda i, j: (i, 0)),
        out_shape=jax.ShapeDtypeStruct((t_rows, D_MODEL), F32),
        compiler_params=_cparams(("parallel", "arbitrary")),
    )(act, w_ffn_out, x1, norm_w)


def _rope_tables(seq):
    inv = ROPE_THETA ** (-jnp.arange(0, A_DH, 2, dtype=F32) / A_DH)
    ang = jnp.arange(seq, dtype=F32)[:, None] * inv[None, :]
    cos = jnp.cos(ang)
    sin = jnp.sin(ang)
    cos_t = jnp.concatenate([cos, cos, cos, cos], axis=1)
    sin_t = jnp.concatenate([-sin, -sin, sin, sin], axis=1)
    return cos_t, sin_t


def _rotary_layout(w_seg):
    d = w_seg.shape[0]
    half = A_DH // 2
    return w_seg.reshape(d, A_HEADS, 2, 2, half).transpose(0, 1, 3, 2, 4).reshape(d, A_WIDTH)


def kernel(x, norm1_w, w_in, b_igate, b_fgate, b_branch_gate, mlstm_norm_w, lam_q1, lam_k1, lam_q2, lam_k2, attn_norm_w, w_branch_m, w_branch_a, w_out, norm2_w, w_ffn_in, w_ffn_out, final_norm_w):
    batch, seq, d = x.shape
    depth = w_in.shape[0]
    assert d == D_MODEL and depth == 1 and seq % PJ_TM == 0 and seq % AT_TQ == 0
    t_rows = batch * seq
    x2 = x.reshape(t_rows, d)
    cos_t, sin_t = _rope_tables(seq)

    l = 0
    w = w_in[l]
    qk_w = M_HEADS * M_DQK
    w_vq = jnp.concatenate([w[:, 2 * qk_w:OFF_MO], w[:, OFF_MQ:qk_w]], axis=1).astype(BF16)
    w_kvt = jnp.concatenate([w[:, qk_w:2 * qk_w], w[:, OFF_AV:OFF_GT]], axis=1).astype(BF16)
    w_sig = jnp.concatenate([w[:, OFF_GT:OFF_GT + N_BRANCH_GATES], w[:, OFF_MO:OFF_MG]], axis=1).astype(BF16)
    w_rot = jnp.concatenate([_rotary_layout(w[:, OFF_AQ:OFF_AK]),
                             _rotary_layout(w[:, OFF_AK:OFF_AV])], axis=1).astype(BF16)
    w_gate = jnp.pad(w[:, OFF_MG:OFF_AQ], ((0, 0), (0, LANES - N_GATE))).astype(BF16)

    scale_vq = jnp.ones((1, M_WIDTH + qk_w), F32)
    scale_kvt = jnp.concatenate([jnp.full((1, qk_w), M_DQK ** -0.5, F32), jnp.ones((1, A_WIDTH), F32)], axis=1)
    scale_rot = jnp.concatenate([jnp.full((1, A_WIDTH), Q_SCALE, F32), jnp.ones((1, A_WIDTH), F32)], axis=1)
    bias_sig = jnp.concatenate([b_branch_gate[l].astype(F32), jnp.zeros((M_WIDTH,), F32)]).reshape(1, -1)
    gate_bias = jnp.stack([b_igate[l], b_fgate[l]], axis=1).reshape(N_GATE).astype(F32)
    gate_bias = jnp.pad(gate_bias, (0, LANES - N_GATE))

    hn, gates = _rmsnorm(x2, norm1_w[l].reshape(1, d), w_gate)
    vq = _proj("scale", hn, w_vq, [scale_vq], seq, tn=PJ_TN_NARROW)
    sig = _proj("sigmoid", hn, w_sig, [bias_sig], seq)
    qk = _proj("rope", hn, w_rot, [cos_t, sin_t, scale_rot], seq)
    kvt = _proj("transpose", hn, w_kvt, [scale_kvt], seq, tn=PJ_TN_NARROW)

    hf, hb = _mlstm(vq, kvt, gates, gate_bias.reshape(1, LANES), gate_bias.reshape(LANES, 1), batch, seq)
    ha = _attention(qk, kvt, lam_q1[l].reshape(1, A_DH), lam_k1[l].reshape(1, A_DH),
                    lam_q2[l].reshape(1, A_DH), lam_k2[l].reshape(1, A_DH),
                    attn_norm_w[l].reshape(A_DV, 1), batch, seq)
    mixed = _merge(hf, hb, sig, ha, mlstm_norm_w[l].reshape(1, M_WIDTH),
                   w_branch_m[l].astype(BF16), w_branch_a[l].astype(BF16))
    x1, h2 = _outproj(mixed, x2, w_out[l].astype(BF16), norm2_w[l].reshape(1, d))
    act = _ffn_in(h2, w_ffn_in[l].astype(BF16))
    out = _ffn_out(act, w_ffn_out[l].astype(BF16), x1, final_norm_w.reshape(1, d))
    return out.reshape(batch, seq, d)
```

```python
import functools
import math

import jax
import jax.numpy as jnp
from jax import lax
from jax.experimental import pallas as pl
from jax.experimental.pallas import tpu as pltpu

F32 = jnp.float32
BF16 = jnp.bfloat16

D_MODEL = 2048
M_HEADS = 4
M_DQK = 128
M_DV = 256
M_CHUNK = 128
GATE_CAP = 15.0
A_HEADS = 8
A_DH = 64
A_DV = 2 * A_DH
ROPE_THETA = 10000.0
D_FF = 5632
EPS = 1e-6
M_WIDTH = M_HEADS * M_DV
A_WIDTH = A_HEADS * A_DV
N_BRANCH_GATES = 2 * D_MODEL
LAM_INIT = 0.8 - 0.6 * math.exp(-0.3 * 0)

OFF_MQ = 0
OFF_MO = 2 * M_HEADS * M_DQK + M_WIDTH
OFF_MG = OFF_MO + M_WIDTH
N_GATE = 4 * M_HEADS
OFF_AQ = OFF_MG + N_GATE
OFF_AK = OFF_AQ + A_WIDTH
OFF_AV = OFF_AK + A_WIDTH
OFF_GT = OFF_AV + A_WIDTH
LANES = 128

VMEM_LIMIT = 56 * 1024 * 1024


def _cparams(sem):
    return pltpu.CompilerParams(dimension_semantics=sem, vmem_limit_bytes=VMEM_LIMIT)


def _dot(a, b):
    return jnp.dot(a, b, preferred_element_type=F32)


def _dot_nt(a, b):
    return lax.dot_general(a, b, (((1,), (1,)), ((), ())), preferred_element_type=F32)


def _dot_tn(a, b):
    return lax.dot_general(a, b, (((0,), (0,)), ((), ())), preferred_element_type=F32)


def _sigmoid(x):
    return 0.5 * jnp.tanh(0.5 * x) + 0.5


NORM_TM = 512
PJ_TM = 1024
PJ_TN = 1024
PJ_TN_NARROW = 768
ROW_CHUNK = 256
Q_SCALE = (A_DH ** -0.5) * math.log2(math.e)


def _rmsnorm_kernel(x_ref, w_ref, wg_ref, o_ref, g_ref):
    x = x_ref[...]
    ms = jnp.mean(x * x, axis=-1, keepdims=True)
    hn = (x * lax.rsqrt(ms + EPS) * w_ref[...]).astype(BF16)
    o_ref[...] = hn
    g_ref[...] = _dot(hn, wg_ref[...])


def _rmsnorm(x2, norm_w, w_gate):
    t_rows, d = x2.shape
    return pl.pallas_call(
        _rmsnorm_kernel,
        name="rmsnorm",
        grid=(t_rows // NORM_TM,),
        in_specs=[pl.BlockSpec((NORM_TM, d), lambda i: (i, 0)),
                  pl.BlockSpec((1, d), lambda i: (0, 0)),
                  pl.BlockSpec((d, LANES), lambda i: (0, 0))],
        out_specs=[pl.BlockSpec((NORM_TM, d), lambda i: (i, 0)),
                   pl.BlockSpec((NORM_TM, LANES), lambda i: (i, 0))],
        out_shape=[jax.ShapeDtypeStruct((t_rows, d), BF16),
                   jax.ShapeDtypeStruct((t_rows, LANES), F32)],
        compiler_params=_cparams(("parallel",)),
    )(x2, norm_w, w_gate)


def _rope(acc, cos, sin_signed):
    outs = []
    for c in range(acc.shape[1] // LANES):
        t = acc[:, c * LANES:(c + 1) * LANES]
        outs.append(t * cos + pltpu.roll(t, LANES // 2, axis=1) * sin_signed)
    return jnp.concatenate(outs, axis=1)


def _proj_kernel(mode, h_ref, w_ref, *refs):
    o_ref = refs[-1]
    for r in range(PJ_TM // ROW_CHUNK):
        rows = slice(r * ROW_CHUNK, (r + 1) * ROW_CHUNK)
        acc = _dot(h_ref[rows, :], w_ref[...])
        if mode == "scale":
            o_ref[rows, :] = (acc * refs[0][...]).astype(BF16)
        elif mode == "sigmoid":
            o_ref[rows, :] = _sigmoid(acc + refs[0][...]).astype(BF16)
        elif mode == "rope":
            cos_ref, sin_ref, cs_ref = refs[:3]
            o_ref[rows, :] = (_rope(acc, cos_ref[rows, :], sin_ref[rows, :]) * cs_ref[...]).astype(BF16)
        else:
            assert mode == "transpose"
            o_ref[0, :, rows] = (acc * refs[0][...]).T.astype(BF16)


def _proj(mode, hn, w, aux, seq, tn=PJ_TN):
    t_rows, d = hn.shape
    n = w.shape[1]
    tn = min(tn, n)
    s_blocks = seq // PJ_TM
    col = pl.BlockSpec((1, tn), lambda i, j: (0, j))
    pos = pl.BlockSpec((PJ_TM, LANES), lambda i, j: (i % s_blocks, 0))
    aux_specs = {"scale": [col], "sigmoid": [col], "rope": [pos, pos, col], "transpose": [col]}[mode]
    if mode == "transpose":
        out_spec = pl.BlockSpec((1, tn, PJ_TM), lambda i, j: (i // s_blocks, j, i % s_blocks))
        out_shape = jax.ShapeDtypeStruct((t_rows // seq, n, seq), BF16)
    else:
        out_spec = pl.BlockSpec((PJ_TM, tn), lambda i, j: (i, j))
        out_shape = jax.ShapeDtypeStruct((t_rows, n), BF16)
    return pl.pallas_call(
        functools.partial(_proj_kernel, mode),
        name="proj_" + mode,
        grid=(t_rows // PJ_TM, n // tn),
        in_specs=[pl.BlockSpec((PJ_TM, d), lambda i, j: (i, 0)),
                  pl.BlockSpec((d, tn), lambda i, j: (0, j))] + aux_specs,
        out_specs=out_spec,
        out_shape=out_shape,
        compiler_params=_cparams(("parallel", "arbitrary")),
    )(hn, w, *aux)


L = M_CHUNK
MS_SUB = 4
DV_EXT = M_DV + LANES


def _softcap(t):
    return GATE_CAP * jnp.tanh(t / GATE_CAP)


def _log_sigmoid(t):
    return jnp.minimum(t, 0.0) - jnp.log(1.0 + jnp.exp(-jnp.abs(t)))


def _gate_act(pre, is_forget):
    c = _softcap(pre)
    return jnp.where(is_forget, _log_sigmoid(c), c)


def _split_dot(a, b, a_is_exact):
    if a_is_exact:
        hi = b.astype(BF16)
        lo = (b - hi.astype(F32)).astype(BF16)
        ab = a.astype(BF16)
        return _dot(ab, hi) + _dot(ab, lo)
    hi = a.astype(BF16)
    lo = (a - hi.astype(F32)).astype(BF16)
    bb = b.astype(BF16)
    return _dot(hi, bb) + _dot(lo, bb)


def _mlstm_kernel(qf_ref, kf_ref, vf_ref, gf_ref, qb_ref, kb_ref, vb_ref, gb_ref,
                  brow_ref, bcol_ref, hf_ref, hb_ref, c_ref, m_ref):
    step = pl.program_id(1)

    @pl.when(step == 0)
    def _():
        c_ref[...] = jnp.zeros_like(c_ref)
        m_ref[...] = jnp.zeros_like(m_ref)

    row = lax.broadcasted_iota(jnp.int32, (L, L), 0)
    col = lax.broadcasted_iota(jnp.int32, (L, L), 1)
    lane_id = lax.broadcasted_iota(jnp.int32, (1, LANES), 1)
    sub_id = lax.broadcasted_iota(jnp.int32, (LANES, 1), 0)
    forget_lane = (lane_id % 8) >= 4
    forget_sub = (sub_id % 8) >= 4
    ones_ext = jnp.ones((L, LANES), BF16)

    dirs = ((qf_ref, kf_ref, vf_ref, gf_ref, hf_ref), (qb_ref, kb_ref, vb_ref, gb_ref, hb_ref))
    for sub, d in [(sub, d) for sub in range(MS_SUB) for d in range(2)]:
        q_blk, k_blk, v_blk, g_blk, h_blk = dirs[d]
        r0 = (sub if d == 0 else MS_SUB - 1 - sub) * L
        q_ref, v_ref, g_ref, h_ref = (ref.at[r0:r0 + L, :] for ref in (q_blk, v_blk, g_blk, h_blk))
        kt_ref = k_blk.at[0, :, r0:r0 + L]
        visible = (row >= col) if d == 0 else (col >= row)
        vis_f = visible.astype(F32)

        g = g_ref[...]
        g_t = g.T
        act_c = _gate_act(g + brow_ref[...], forget_lane)
        act_r = _gate_act(g_t + bcol_ref[...], forget_sub)
        cum_c = _split_dot(vis_f, act_c, True)
        cum_r = _split_dot(act_r, vis_f.T, False)

        for h in range(M_HEADS):
            idx = d * M_HEADS + h
            ci = d * 8 + h
            cf = d * 8 + 4 + h
            bc = cum_c[:, cf:cf + 1]
            br = cum_r[cf:cf + 1, :]
            igr = act_r[ci:ci + 1, :]
            b_last = br[:, L - 1:L] if d == 0 else br[:, 0:1]
            m_old = m_ref[idx][0:1, 0:1]

            q = q_ref[:, h * M_DQK:(h + 1) * M_DQK]
            kt = kt_ref[h * M_DQK:(h + 1) * M_DQK, :]
            v_ext = jnp.concatenate([v_ref[:, h * M_DV:(h + 1) * M_DV], ones_ext], axis=1)

            dmat = jnp.where(visible, bc - br + igr, -jnp.inf)
            m_loc = jnp.max(dmat, axis=1, keepdims=True)
            s = _dot(q, kt) * jnp.exp(dmat - m_loc)
            sv = _dot(s.astype(BF16), v_ext)
            g_row = b_last - br + igr
            mg = jnp.max(g_row, axis=1, keepdims=True)
            kw_t = (kt.astype(F32) * jnp.exp(g_row - mg)).astype(BF16)
            u = _dot(kw_t, v_ext)

            c_old = c_ref[idx]
            inter = bc + m_old
            m_t = jnp.maximum(inter, m_loc)
            comb = jnp.exp(inter - m_t) * _dot(q, c_old.astype(BF16)) + jnp.exp(m_loc - m_t) * sv
            num = comb[:, :M_DV]
            den = comb[:, M_DV:M_DV + 1]
            hval = num / jnp.maximum(jnp.abs(den), jnp.exp(-m_t))
            h_ref[:, h * M_DV:(h + 1) * M_DV] = hval

            m_new = jnp.maximum(b_last + m_old, mg)
            c_ref[idx] = jnp.exp(b_last + m_old - m_new) * c_old + jnp.exp(mg - m_new) * u
            m_ref[idx] = jnp.broadcast_to(m_new, (8, LANES))


def _mlstm(vq, kvt, gates, bias_row, bias_col, batch, seq):
    t_rows = vq.shape[0]
    rows = MS_SUB * L
    nc = seq // rows
    fwd = lambda b, c: b * nc + c
    bwd = lambda b, c: b * nc + (nc - 1 - c)
    qk_w = M_HEADS * M_DQK
    in_specs = []
    for ch in (fwd, bwd):
        in_specs += [
            pl.BlockSpec((rows, qk_w), lambda b, c, ch=ch: (ch(b, c), M_WIDTH // qk_w)),
            pl.BlockSpec((1, qk_w, rows), lambda b, c, ch=ch: (b, 0, ch(0, c))),
            pl.BlockSpec((rows, M_WIDTH), lambda b, c, ch=ch: (ch(b, c), 0)),
            pl.BlockSpec((rows, LANES), lambda b, c, ch=ch: (ch(b, c), 0)),
        ]
    in_specs += [pl.BlockSpec((1, LANES), lambda b, c: (0, 0)),
                 pl.BlockSpec((LANES, 1), lambda b, c: (0, 0))]
    return pl.pallas_call(
        _mlstm_kernel,
        name="mlstm",
        grid=(batch, nc),
        in_specs=in_specs,
        out_specs=[pl.BlockSpec((rows, M_WIDTH), lambda b, c: (fwd(b, c), 0)),
                   pl.BlockSpec((rows, M_WIDTH), lambda b, c: (bwd(b, c), 0))],
        out_shape=[jax.ShapeDtypeStruct((t_rows, M_WIDTH), F32)] * 2,
        scratch_shapes=[pltpu.VMEM((2 * M_HEADS, M_DQK, DV_EXT), F32),
                        pltpu.VMEM((2 * M_HEADS, 8, LANES), F32)],
        compiler_params=_cparams(("parallel", "arbitrary")),
    )(vq, kvt, vq, gates, vq, kvt, vq, gates, bias_row, bias_col)


AT_TQ = 1024
AT_TK = 1024


def _attn_kernel(q_ref, qn_ref, k_ref, vt_ref, lq1_ref, lk1_ref, lq2_ref, lk2_ref, nw_ref,
                 o_ref, acc1_ref, acc2_ref, sa1_ref, sa2_ref, sb1_ref, sb2_ref, mba_ref):
    seq = k_ref.shape[0]
    nblk = seq // AT_TK
    qi = pl.program_id(2)
    lane = lax.broadcasted_iota(jnp.int32, (1, LANES), 1)
    in_map1 = (lane % A_DH) < (A_DH // 2)

    def split_maps(q):
        zero = jnp.zeros_like(q)
        return jnp.where(in_map1, q, zero), jnp.where(in_map1, zero, q)

    q_cur = split_maps(q_ref[...])
    acc1_ref[...] = jnp.zeros_like(acc1_ref)
    acc2_ref[...] = jnp.zeros_like(acc2_ref)

    def produce(i, qs, s1_ref, s2_ref):
        off = pl.multiple_of(i * AT_TK, AT_TK)
        kblk = k_ref[pl.ds(off, AT_TK), :]
        s1 = _dot_nt(kblk, qs[0])
        s1_ref[...] = s1
        s2 = _dot_nt(kblk, qs[1])
        s2_ref[...] = s2
        return jnp.max(s1, axis=0, keepdims=True), jnp.max(s2, axis=0, keepdims=True)

    def consume_map(s_ref, vtblk, mb, m, l, acc_ref):
        m_new = jnp.maximum(m, mb)
        alpha = jnp.exp2(m - m_new)
        p = jnp.exp2(s_ref[...] - m_new)
        l_new = alpha * l + jnp.sum(p, axis=0, keepdims=True)
        acc_ref[...] = alpha * acc_ref[...] + _dot(vtblk, p.astype(BF16))
        return m_new, l_new

    def consume(i, s1_ref, s2_ref, mb, stats):
        m1, l1, m2, l2 = stats
        off = pl.multiple_of(i * AT_TK, AT_TK)
        vtblk = vt_ref[0, :, pl.ds(off, AT_TK)]
        m1, l1 = consume_map(s1_ref, vtblk, mb[0], m1, l1, acc1_ref)
        m2, l2 = consume_map(s2_ref, vtblk, mb[1], m2, l2, acc2_ref)
        return m1, l1, m2, l2

    @pl.when(qi == 0)
    def _():
        mb = produce(0, q_cur, sa1_ref, sa2_ref)
        mba_ref[0:1, :] = mb[0]
        mba_ref[1:2, :] = mb[1]

    def body(j, carry):
        mb_a, stats = carry[:2], carry[2:]
        mb_b = produce(2 * j + 1, q_cur, sb1_ref, sb2_ref)
        stats = consume(2 * j, sa1_ref, sa2_ref, mb_a, stats)
        mb_a = produce(2 * j + 2, q_cur, sa1_ref, sa2_ref)
        stats = consume(2 * j + 1, sb1_ref, sb2_ref, mb_b, stats)
        return (*mb_a, *stats)

    neg = jnp.full((1, AT_TQ), -jnp.inf, F32)
    zer = jnp.zeros((1, AT_TQ), F32)
    carry = lax.fori_loop(0, nblk // 2 - 1, body,
                          (mba_ref[0:1, :], mba_ref[1:2, :], neg, zer, neg, zer))
    mb_a, stats = carry[:2], carry[2:]
    mb_b = produce(nblk - 1, q_cur, sb1_ref, sb2_ref)
    stats = consume(nblk - 2, sa1_ref, sa2_ref, mb_a, stats)
    mb_next = produce(0, split_maps(qn_ref[...]), sa1_ref, sa2_ref)
    mba_ref[0:1, :] = mb_next[0]
    mba_ref[1:2, :] = mb_next[1]
    m1, l1, m2, l2 = consume(nblk - 1, sb1_ref, sb2_ref, mb_b, stats)

    lam = (jnp.exp(jnp.sum(lq1_ref[...] * lk1_ref[...], axis=1, keepdims=True))
           - jnp.exp(jnp.sum(lq2_ref[...] * lk2_ref[...], axis=1, keepdims=True))
           + LAM_INIT)
    o = acc1_ref[...] * (1.0 / l1) - acc2_ref[...] * (lam / l2)
    ms = jnp.mean(o * o, axis=0, keepdims=True)
    y = o * lax.rsqrt(ms + EPS) * nw_ref[...] * (1.0 - LAM_INIT)
    o_ref[...] = y.T.astype(BF16)


def _attention(qk, kvt, lq1, lk1, lq2, lk2, norm_w, batch, seq):
    v_blk0 = (M_HEADS * M_DQK) // A_DV
    t_rows = qk.shape[0]
    nq = seq // AT_TQ
    small = pl.BlockSpec((1, A_DH), lambda b, h, i: (0, 0))
    return pl.pallas_call(
        _attn_kernel,
        name="attention",
        grid=(batch, A_HEADS, nq),
        in_specs=[
            pl.BlockSpec((AT_TQ, LANES), lambda b, h, i: (b * nq + i, h)),
            pl.BlockSpec((AT_TQ, LANES), lambda b, h, i: (b * nq + jnp.minimum(i + 1, nq - 1), h)),
            pl.BlockSpec((seq, LANES), lambda b, h, i: (b, A_HEADS + h)),
            pl.BlockSpec((1, A_DV, seq), lambda b, h, i: (b, v_blk0 + h, 0)),
            small, small, small, small,
            pl.BlockSpec((A_DV, 1), lambda b, h, i: (0, 0)),
        ],
        out_specs=pl.BlockSpec((AT_TQ, LANES), lambda b, h, i: (b * nq + i, h)),
        out_shape=jax.ShapeDtypeStruct((t_rows, A_WIDTH), BF16),
        scratch_shapes=([pltpu.VMEM((A_DV, AT_TQ), F32)] * 2 + [pltpu.VMEM((AT_TK, AT_TQ), F32)] * 4
                        + [pltpu.VMEM((8, AT_TQ), F32)]),
        compiler_params=_cparams(("arbitrary", "arbitrary", "arbitrary")),
    )(qk, qk, qk, kvt, lq1, lk1, lq2, lk2, norm_w)


MG_TM = 512


def _merge_kernel(hf_ref, hb_ref, mo_ref, ha_ref, gm_ref, ga_ref, nw_ref, wm_ref, wa_ref, out_ref):
    hm = hf_ref[...] + hb_ref[...]
    parts = []
    for h in range(M_HEADS):
        seg = hm[:, h * M_DV:(h + 1) * M_DV]
        ms = jnp.mean(seg * seg, axis=-1, keepdims=True)
        parts.append(seg * lax.rsqrt(ms + EPS))
    hn = jnp.concatenate(parts, axis=1) * nw_ref[...]
    hn = (hn * mo_ref[...].astype(F32)).astype(BF16)
    branch_m = _dot(hn, wm_ref[...])
    branch_a = _dot(ha_ref[...], wa_ref[...])
    mixed = gm_ref[...].astype(F32) * branch_m + ga_ref[...].astype(F32) * branch_a
    out_ref[...] = mixed.astype(BF16)


def _merge(hf, hb, sig, ha, norm_w, w_m, w_a):
    t_rows = hf.shape[0]
    row = lambda i: (i, 0)
    const = lambda i: (0, 0)
    return pl.pallas_call(
        _merge_kernel,
        name="merge",
        grid=(t_rows // MG_TM,),
        in_specs=[
            pl.BlockSpec((MG_TM, M_WIDTH), row),
            pl.BlockSpec((MG_TM, M_WIDTH), row),
            pl.BlockSpec((MG_TM, M_WIDTH), lambda i: (i, N_BRANCH_GATES // M_WIDTH)),
            pl.BlockSpec((MG_TM, A_WIDTH), row),
            pl.BlockSpec((MG_TM, D_MODEL), lambda i: (i, 0)),
            pl.BlockSpec((MG_TM, D_MODEL), lambda i: (i, 1)),
            pl.BlockSpec((1, M_WIDTH), const),
            pl.BlockSpec((M_WIDTH, D_MODEL), const),
            pl.BlockSpec((A_WIDTH, D_MODEL), const),
        ],
        out_specs=pl.BlockSpec((MG_TM, D_MODEL), row),
        out_shape=jax.ShapeDtypeStruct((t_rows, D_MODEL), BF16),
        compiler_params=_cparams(("parallel",)),
    )(hf, hb, sig, ha, sig, sig, norm_w, w_m, w_a)


OP_TM = 512


def _outproj_kernel(mixed_ref, x_ref, w_ref, nw_ref, x1_ref, h2_ref):
    x1 = x_ref[...] + _dot(mixed_ref[...], w_ref[...])
    x1_ref[...] = x1
    ms = jnp.mean(x1 * x1, axis=-1, keepdims=True)
    h2_ref[...] = (x1 * lax.rsqrt(ms + EPS) * nw_ref[...]).astype(BF16)


def _outproj(mixed, x2, w_out, norm_w):
    t_rows = x2.shape[0]
    row = lambda i: (i, 0)
    const = lambda i: (0, 0)
    return pl.pallas_call(
        _outproj_kernel,
        name="outproj",
        grid=(t_rows // OP_TM,),
        in_specs=[
            pl.BlockSpec((OP_TM, D_MODEL), row),
            pl.BlockSpec((OP_TM, D_MODEL), row),
            pl.BlockSpec((D_MODEL, D_MODEL), const),
            pl.BlockSpec((1, D_MODEL), const),
        ],
        out_specs=[pl.BlockSpec((OP_TM, D_MODEL), row), pl.BlockSpec((OP_TM, D_MODEL), row)],
        out_shape=[jax.ShapeDtypeStruct((t_rows, D_MODEL), F32),
                   jax.ShapeDtypeStruct((t_rows, D_MODEL), BF16)],
        compiler_params=_cparams(("parallel",)),
    )(mixed, x2, w_out, norm_w)


FI_TM = 1024
FI_TN = 512


def _ffn_in_kernel(h_ref, wg_ref, wu_ref, out_ref):
    h = h_ref[...]
    gate = _dot(h, wg_ref[...])
    up = _dot(h, wu_ref[...])
    out_ref[...] = (gate * _sigmoid(gate) * up).astype(BF16)


def _ffn_in(h2, w_ffn_in):
    t_rows = h2.shape[0]
    nj = D_FF // FI_TN
    return pl.pallas_call(
        _ffn_in_kernel,
        name="ffn_in",
        grid=(t_rows // FI_TM, nj),
        in_specs=[
            pl.BlockSpec((FI_TM, D_MODEL), lambda i, j: (i, 0)),
            pl.BlockSpec((D_MODEL, FI_TN), lambda i, j: (0, j)),
            pl.BlockSpec((D_MODEL, FI_TN), lambda i, j: (0, nj + j)),
        ],
        out_specs=pl.BlockSpec((FI_TM, FI_TN), lambda i, j: (i, j)),
        out_shape=jax.ShapeDtypeStruct((t_rows, D_FF), BF16),
        compiler_params=_cparams(("parallel", "arbitrary")),
    )(h2, w_ffn_in, w_ffn_in)


FO_TM = 1024
FO_TN = 256


def _ffn_out_kernel(act_ref, w_ref, x1_ref, nw_ref, out_ref):
    j = pl.program_id(1)
    cols = pl.ds(pl.multiple_of(j * FO_TN, FO_TN), FO_TN)
    out_ref[:, cols] = x1_ref[...] + _dot(act_ref[...], w_ref[...])

    @pl.when(j == pl.num_programs(1) - 1)
    def _():
        x2 = out_ref[...]
        ms = jnp.mean(x2 * x2, axis=-1, keepdims=True)
        out_ref[...] = x2 * lax.rsqrt(ms + EPS) * nw_ref[...]


def _ffn_out(act, w_ffn_out, x1, norm_w):
    t_rows = x1.shape[0]
    return pl.pallas_call(
        _ffn_out_kernel,
        name="ffn_out",
        grid=(t_rows // FO_TM, D_MODEL // FO_TN),
        in_specs=[
            pl.BlockSpec((FO_TM, D_FF), lambda i, j: (i, 0)),
            pl.BlockSpec((D_FF, FO_TN), lambda i, j: (0, j)),
            pl.BlockSpec((FO_TM, FO_TN), lambda i, j: (i, j)),
            pl.BlockSpec((1, D_MODEL), lambda i, j: (0, 0)),
        ],
        out_specs=pl.BlockSpec((FO_TM, D_MODEL), lambda i, j: (i, 0)),
        out_shape=jax.ShapeDtypeStruct((t_rows, D_MODEL), F32),
        compiler_params=_cparams(("parallel", "arbitrary")),
    )(act, w_ffn_out, x1, norm_w)


def _rope_tables(seq):
    inv = ROPE_THETA ** (-jnp.arange(0, A_DH, 2, dtype=F32) / A_DH)
    ang = jnp.arange(seq, dtype=F32)[:, None] * inv[None, :]
    cos = jnp.cos(ang)
    sin = jnp.sin(ang)
    cos_t = jnp.concatenate([cos, cos, cos, cos], axis=1)
    sin_t = jnp.concatenate([-sin, -sin, sin, sin], axis=1)
    return cos_t, sin_t


def _rotary_layout(w_seg):
    d = w_seg.shape[0]
    half = A_DH // 2
    return w_seg.reshape(d, A_HEADS, 2, 2, half).transpose(0, 1, 3, 2, 4).reshape(d, A_WIDTH)


def kernel(x, norm1_w, w_in, b_igate, b_fgate, b_branch_gate, mlstm_norm_w, lam_q1, lam_k1, lam_q2, lam_k2, attn_norm_w, w_branch_m, w_branch_a, w_out, norm2_w, w_ffn_in, w_ffn_out, final_norm_w):
    batch, seq, d = x.shape
    depth = w_in.shape[0]
    assert d == D_MODEL and depth == 1 and seq % PJ_TM == 0 and seq % AT_TQ == 0
    t_rows = batch * seq
    x2 = x.reshape(t_rows, d)
    cos_t, sin_t = _rope_tables(seq)

    l = 0
    w = w_in[l].astype(BF16)
    qk_w = M_HEADS * M_DQK
    w_vq = jnp.concatenate([w[:, 2 * qk_w:OFF_MO], w[:, OFF_MQ:qk_w]], axis=1)
    w_kvt = jnp.concatenate([w[:, qk_w:2 * qk_w], w[:, OFF_AV:OFF_GT]], axis=1)
    w_sig = jnp.concatenate([w[:, OFF_GT:OFF_GT + N_BRANCH_GATES], w[:, OFF_MO:OFF_MG]], axis=1)
    w_rot = jnp.concatenate([_rotary_layout(w[:, OFF_AQ:OFF_AK]), _rotary_layout(w[:, OFF_AK:OFF_AV])], axis=1)
    w_gate = jnp.pad(w[:, OFF_MG:OFF_AQ], ((0, 0), (0, LANES - N_GATE)))

    scale_vq = jnp.ones((1, M_WIDTH + qk_w), F32)
    scale_kvt = jnp.concatenate([jnp.full((1, qk_w), M_DQK ** -0.5, F32), jnp.ones((1, A_WIDTH), F32)], axis=1)
    scale_rot = jnp.concatenate([jnp.full((1, A_WIDTH), Q_SCALE, F32), jnp.ones((1, A_WIDTH), F32)], axis=1)
    bias_sig = jnp.concatenate([b_branch_gate[l].astype(F32), jnp.zeros((M_WIDTH,), F32)]).reshape(1, -1)
    gate_bias = jnp.stack([b_igate[l], b_fgate[l]], axis=1).reshape(N_GATE).astype(F32)
    gate_bias = jnp.pad(gate_bias, (0, LANES - N_GATE))

    hn, gates = _rmsnorm(x2, norm1_w[l].reshape(1, d), w_gate)
    vq = _proj("scale", hn, w_vq, [scale_vq], seq, tn=PJ_TN_NARROW)
    sig = _proj("sigmoid", hn, w_sig, [bias_sig], seq)
    qk = _proj("rope", hn, w_rot, [cos_t, sin_t, scale_rot], seq)
    kvt = _proj("transpose", hn, w_kvt, [scale_kvt], seq, tn=PJ_TN_NARROW)

    hf, hb = _mlstm(vq, kvt, gates, gate_bias.reshape(1, LANES), gate_bias.reshape(LANES, 1), batch, seq)
    ha = _attention(qk, kvt, lam_q1[l].reshape(1, A_DH), lam_k1[l].reshape(1, A_DH),
                    lam_q2[l].reshape(1, A_DH), lam_k2[l].reshape(1, A_DH),
                    attn_norm_w[l].reshape(A_DV, 1), batch, seq)
    mixed = _merge(hf, hb, sig, ha, mlstm_norm_w[l].reshape(1, M_WIDTH),
                   w_branch_m[l].astype(BF16), w_branch_a[l].astype(BF16))
    x1, h2 = _outproj(mixed, x2, w_out[l].astype(BF16), norm2_w[l].reshape(1, d))
    act = _ffn_in(h2, w_ffn_in[l].astype(BF16))
    out = _ffn_out(act, w_ffn_out[l].astype(BF16), x1, final_norm_w.reshape(1, d))
    return out.reshape(batch, seq, d)
```

```python
import functools
import math

import jax
import jax.numpy as jnp
from jax import lax
from jax.experimental import pallas as pl
from jax.experimental.pallas import tpu as pltpu

F32 = jnp.float32
BF16 = jnp.bfloat16

D_MODEL = 2048
M_HEADS = 4
M_DQK = 128
M_DV = 256
M_CHUNK = 128
GATE_CAP = 15.0
A_HEADS = 8
A_DH = 64
A_DV = 2 * A_DH
ROPE_THETA = 10000.0
D_FF = 5632
EPS = 1e-6
M_WIDTH = M_HEADS * M_DV
A_WIDTH = A_HEADS * A_DV
N_BRANCH_GATES = 2 * D_MODEL
LAM_INIT = 0.8 - 0.6 * math.exp(-0.3 * 0)

OFF_MQ = 0
OFF_MO = 2 * M_HEADS * M_DQK + M_WIDTH
OFF_MG = OFF_MO + M_WIDTH
N_GATE = 4 * M_HEADS
OFF_AQ = OFF_MG + N_GATE
OFF_AK = OFF_AQ + A_WIDTH
OFF_AV = OFF_AK + A_WIDTH
OFF_GT = OFF_AV + A_WIDTH
LANES = 128

VMEM_LIMIT = 56 * 1024 * 1024


def _cparams(sem):
    return pltpu.CompilerParams(dimension_semantics=sem, vmem_limit_bytes=VMEM_LIMIT)


def _dot(a, b):
    return jnp.dot(a, b, preferred_element_type=F32)


def _dot_nt(a, b):
    return lax.dot_general(a, b, (((1,), (1,)), ((), ())), preferred_element_type=F32)


def _dot_tn(a, b):
    return lax.dot_general(a, b, (((0,), (0,)), ((), ())), preferred_element_type=F32)


def _sigmoid(x):
    return 0.5 * jnp.tanh(0.5 * x) + 0.5


NORM_TM = 512
PJ_TM = 1024
PJ_TN = 1024
PJ_TN_NARROW = 768
ROW_CHUNK = 256
Q_SCALE = (A_DH ** -0.5) * math.log2(math.e)


def _rmsnorm_kernel(x_ref, w_ref, wg_ref, o_ref, g_ref):
    x = x_ref[...]
    ms = jnp.mean(x * x, axis=-1, keepdims=True)
    hn = (x * lax.rsqrt(ms + EPS) * w_ref[...]).astype(BF16)
    o_ref[...] = hn
    g_ref[...] = _dot(hn, wg_ref[...])


def _rmsnorm(x2, norm_w, w_gate):
    t_rows, d = x2.shape
    return pl.pallas_call(
        _rmsnorm_kernel,
        name="rmsnorm",
        grid=(t_rows // NORM_TM,),
        in_specs=[pl.BlockSpec((NORM_TM, d), lambda i: (i, 0)),
                  pl.BlockSpec((1, d), lambda i: (0, 0)),
                  pl.BlockSpec((d, LANES), lambda i: (0, 0))],
        out_specs=[pl.BlockSpec((NORM_TM, d), lambda i: (i, 0)),
                   pl.BlockSpec((NORM_TM, LANES), lambda i: (i, 0))],
        out_shape=[jax.ShapeDtypeStruct((t_rows, d), BF16),
                   jax.ShapeDtypeStruct((t_rows, LANES), F32)],
        compiler_params=_cparams(("parallel",)),
    )(x2, norm_w, w_gate)


def _rope(acc, cos, sin_signed):
    outs = []
    for c in range(acc.shape[1] // LANES):
        t = acc[:, c * LANES:(c + 1) * LANES]
        outs.append(t * cos + pltpu.roll(t, LANES // 2, axis=1) * sin_signed)
    return jnp.concatenate(outs, axis=1)


def _proj_kernel(mode, h_ref, w_ref, *refs):
    o_ref = refs[-1]
    for r in range(PJ_TM // ROW_CHUNK):
        rows = slice(r * ROW_CHUNK, (r + 1) * ROW_CHUNK)
        acc = _dot(h_ref[rows, :], w_ref[...])
        if mode == "scale":
            o_ref[rows, :] = (acc * refs[0][...]).astype(BF16)
        elif mode == "sigmoid":
            o_ref[rows, :] = _sigmoid(acc + refs[0][...]).astype(BF16)
        elif mode == "rope":
            cos_ref, sin_ref, cs_ref = refs[:3]
            o_ref[rows, :] = (_rope(acc, cos_ref[rows, :], sin_ref[rows, :]) * cs_ref[...]).astype(BF16)
        else:
            assert mode == "transpose"
            o_ref[0, :, rows] = (acc * refs[0][...]).T.astype(BF16)


def _proj(mode, hn, w, aux, seq, tn=PJ_TN):
    t_rows, d = hn.shape
    n = w.shape[1]
    tn = min(tn, n)
    s_blocks = seq // PJ_TM
    col = pl.BlockSpec((1, tn), lambda i, j: (0, j))
    pos = pl.BlockSpec((PJ_TM, LANES), lambda i, j: (i % s_blocks, 0))
    aux_specs = {"scale": [col], "sigmoid": [col], "rope": [pos, pos, col], "transpose": [col]}[mode]
    if mode == "transpose":
        out_spec = pl.BlockSpec((1, tn, PJ_TM), lambda i, j: (i // s_blocks, j, i % s_blocks))
        out_shape = jax.ShapeDtypeStruct((t_rows // seq, n, seq), BF16)
    else:
        out_spec = pl.BlockSpec((PJ_TM, tn), lambda i, j: (i, j))
        out_shape = jax.ShapeDtypeStruct((t_rows, n), BF16)
    return pl.pallas_call(
        functools.partial(_proj_kernel, mode),
        name="proj_" + mode,
        grid=(t_rows // PJ_TM, n // tn),
        in_specs=[pl.BlockSpec((PJ_TM, d), lambda i, j: (i, 0)),
                  pl.BlockSpec((d, tn), lambda i, j: (0, j))] + aux_specs,
        out_specs=out_spec,
        out_shape=out_shape,
        compiler_params=_cparams(("parallel", "arbitrary")),
    )(hn, w, *aux)


L = M_CHUNK
MS_SUB = 4
DV_EXT = M_DV + LANES


def _softcap(t):
    return GATE_CAP * jnp.tanh(t / GATE_CAP)


def _log_sigmoid(t):
    return jnp.minimum(t, 0.0) - jnp.log(1.0 + jnp.exp(-jnp.abs(t)))


def _gate_act(pre, is_forget):
    c = _softcap(pre)
    return jnp.where(is_forget, _log_sigmoid(c), c)


def _split_dot(a, b, a_is_exact):
    if a_is_exact:
        hi = b.astype(BF16)
        lo = (b - hi.astype(F32)).astype(BF16)
        ab = a.astype(BF16)
        return _dot(ab, hi) + _dot(ab, lo)
    hi = a.astype(BF16)
    lo = (a - hi.astype(F32)).astype(BF16)
    bb = b.astype(BF16)
    return _dot(hi, bb) + _dot(lo, bb)


def _mlstm_kernel(qf_ref, kf_ref, vf_ref, gf_ref, qb_ref, kb_ref, vb_ref, gb_ref,
                  brow_ref, bcol_ref, hf_ref, hb_ref, c_ref, m_ref):
    step = pl.program_id(1)

    @pl.when(step == 0)
    def _():
        c_ref[...] = jnp.zeros_like(c_ref)
        m_ref[...] = jnp.zeros_like(m_ref)

    row = lax.broadcasted_iota(jnp.int32, (L, L), 0)
    col = lax.broadcasted_iota(jnp.int32, (L, L), 1)
    lane_id = lax.broadcasted_iota(jnp.int32, (1, LANES), 1)
    sub_id = lax.broadcasted_iota(jnp.int32, (LANES, 1), 0)
    forget_lane = (lane_id % 8) >= 4
    forget_sub = (sub_id % 8) >= 4
    ones_ext = jnp.ones((L, LANES), BF16)

    dirs = ((qf_ref, kf_ref, vf_ref, gf_ref, hf_ref), (qb_ref, kb_ref, vb_ref, gb_ref, hb_ref))
    for sub, d in [(sub, d) for sub in range(MS_SUB) for d in range(2)]:
        q_blk, k_blk, v_blk, g_blk, h_blk = dirs[d]
        r0 = (sub if d == 0 else MS_SUB - 1 - sub) * L
        q_ref, v_ref, g_ref, h_ref = (ref.at[r0:r0 + L, :] for ref in (q_blk, v_blk, g_blk, h_blk))
        kt_ref = k_blk.at[0, :, r0:r0 + L]
        visible = (row >= col) if d == 0 else (col >= row)
        vis_f = visible.astype(F32)

        g = g_ref[...]
        g_t = g.T
        act_c = _gate_act(g + brow_ref[...], forget_lane)
        act_r = _gate_act(g_t + bcol_ref[...], forget_sub)
        cum_c = _split_dot(vis_f, act_c, True)
        cum_r = _split_dot(act_r, vis_f.T, False)

        for h in range(M_HEADS):
            idx = d * M_HEADS + h
            ci = d * 8 + h
            cf = d * 8 + 4 + h
            bc = cum_c[:, cf:cf + 1]
            br = cum_r[cf:cf + 1, :]
            igr = act_r[ci:ci + 1, :]
            b_last = br[:, L - 1:L] if d == 0 else br[:, 0:1]
            m_old = m_ref[idx][0:1, 0:1]

            q = q_ref[:, h * M_DQK:(h + 1) * M_DQK]
            kt = kt_ref[h * M_DQK:(h + 1) * M_DQK, :]
            v_ext = jnp.concatenate([v_ref[:, h * M_DV:(h + 1) * M_DV], ones_ext], axis=1)

            dmat = jnp.where(visible, bc - br + igr, -jnp.inf)
            m_loc = jnp.max(dmat, axis=1, keepdims=True)
            s = _dot(q, kt) * jnp.exp(dmat - m_loc)
            sv = _dot(s.astype(BF16), v_ext)
            g_row = b_last - br + igr
            mg = jnp.max(g_row, axis=1, keepdims=True)
            kw_t = (kt.astype(F32) * jnp.exp(g_row - mg)).astype(BF16)
            u = _dot(kw_t, v_ext)

            c_old = c_ref[idx]
            inter = bc + m_old
            m_t = jnp.maximum(inter, m_loc)
            comb = jnp.exp(inter - m_t) * _dot(q, c_old.astype(BF16)) + jnp.exp(m_loc - m_t) * sv
            num = comb[:, :M_DV]
            den = comb[:, M_DV:M_DV + 1]
            hval = num / jnp.maximum(jnp.abs(den), jnp.exp(-m_t))
            h_ref[:, h * M_DV:(h + 1) * M_DV] = hval

            m_new = jnp.maximum(b_last + m_old, mg)
            c_ref[idx] = jnp.exp(b_last + m_old - m_new) * c_old + jnp.exp(mg - m_new) * u
            m_ref[idx] = jnp.broadcast_to(m_new, (8, LANES))


def _mlstm(vq, kvt, gates, bias_row, bias_col, batch, seq):
    t_rows = vq.shape[0]
    rows = MS_SUB * L
    nc = seq // rows
    fwd = lambda b, c: b * nc + c
    bwd = lambda b, c: b * nc + (nc - 1 - c)
    qk_w = M_HEADS * M_DQK
    in_specs = []
    for ch in (fwd, bwd):
        in_specs += [
            pl.BlockSpec((rows, qk_w), lambda b, c, ch=ch: (ch(b, c), M_WIDTH // qk_w)),
            pl.BlockSpec((1, qk_w, rows), lambda b, c, ch=ch: (b, 0, ch(0, c))),
            pl.BlockSpec((rows, M_WIDTH), lambda b, c, ch=ch: (ch(b, c), 0)),
            pl.BlockSpec((rows, LANES), lambda b, c, ch=ch: (ch(b, c), 0)),
        ]
    in_specs += [pl.BlockSpec((1, LANES), lambda b, c: (0, 0)),
                 pl.BlockSpec((LANES, 1), lambda b, c: (0, 0))]
    return pl.pallas_call(
        _mlstm_kernel,
        name="mlstm",
        grid=(batch, nc),
        in_specs=in_specs,
        out_specs=[pl.BlockSpec((rows, M_WIDTH), lambda b, c: (fwd(b, c), 0)),
                   pl.BlockSpec((rows, M_WIDTH), lambda b, c: (bwd(b, c), 0))],
        out_shape=[jax.ShapeDtypeStruct((t_rows, M_WIDTH), F32)] * 2,
        scratch_shapes=[pltpu.VMEM((2 * M_HEADS, M_DQK, DV_EXT), F32),
                        pltpu.VMEM((2 * M_HEADS, 8, LANES), F32)],
        compiler_params=_cparams(("parallel", "arbitrary")),
    )(vq, kvt, vq, gates, vq, kvt, vq, gates, bias_row, bias_col)


AT_TQ = 1024
AT_TK = 1024
AT_CG = 256


def _attn_kernel(q_ref, qn_ref, k_ref, vt_ref, lq1_ref, lk1_ref, lq2_ref, lk2_ref, nw_ref,
                 o_ref, acc1_ref, acc2_ref, sa1_ref, sa2_ref, sb1_ref, sb2_ref, mba_ref):
    seq = k_ref.shape[0]
    nblk = seq // AT_TK
    qi = pl.program_id(2)
    lane = lax.broadcasted_iota(jnp.int32, (1, LANES), 1)
    in_map1 = (lane % A_DH) < (A_DH // 2)

    def split_maps(q):
        zero = jnp.zeros_like(q)
        return jnp.where(in_map1, q, zero), jnp.where(in_map1, zero, q)

    q_cur = split_maps(q_ref[...])
    acc1_ref[...] = jnp.zeros_like(acc1_ref)
    acc2_ref[...] = jnp.zeros_like(acc2_ref)

    groups = [slice(g * AT_CG, (g + 1) * AT_CG) for g in range(AT_TQ // AT_CG)]
    ng = len(groups)

    def produce(i, qs, s_refs):
        off = pl.multiple_of(i * AT_TK, AT_TK)
        kblk = k_ref[pl.ds(off, AT_TK), :]
        mbs = []
        for qm, s_ref in zip(qs, s_refs):
            for gs in groups:
                s = _dot_nt(kblk, qm[gs, :])
                s_ref[:, gs] = s
                mbs.append(jnp.max(s, axis=0, keepdims=True))
        return tuple(mbs)

    def consume(i, s_refs, mbs, stats):
        off = pl.multiple_of(i * AT_TK, AT_TK)
        vtblk = vt_ref[0, :, pl.ds(off, AT_TK)]
        out = []
        for mi, (s_ref, acc_ref) in enumerate(zip(s_refs, (acc1_ref, acc2_ref))):
            for gi, gs in enumerate(groups):
                idx = mi * ng + gi
                m, l = stats[2 * idx], stats[2 * idx + 1]
                m_new = jnp.maximum(m, mbs[idx])
                alpha = jnp.exp2(m - m_new)
                p = jnp.exp2(s_ref[:, gs] - m_new)
                l_new = alpha * l + jnp.sum(p, axis=0, keepdims=True)
                acc_ref[:, gs] = alpha * acc_ref[:, gs] + _dot(vtblk, p.astype(BF16))
                out += [m_new, l_new]
        return tuple(out)

    slot_a, slot_b = (sa1_ref, sa2_ref), (sb1_ref, sb2_ref)

    def save_maxima(mbs):
        for idx, mb in enumerate(mbs):
            mba_ref[idx // ng:idx // ng + 1, groups[idx % ng]] = mb

    @pl.when(qi == 0)
    def _():
        save_maxima(produce(0, q_cur, slot_a))

    def body(j, carry):
        mb_a, stats = carry[:2 * ng], carry[2 * ng:]
        mb_b = produce(2 * j + 1, q_cur, slot_b)
        stats = consume(2 * j, slot_a, mb_a, stats)
        mb_a = produce(2 * j + 2, q_cur, slot_a)
        stats = consume(2 * j + 1, slot_b, mb_b, stats)
        return (*mb_a, *stats)

    neg = jnp.full((1, AT_CG), -jnp.inf, F32)
    zer = jnp.zeros((1, AT_CG), F32)
    mb_a0 = tuple(mba_ref[idx // ng:idx // ng + 1, groups[idx % ng]] for idx in range(2 * ng))
    carry = lax.fori_loop(0, nblk // 2 - 1, body, (*mb_a0, *((neg, zer) * (2 * ng))))
    mb_a, stats = carry[:2 * ng], carry[2 * ng:]
    mb_b = produce(nblk - 1, q_cur, slot_b)
    stats = consume(nblk - 2, slot_a, mb_a, stats)
    save_maxima(produce(0, split_maps(qn_ref[...]), slot_a))
    stats = consume(nblk - 1, slot_b, mb_b, stats)
    l1 = jnp.concatenate([stats[2 * g + 1] for g in range(ng)], axis=1)
    l2 = jnp.concatenate([stats[2 * (ng + g) + 1] for g in range(ng)], axis=1)

    lam = (jnp.exp(jnp.sum(lq1_ref[...] * lk1_ref[...], axis=1, keepdims=True))
           - jnp.exp(jnp.sum(lq2_ref[...] * lk2_ref[...], axis=1, keepdims=True))
           + LAM_INIT)
    o = acc1_ref[...] * (1.0 / l1) - acc2_ref[...] * (lam / l2)
    ms = jnp.mean(o * o, axis=0, keepdims=True)
    y = o * lax.rsqrt(ms + EPS) * nw_ref[...] * (1.0 - LAM_INIT)
    o_ref[...] = y.T.astype(BF16)


def _attention(qk, kvt, lq1, lk1, lq2, lk2, norm_w, batch, seq):
    v_blk0 = (M_HEADS * M_DQK) // A_DV
    t_rows = qk.shape[0]
    nq = seq // AT_TQ
    small = pl.BlockSpec((1, A_DH), lambda b, h, i: (0, 0))
    return pl.pallas_call(
        _attn_kernel,
        name="attention",
        grid=(batch, A_HEADS, nq),
        in_specs=[
            pl.BlockSpec((AT_TQ, LANES), lambda b, h, i: (b * nq + i, h)),
            pl.BlockSpec((AT_TQ, LANES), lambda b, h, i: (b * nq + jnp.minimum(i + 1, nq - 1), h)),
            pl.BlockSpec((seq, LANES), lambda b, h, i: (b, A_HEADS + h)),
            pl.BlockSpec((1, A_DV, seq), lambda b, h, i: (b, v_blk0 + h, 0)),
            small, small, small, small,
            pl.BlockSpec((A_DV, 1), lambda b, h, i: (0, 0)),
        ],
        out_specs=pl.BlockSpec((AT_TQ, LANES), lambda b, h, i: (b * nq + i, h)),
        out_shape=jax.ShapeDtypeStruct((t_rows, A_WIDTH), BF16),
        scratch_shapes=([pltpu.VMEM((A_DV, AT_TQ), F32)] * 2 + [pltpu.VMEM((AT_TK, AT_TQ), F32)] * 4
                        + [pltpu.VMEM((8, AT_TQ), F32)]),
        compiler_params=_cparams(("arbitrary", "arbitrary", "arbitrary")),
    )(qk, qk, qk, kvt, lq1, lk1, lq2, lk2, norm_w)


MG_TM = 512


def _merge_kernel(hf_ref, hb_ref, mo_ref, ha_ref, gm_ref, ga_ref, nw_ref, wm_ref, wa_ref, out_ref):
    hm = hf_ref[...] + hb_ref[...]
    parts = []
    for h in range(M_HEADS):
        seg = hm[:, h * M_DV:(h + 1) * M_DV]
        ms = jnp.mean(seg * seg, axis=-1, keepdims=True)
        parts.append(seg * lax.rsqrt(ms + EPS))
    hn = jnp.concatenate(parts, axis=1) * nw_ref[...]
    hn = (hn * mo_ref[...].astype(F32)).astype(BF16)
    branch_m = _dot(hn, wm_ref[...])
    branch_a = _dot(ha_ref[...], wa_ref[...])
    mixed = gm_ref[...].astype(F32) * branch_m + ga_ref[...].astype(F32) * branch_a
    out_ref[...] = mixed.astype(BF16)


def _merge(hf, hb, sig, ha, norm_w, w_m, w_a):
    t_rows = hf.shape[0]
    row = lambda i: (i, 0)
    const = lambda i: (0, 0)
    return pl.pallas_call(
        _merge_kernel,
        name="merge",
        grid=(t_rows // MG_TM,),
        in_specs=[
            pl.BlockSpec((MG_TM, M_WIDTH), row),
            pl.BlockSpec((MG_TM, M_WIDTH), row),
            pl.BlockSpec((MG_TM, M_WIDTH), lambda i: (i, N_BRANCH_GATES // M_WIDTH)),
            pl.BlockSpec((MG_TM, A_WIDTH), row),
            pl.BlockSpec((MG_TM, D_MODEL), lambda i: (i, 0)),
            pl.BlockSpec((MG_TM, D_MODEL), lambda i: (i, 1)),
            pl.BlockSpec((1, M_WIDTH), const),
            pl.BlockSpec((M_WIDTH, D_MODEL), const),
            pl.BlockSpec((A_WIDTH, D_MODEL), const),
        ],
        out_specs=pl.BlockSpec((MG_TM, D_MODEL), row),
        out_shape=jax.ShapeDtypeStruct((t_rows, D_MODEL), BF16),
        compiler_params=_cparams(("parallel",)),
    )(hf, hb, sig, ha, sig, sig, norm_w, w_m, w_a)


OP_TM = 512


def _outproj_kernel(mixed_ref, x_ref, w_ref, nw_ref, x1_ref, h2_ref):
    x1 = x_ref[...] + _dot(mixed_ref[...], w_ref[...])
    x1_ref[...] = x1
    ms = jnp.mean(x1 * x1, axis=-1, keepdims=True)
    h2_ref[...] = (x1 * lax.rsqrt(ms + EPS) * nw_ref[...]).astype(BF16)


def _outproj(mixed, x2, w_out, norm_w):
    t_rows = x2.shape[0]
    row = lambda i: (i, 0)
    const = lambda i: (0, 0)
    return pl.pallas_call(
        _outproj_kernel,
        name="outproj",
        grid=(t_rows // OP_TM,),
        in_specs=[
            pl.BlockSpec((OP_TM, D_MODEL), row),
            pl.BlockSpec((OP_TM, D_MODEL), row),
            pl.BlockSpec((D_MODEL, D_MODEL), const),
            pl.BlockSpec((1, D_MODEL), const),
        ],
        out_specs=[pl.BlockSpec((OP_TM, D_MODEL), row), pl.BlockSpec((OP_TM, D_MODEL), row)],
        out_shape=[jax.ShapeDtypeStruct((t_rows, D_MODEL), F32),
                   jax.ShapeDtypeStruct((t_rows, D_MODEL), BF16)],
        compiler_params=_cparams(("parallel",)),
    )(mixed, x2, w_out, norm_w)


FI_TM = 1024
FI_TN = 512


def _ffn_in_kernel(h_ref, wg_ref, wu_ref, out_ref):
    h = h_ref[...]
    gate = _dot(h, wg_ref[...])
    up = _dot(h, wu_ref[...])
    out_ref[...] = (gate * _sigmoid(gate) * up).astype(BF16)


def _ffn_in(h2, w_ffn_in):
    t_rows = h2.shape[0]
    nj = D_FF // FI_TN
    return pl.pallas_call(
        _ffn_in_kernel,
        name="ffn_in",
        grid=(t_rows // FI_TM, nj),
        in_specs=[
            pl.BlockSpec((FI_TM, D_MODEL), lambda i, j: (i, 0)),
            pl.BlockSpec((D_MODEL, FI_TN), lambda i, j: (0, j)),
            pl.BlockSpec((D_MODEL, FI_TN), lambda i, j: (0, nj + j)),
        ],
        out_specs=pl.BlockSpec((FI_TM, FI_TN), lambda i, j: (i, j)),
        out_shape=jax.ShapeDtypeStruct((t_rows, D_FF), BF16),
        compiler_params=_cparams(("parallel", "arbitrary")),
    )(h2, w_ffn_in, w_ffn_in)


FO_TM = 1024
FO_TN = 256


def _ffn_out_kernel(act_ref, w_ref, x1_ref, nw_ref, out_ref):
    j = pl.program_id(1)
    cols = pl.ds(pl.multiple_of(j * FO_TN, FO_TN), FO_TN)
    out_ref[:, cols] = x1_ref[...] + _dot(act_ref[...], w_ref[...])

    @pl.when(j == pl.num_programs(1) - 1)
    def _():
        x2 = out_ref[...]
        ms = jnp.mean(x2 * x2, axis=-1, keepdims=True)
        out_ref[...] = x2 * lax.rsqrt(ms + EPS) * nw_ref[...]


def _ffn_out(act, w_ffn_out, x1, norm_w):
    t_rows = x1.shape[0]
    return pl.pallas_call(
        _ffn_out_kernel,
        name="ffn_out",
        grid=(t_rows // FO_TM, D_MODEL // FO_TN),
        in_specs=[
            pl.BlockSpec((FO_TM, D_FF), lambda i, j: (i, 0)),
            pl.BlockSpec((D_FF, FO_TN), lambda i, j: (0, j)),
            pl.BlockSpec((FO_TM, FO_TN), lambda i, j: (i, j)),
            pl.BlockSpec((1, D_MODEL), lambda i, j: (0, 0)),
        ],
        out_specs=pl.BlockSpec((FO_TM, D_MODEL), lambda i, j: (i, 0)),
        out_shape=jax.ShapeDtypeStruct((t_rows, D_MODEL), F32),
        compiler_params=_cparams(("parallel", "arbitrary")),
    )(act, w_ffn_out, x1, norm_w)


def _rope_tables(seq):
    inv = ROPE_THETA ** (-jnp.arange(0, A_DH, 2, dtype=F32) / A_DH)
    ang = jnp.arange(seq, dtype=F32)[:, None] * inv[None, :]
    cos = jnp.cos(ang)
    sin = jnp.sin(ang)
    cos_t = jnp.concatenate([cos, cos, cos, cos], axis=1)
    sin_t = jnp.concatenate([-sin, -sin, sin, sin], axis=1)
    return cos_t, sin_t


def _rotary_layout(w_seg):
    d = w_seg.shape[0]
    half = A_DH // 2
    return w_seg.reshape(d, A_HEADS, 2, 2, half).transpose(0, 1, 3, 2, 4).reshape(d, A_WIDTH)


def kernel(x, norm1_w, w_in, b_igate, b_fgate, b_branch_gate, mlstm_norm_w, lam_q1, lam_k1, lam_q2, lam_k2, attn_norm_w, w_branch_m, w_branch_a, w_out, norm2_w, w_ffn_in, w_ffn_out, final_norm_w):
    batch, seq, d = x.shape
    depth = w_in.shape[0]
    assert d == D_MODEL and depth == 1 and seq % PJ_TM == 0 and seq % AT_TQ == 0
    t_rows = batch * seq
    x2 = x.reshape(t_rows, d)
    cos_t, sin_t = _rope_tables(seq)

    l = 0
    w = w_in[l].astype(BF16)
    qk_w = M_HEADS * M_DQK
    w_vq = jnp.concatenate([w[:, 2 * qk_w:OFF_MO], w[:, OFF_MQ:qk_w]], axis=1)
    w_kvt = jnp.concatenate([w[:, qk_w:2 * qk_w], w[:, OFF_AV:OFF_GT]], axis=1)
    w_sig = jnp.concatenate([w[:, OFF_GT:OFF_GT + N_BRANCH_GATES], w[:, OFF_MO:OFF_MG]], axis=1)
    w_rot = jnp.concatenate([_rotary_layout(w[:, OFF_AQ:OFF_AK]), _rotary_layout(w[:, OFF_AK:OFF_AV])], axis=1)
    w_gate = jnp.pad(w[:, OFF_MG:OFF_AQ], ((0, 0), (0, LANES - N_GATE)))

    scale_vq = jnp.ones((1, M_WIDTH + qk_w), F32)
    scale_kvt = jnp.concatenate([jnp.full((1, qk_w), M_DQK ** -0.5, F32), jnp.ones((1, A_WIDTH), F32)], axis=1)
    scale_rot = jnp.concatenate([jnp.full((1, A_WIDTH), Q_SCALE, F32), jnp.ones((1, A_WIDTH), F32)], axis=1)
    bias_sig = jnp.concatenate([b_branch_gate[l].astype(F32), jnp.zeros((M_WIDTH,), F32)]).reshape(1, -1)
    gate_bias = jnp.stack([b_igate[l], b_fgate[l]], axis=1).reshape(N_GATE).astype(F32)
    gate_bias = jnp.pad(gate_bias, (0, LANES - N_GATE))

    hn, gates = _rmsnorm(x2, norm1_w[l].reshape(1, d), w_gate)
    vq = _proj("scale", hn, w_vq, [scale_vq], seq, tn=PJ_TN_NARROW)
    sig = _proj("sigmoid", hn, w_sig, [bias_sig], seq)
    qk = _proj("rope", hn, w_rot, [cos_t, sin_t, scale_rot], seq)
    kvt = _proj("transpose", hn, w_kvt, [scale_kvt], seq, tn=PJ_TN_NARROW)

    hf, hb = _mlstm(vq, kvt, gates, gate_bias.reshape(1, LANES), gate_bias.reshape(LANES, 1), batch, seq)
    ha = _attention(qk, kvt, lam_q1[l].reshape(1, A_DH), lam_k1[l].reshape(1, A_DH),
                    lam_q2[l].reshape(1, A_DH), lam_k2[l].reshape(1, A_DH),
                    attn_norm_w[l].reshape(A_DV, 1), batch, seq)
    mixed = _merge(hf, hb, sig, ha, mlstm_norm_w[l].reshape(1, M_WIDTH),
                   w_branch_m[l].astype(BF16), w_branch_a[l].astype(BF16))
    x1, h2 = _outproj(mixed, x2, w_out[l].astype(BF16), norm2_w[l].reshape(1, d))
    act = _ffn_in(h2, w_ffn_in[l].astype(BF16))
    out = _ffn_out(act, w_ffn_out[l].astype(BF16), x1, final_norm_w.reshape(1, d))
    return out.reshape(batch, seq, d)
```

```python
import functools
import math

import jax
import jax.numpy as jnp
from jax import lax
from jax.experimental import pallas as pl
from jax.experimental.pallas import tpu as pltpu

F32 = jnp.float32
BF16 = jnp.bfloat16

D_MODEL = 2048
M_HEADS = 4
M_DQK = 128
M_DV = 256
M_CHUNK = 128
GATE_CAP = 15.0
A_HEADS = 8
A_DH = 64
A_DV = 2 * A_DH
ROPE_THETA = 10000.0
D_FF = 5632
EPS = 1e-6
M_WIDTH = M_HEADS * M_DV
A_WIDTH = A_HEADS * A_DV
N_BRANCH_GATES = 2 * D_MODEL
LAM_INIT = 0.8 - 0.6 * math.exp(-0.3 * 0)

OFF_MQ = 0
OFF_MO = 2 * M_HEADS * M_DQK + M_WIDTH
OFF_MG = OFF_MO + M_WIDTH
N_GATE = 4 * M_HEADS
OFF_AQ = OFF_MG + N_GATE
OFF_AK = OFF_AQ + A_WIDTH
OFF_AV = OFF_AK + A_WIDTH
OFF_GT = OFF_AV + A_WIDTH
LANES = 128

VMEM_LIMIT = 56 * 1024 * 1024


def _cparams(sem):
    return pltpu.CompilerParams(dimension_semantics=sem, vmem_limit_bytes=VMEM_LIMIT)


def _dot(a, b):
    return jnp.dot(a, b, preferred_element_type=F32)


def _dot_nt(a, b):
    return lax.dot_general(a, b, (((1,), (1,)), ((), ())), preferred_element_type=F32)


def _dot_tn(a, b):
    return lax.dot_general(a, b, (((0,), (0,)), ((), ())), preferred_element_type=F32)


def _sigmoid(x):
    return 0.5 * jnp.tanh(0.5 * x) + 0.5


NORM_TM = 512
PJ_TM = 2048
PJ_TN = 1024
PJ_TN_NARROW = 768
ROW_CHUNK = 256
Q_SCALE = (A_DH ** -0.5) * math.log2(math.e)


def _rmsnorm_kernel(x_ref, w_ref, wg_ref, o_ref, g_ref):
    x = x_ref[...]
    ms = jnp.mean(x * x, axis=-1, keepdims=True)
    hn = (x * lax.rsqrt(ms + EPS) * w_ref[...]).astype(BF16)
    o_ref[...] = hn
    g_ref[...] = _dot(hn, wg_ref[...])


def _rmsnorm(x2, norm_w, w_gate):
    t_rows, d = x2.shape
    return pl.pallas_call(
        _rmsnorm_kernel,
        name="rmsnorm",
        grid=(t_rows // NORM_TM,),
        in_specs=[pl.BlockSpec((NORM_TM, d), lambda i: (i, 0)),
                  pl.BlockSpec((1, d), lambda i: (0, 0)),
                  pl.BlockSpec((d, LANES), lambda i: (0, 0))],
        out_specs=[pl.BlockSpec((NORM_TM, d), lambda i: (i, 0)),
                   pl.BlockSpec((NORM_TM, LANES), lambda i: (i, 0))],
        out_shape=[jax.ShapeDtypeStruct((t_rows, d), BF16),
                   jax.ShapeDtypeStruct((t_rows, LANES), F32)],
        compiler_params=_cparams(("parallel",)),
    )(x2, norm_w, w_gate)


def _rope(acc, cos, sin_signed):
    outs = []
    for c in range(acc.shape[1] // LANES):
        t = acc[:, c * LANES:(c + 1) * LANES]
        outs.append(t * cos + pltpu.roll(t, LANES // 2, axis=1) * sin_signed)
    return jnp.concatenate(outs, axis=1)


def _proj_kernel(mode, h_ref, w_ref, *refs):
    o_ref = refs[-1]
    for r in range(PJ_TM // ROW_CHUNK):
        rows = slice(r * ROW_CHUNK, (r + 1) * ROW_CHUNK)
        acc = _dot(h_ref[rows, :], w_ref[...])
        if mode == "scale":
            o_ref[rows, :] = (acc * refs[0][...]).astype(BF16)
        elif mode == "sigmoid":
            o_ref[rows, :] = _sigmoid(acc + refs[0][...]).astype(BF16)
        elif mode == "rope":
            cos_ref, sin_ref, cs_ref = refs[:3]
            o_ref[rows, :] = (_rope(acc, cos_ref[rows, :], sin_ref[rows, :]) * cs_ref[...]).astype(BF16)
        else:
            assert mode == "transpose"
            o_ref[0, :, rows] = (acc * refs[0][...]).T.astype(BF16)


def _proj(mode, hn, w, aux, seq, tn=PJ_TN):
    t_rows, d = hn.shape
    n = w.shape[1]
    tn = min(tn, n)
    s_blocks = seq // PJ_TM
    col = pl.BlockSpec((1, tn), lambda i, j: (0, j))
    pos = pl.BlockSpec((PJ_TM, LANES), lambda i, j: (i % s_blocks, 0))
    aux_specs = {"scale": [col], "sigmoid": [col], "rope": [pos, pos, col], "transpose": [col]}[mode]
    if mode == "transpose":
        out_spec = pl.BlockSpec((1, tn, PJ_TM), lambda i, j: (i // s_blocks, j, i % s_blocks))
        out_shape = jax.ShapeDtypeStruct((t_rows // seq, n, seq), BF16)
    else:
        out_spec = pl.BlockSpec((PJ_TM, tn), lambda i, j: (i, j))
        out_shape = jax.ShapeDtypeStruct((t_rows, n), BF16)
    return pl.pallas_call(
        functools.partial(_proj_kernel, mode),
        name="proj_" + mode,
        grid=(t_rows // PJ_TM, n // tn),
        in_specs=[pl.BlockSpec((PJ_TM, d), lambda i, j: (i, 0)),
                  pl.BlockSpec((d, tn), lambda i, j: (0, j))] + aux_specs,
        out_specs=out_spec,
        out_shape=out_shape,
        compiler_params=_cparams(("parallel", "arbitrary")),
    )(hn, w, *aux)


L = M_CHUNK
MS_SUB = 4
DV_EXT = M_DV + LANES


def _softcap(t):
    return GATE_CAP * jnp.tanh(t / GATE_CAP)


def _log_sigmoid(t):
    return jnp.minimum(t, 0.0) - jnp.log(1.0 + jnp.exp(-jnp.abs(t)))


def _gate_act(pre, is_forget):
    c = _softcap(pre)
    return jnp.where(is_forget, _log_sigmoid(c), c)


def _split_dot(a, b, a_is_exact):
    if a_is_exact:
        hi = b.astype(BF16)
        lo = (b - hi.astype(F32)).astype(BF16)
        ab = a.astype(BF16)
        return _dot(ab, hi) + _dot(ab, lo)
    hi = a.astype(BF16)
    lo = (a - hi.astype(F32)).astype(BF16)
    bb = b.astype(BF16)
    return _dot(hi, bb) + _dot(lo, bb)


def _mlstm_kernel(qf_ref, kf_ref, vf_ref, gf_ref, qb_ref, kb_ref, vb_ref, gb_ref,
                  brow_ref, bcol_ref, hf_ref, hb_ref, c_ref, m_ref):
    step = pl.program_id(1)

    @pl.when(step == 0)
    def _():
        c_ref[...] = jnp.zeros_like(c_ref)
        m_ref[...] = jnp.zeros_like(m_ref)

    row = lax.broadcasted_iota(jnp.int32, (L, L), 0)
    col = lax.broadcasted_iota(jnp.int32, (L, L), 1)
    lane_id = lax.broadcasted_iota(jnp.int32, (1, LANES), 1)
    sub_id = lax.broadcasted_iota(jnp.int32, (LANES, 1), 0)
    forget_lane = (lane_id % 8) >= 4
    forget_sub = (sub_id % 8) >= 4
    ones_ext = jnp.ones((L, LANES), BF16)

    dirs = ((qf_ref, kf_ref, vf_ref, gf_ref, hf_ref), (qb_ref, kb_ref, vb_ref, gb_ref, hb_ref))
    for sub, d in [(sub, d) for sub in range(MS_SUB) for d in range(2)]:
        q_blk, k_blk, v_blk, g_blk, h_blk = dirs[d]
        r0 = (sub if d == 0 else MS_SUB - 1 - sub) * L
        q_ref, v_ref, g_ref, h_ref = (ref.at[r0:r0 + L, :] for ref in (q_blk, v_blk, g_blk, h_blk))
        kt_ref = k_blk.at[0, :, r0:r0 + L]
        visible = (row >= col) if d == 0 else (col >= row)
        vis_f = visible.astype(F32)

        g = g_ref[...]
        g_t = g.T
        act_c = _gate_act(g + brow_ref[...], forget_lane)
        act_r = _gate_act(g_t + bcol_ref[...], forget_sub)
        cum_c = _split_dot(vis_f, act_c, True)
        cum_r = _split_dot(act_r, vis_f.T, False)

        for h in range(M_HEADS):
            idx = d * M_HEADS + h
            ci = d * 8 + h
            cf = d * 8 + 4 + h
            bc = cum_c[:, cf:cf + 1]
            br = cum_r[cf:cf + 1, :]
            igr = act_r[ci:ci + 1, :]
            b_last = br[:, L - 1:L] if d == 0 else br[:, 0:1]
            m_old = m_ref[idx][0:1, 0:1]

            q = q_ref[:, h * M_DQK:(h + 1) * M_DQK]
            kt = kt_ref[h * M_DQK:(h + 1) * M_DQK, :]
            v_ext = jnp.concatenate([v_ref[:, h * M_DV:(h + 1) * M_DV], ones_ext], axis=1)

            dmat = jnp.where(visible, bc - br + igr, -jnp.inf)
            m_loc = jnp.max(dmat, axis=1, keepdims=True)
            s = _dot(q, kt) * jnp.exp(dmat - m_loc)
            sv = _dot(s.astype(BF16), v_ext)
            g_row = b_last - br + igr
            mg = jnp.max(g_row, axis=1, keepdims=True)
            kw_t = (kt.astype(F32) * jnp.exp(g_row - mg)).astype(BF16)
            u = _dot(kw_t, v_ext)

            c_old = c_ref[idx]
            inter = bc + m_old
            m_t = jnp.maximum(inter, m_loc)
            comb = jnp.exp(inter - m_t) * _dot(q, c_old.astype(BF16)) + jnp.exp(m_loc - m_t) * sv
            num = comb[:, :M_DV]
            den = comb[:, M_DV:M_DV + 1]
            hval = num / jnp.maximum(jnp.abs(den), jnp.exp(-m_t))
            h_ref[:, h * M_DV:(h + 1) * M_DV] = hval

            m_new = jnp.maximum(b_last + m_old, mg)
            c_ref[idx] = jnp.exp(b_last + m_old - m_new) * c_old + jnp.exp(mg - m_new) * u
            m_ref[idx] = jnp.broadcast_to(m_new, (8, LANES))


def _mlstm(vq, kvt, gates, bias_row, bias_col, batch, seq):
    t_rows = vq.shape[0]
    rows = MS_SUB * L
    nc = seq // rows
    fwd = lambda b, c: b * nc + c
    bwd = lambda b, c: b * nc + (nc - 1 - c)
    qk_w = M_HEADS * M_DQK
    in_specs = []
    for ch in (fwd, bwd):
        in_specs += [
            pl.BlockSpec((rows, qk_w), lambda b, c, ch=ch: (ch(b, c), M_WIDTH // qk_w)),
            pl.BlockSpec((1, qk_w, rows), lambda b, c, ch=ch: (b, 0, ch(0, c))),
            pl.BlockSpec((rows, M_WIDTH), lambda b, c, ch=ch: (ch(b, c), 0)),
            pl.BlockSpec((rows, LANES), lambda b, c, ch=ch: (ch(b, c), 0)),
        ]
    in_specs += [pl.BlockSpec((1, LANES), lambda b, c: (0, 0)),
                 pl.BlockSpec((LANES, 1), lambda b, c: (0, 0))]
    return pl.pallas_call(
        _mlstm_kernel,
        name="mlstm",
        grid=(batch, nc),
        in_specs=in_specs,
        out_specs=[pl.BlockSpec((rows, M_WIDTH), lambda b, c: (fwd(b, c), 0)),
                   pl.BlockSpec((rows, M_WIDTH), lambda b, c: (bwd(b, c), 0))],
        out_shape=[jax.ShapeDtypeStruct((t_rows, M_WIDTH), F32)] * 2,
        scratch_shapes=[pltpu.VMEM((2 * M_HEADS, M_DQK, DV_EXT), F32),
                        pltpu.VMEM((2 * M_HEADS, 8, LANES), F32)],
        compiler_params=_cparams(("parallel", "arbitrary")),
    )(vq, kvt, vq, gates, vq, kvt, vq, gates, bias_row, bias_col)


AT_TQ = 1024
AT_TK = 1024
AT_CG = 256


def _attn_kernel(q_ref, qn_ref, k_ref, vt_ref, lq1_ref, lk1_ref, lq2_ref, lk2_ref, nw_ref,
                 o_ref, acc1_ref, acc2_ref, sa1_ref, sa2_ref, sb1_ref, sb2_ref, mba_ref):
    seq = k_ref.shape[0]
    nblk = seq // AT_TK
    qi = pl.program_id(2)
    lane = lax.broadcasted_iota(jnp.int32, (1, LANES), 1)
    in_map1 = (lane % A_DH) < (A_DH // 2)

    def split_maps(q):
        zero = jnp.zeros_like(q)
        return jnp.where(in_map1, q, zero), jnp.where(in_map1, zero, q)

    q_cur = split_maps(q_ref[...])
    acc1_ref[...] = jnp.zeros_like(acc1_ref)
    acc2_ref[...] = jnp.zeros_like(acc2_ref)

    groups = [slice(g * AT_CG, (g + 1) * AT_CG) for g in range(AT_TQ // AT_CG)]
    ng = len(groups)

    def produce(i, qs, s_refs):
        off = pl.multiple_of(i * AT_TK, AT_TK)
        kblk = k_ref[pl.ds(off, AT_TK), :]
        mbs = []
        for qm, s_ref in zip(qs, s_refs):
            for gs in groups:
                s = _dot_nt(kblk, qm[gs, :])
                s_ref[:, gs] = s
                mbs.append(jnp.max(s, axis=0, keepdims=True))
        return tuple(mbs)

    def consume(i, s_refs, mbs, stats):
        off = pl.multiple_of(i * AT_TK, AT_TK)
        vtblk = vt_ref[0, :, pl.ds(off, AT_TK)]
        out = []
        for mi, (s_ref, acc_ref) in enumerate(zip(s_refs, (acc1_ref, acc2_ref))):
            for gi, gs in enumerate(groups):
                idx = mi * ng + gi
                m, l = stats[2 * idx], stats[2 * idx + 1]
                m_new = jnp.maximum(m, mbs[idx])
                alpha = jnp.exp2(m - m_new)
                p = jnp.exp2(s_ref[:, gs] - m_new)
                l_new = alpha * l + jnp.sum(p, axis=0, keepdims=True)
                acc_ref[:, gs] = alpha * acc_ref[:, gs] + _dot(vtblk, p.astype(BF16))
                out += [m_new, l_new]
        return tuple(out)

    slot_a, slot_b = (sa1_ref, sa2_ref), (sb1_ref, sb2_ref)

    def save_maxima(mbs):
        for idx, mb in enumerate(mbs):
            mba_ref[idx // ng:idx // ng + 1, groups[idx % ng]] = mb

    @pl.when(qi == 0)
    def _():
        save_maxima(produce(0, q_cur, slot_a))

    def body(j, carry):
        mb_a, stats = carry[:2 * ng], carry[2 * ng:]
        mb_b = produce(2 * j + 1, q_cur, slot_b)
        stats = consume(2 * j, slot_a, mb_a, stats)
        mb_a = produce(2 * j + 2, q_cur, slot_a)
        stats = consume(2 * j + 1, slot_b, mb_b, stats)
        return (*mb_a, *stats)

    neg = jnp.full((1, AT_CG), -jnp.inf, F32)
    zer = jnp.zeros((1, AT_CG), F32)
    mb_a0 = tuple(mba_ref[idx // ng:idx // ng + 1, groups[idx % ng]] for idx in range(2 * ng))
    carry = lax.fori_loop(0, nblk // 2 - 1, body, (*mb_a0, *((neg, zer) * (2 * ng))))
    mb_a, stats = carry[:2 * ng], carry[2 * ng:]
    mb_b = produce(nblk - 1, q_cur, slot_b)
    stats = consume(nblk - 2, slot_a, mb_a, stats)
    save_maxima(produce(0, split_maps(qn_ref[...]), slot_a))
    stats = consume(nblk - 1, slot_b, mb_b, stats)
    l1 = jnp.concatenate([stats[2 * g + 1] for g in range(ng)], axis=1)
    l2 = jnp.concatenate([stats[2 * (ng + g) + 1] for g in range(ng)], axis=1)

    lam = (jnp.exp(jnp.sum(lq1_ref[...] * lk1_ref[...], axis=1, keepdims=True))
           - jnp.exp(jnp.sum(lq2_ref[...] * lk2_ref[...], axis=1, keepdims=True))
           + LAM_INIT)
    o = acc1_ref[...] * (1.0 / l1) - acc2_ref[...] * (lam / l2)
    ms = jnp.mean(o * o, axis=0, keepdims=True)
    y = o * lax.rsqrt(ms + EPS) * nw_ref[...] * (1.0 - LAM_INIT)
    o_ref[...] = y.T.astype(BF16)


def _attention(qk, kvt, lq1, lk1, lq2, lk2, norm_w, batch, seq):
    v_blk0 = (M_HEADS * M_DQK) // A_DV
    t_rows = qk.shape[0]
    nq = seq // AT_TQ
    small = pl.BlockSpec((1, A_DH), lambda b, h, i: (0, 0))
    return pl.pallas_call(
        _attn_kernel,
        name="attention",
        grid=(batch, A_HEADS, nq),
        in_specs=[
            pl.BlockSpec((AT_TQ, LANES), lambda b, h, i: (b * nq + i, h)),
            pl.BlockSpec((AT_TQ, LANES), lambda b, h, i: (b * nq + jnp.minimum(i + 1, nq - 1), h)),
            pl.BlockSpec((seq, LANES), lambda b, h, i: (b, A_HEADS + h)),
            pl.BlockSpec((1, A_DV, seq), lambda b, h, i: (b, v_blk0 + h, 0)),
            small, small, small, small,
            pl.BlockSpec((A_DV, 1), lambda b, h, i: (0, 0)),
        ],
        out_specs=pl.BlockSpec((AT_TQ, LANES), lambda b, h, i: (b * nq + i, h)),
        out_shape=jax.ShapeDtypeStruct((t_rows, A_WIDTH), BF16),
        scratch_shapes=([pltpu.VMEM((A_DV, AT_TQ), F32)] * 2 + [pltpu.VMEM((AT_TK, AT_TQ), F32)] * 4
                        + [pltpu.VMEM((8, AT_TQ), F32)]),
        compiler_params=_cparams(("arbitrary", "arbitrary", "arbitrary")),
    )(qk, qk, qk, kvt, lq1, lk1, lq2, lk2, norm_w)


MG_TM = 512


def _merge_kernel(hf_ref, hb_ref, mo_ref, ha_ref, gm_ref, ga_ref, nw_ref, wm_ref, wa_ref, out_ref):
    hm = hf_ref[...] + hb_ref[...]
    parts = []
    for h in range(M_HEADS):
        seg = hm[:, h * M_DV:(h + 1) * M_DV]
        ms = jnp.mean(seg * seg, axis=-1, keepdims=True)
        parts.append(seg * lax.rsqrt(ms + EPS))
    hn = jnp.concatenate(parts, axis=1) * nw_ref[...]
    hn = (hn * mo_ref[...].astype(F32)).astype(BF16)
    branch_m = _dot(hn, wm_ref[...])
    branch_a = _dot(ha_ref[...], wa_ref[...])
    mixed = gm_ref[...].astype(F32) * branch_m + ga_ref[...].astype(F32) * branch_a
    out_ref[...] = mixed.astype(BF16)


def _merge(hf, hb, sig, ha, norm_w, w_m, w_a):
    t_rows = hf.shape[0]
    row = lambda i: (i, 0)
    const = lambda i: (0, 0)
    return pl.pallas_call(
        _merge_kernel,
        name="merge",
        grid=(t_rows // MG_TM,),
        in_specs=[
            pl.BlockSpec((MG_TM, M_WIDTH), row),
            pl.BlockSpec((MG_TM, M_WIDTH), row),
            pl.BlockSpec((MG_TM, M_WIDTH), lambda i: (i, N_BRANCH_GATES // M_WIDTH)),
            pl.BlockSpec((MG_TM, A_WIDTH), row),
            pl.BlockSpec((MG_TM, D_MODEL), lambda i: (i, 0)),
            pl.BlockSpec((MG_TM, D_MODEL), lambda i: (i, 1)),
            pl.BlockSpec((1, M_WIDTH), const),
            pl.BlockSpec((M_WIDTH, D_MODEL), const),
            pl.BlockSpec((A_WIDTH, D_MODEL), const),
        ],
        out_specs=pl.BlockSpec((MG_TM, D_MODEL), row),
        out_shape=jax.ShapeDtypeStruct((t_rows, D_MODEL), BF16),
        compiler_params=_cparams(("parallel",)),
    )(hf, hb, sig, ha, sig, sig, norm_w, w_m, w_a)


OP_TM = 512


def _outproj_kernel(mixed_ref, x_ref, w_ref, nw_ref, x1_ref, h2_ref):
    x1 = x_ref[...] + _dot(mixed_ref[...], w_ref[...])
    x1_ref[...] = x1
    ms = jnp.mean(x1 * x1, axis=-1, keepdims=True)
    h2_ref[...] = (x1 * lax.rsqrt(ms + EPS) * nw_ref[...]).astype(BF16)


def _outproj(mixed, x2, w_out, norm_w):
    t_rows = x2.shape[0]
    row = lambda i: (i, 0)
    const = lambda i: (0, 0)
    return pl.pallas_call(
        _outproj_kernel,
        name="outproj",
        grid=(t_rows // OP_TM,),
        in_specs=[
            pl.BlockSpec((OP_TM, D_MODEL), row),
            pl.BlockSpec((OP_TM, D_MODEL), row),
            pl.BlockSpec((D_MODEL, D_MODEL), const),
            pl.BlockSpec((1, D_MODEL), const),
        ],
        out_specs=[pl.BlockSpec((OP_TM, D_MODEL), row), pl.BlockSpec((OP_TM, D_MODEL), row)],
        out_shape=[jax.ShapeDtypeStruct((t_rows, D_MODEL), F32),
                   jax.ShapeDtypeStruct((t_rows, D_MODEL), BF16)],
        compiler_params=_cparams(("parallel",)),
    )(mixed, x2, w_out, norm_w)


FI_TM = 1024
FI_TN = 512


def _ffn_in_kernel(h_ref, wg_ref, wu_ref, out_ref):
    h = h_ref[...]
    gate = _dot(h, wg_ref[...])
    up = _dot(h, wu_ref[...])
    out_ref[...] = (gate * _sigmoid(gate) * up).astype(BF16)


def _ffn_in(h2, w_ffn_in):
    t_rows = h2.shape[0]
    nj = D_FF // FI_TN
    return pl.pallas_call(
        _ffn_in_kernel,
        name="ffn_in",
        grid=(t_rows // FI_TM, nj),
        in_specs=[
            pl.BlockSpec((FI_TM, D_MODEL), lambda i, j: (i, 0)),
            pl.BlockSpec((D_MODEL, FI_TN), lambda i, j: (0, j)),
            pl.BlockSpec((D_MODEL, FI_TN), lambda i, j: (0, nj + j)),
        ],
        out_specs=pl.BlockSpec((FI_TM, FI_TN), lambda i, j: (i, j)),
        out_shape=jax.ShapeDtypeStruct((t_rows, D_FF), BF16),
        compiler_params=_cparams(("parallel", "arbitrary")),
    )(h2, w_ffn_in, w_ffn_in)


FO_TM = 1024
FO_TN = 1024
FO_KSPLIT = 2
FO_TK = D_FF // FO_KSPLIT


def _ffn_out_kernel(act_ref, w_ref, x1_ref, nw_ref, out_ref):
    k = pl.program_id(1)
    j = pl.program_id(2)
    cols = pl.ds(pl.multiple_of(j * FO_TN, FO_TN), FO_TN)
    for r in range(FO_TM // ROW_CHUNK):
        rows = slice(r * ROW_CHUNK, (r + 1) * ROW_CHUNK)
        base = jnp.where(k == 0, x1_ref[rows, :], out_ref[rows, cols])
        out_ref[rows, cols] = base + _dot(act_ref[rows, :], w_ref[...])

    @pl.when((k == FO_KSPLIT - 1) & (j == pl.num_programs(2) - 1))
    def _():
        x2 = out_ref[...]
        ms = jnp.mean(x2 * x2, axis=-1, keepdims=True)
        out_ref[...] = x2 * lax.rsqrt(ms + EPS) * nw_ref[...]


def _ffn_out(act, w_ffn_out, x1, norm_w):
    t_rows = x1.shape[0]
    nj = D_MODEL // FO_TN
    return pl.pallas_call(
        _ffn_out_kernel,
        name="ffn_out",
        grid=(t_rows // FO_TM, FO_KSPLIT, nj),
        in_specs=[
            pl.BlockSpec((FO_TM, FO_TK), lambda i, k, j: (i, k)),
            pl.BlockSpec((FO_TK, FO_TN), lambda i, k, j: (k, j)),
            pl.BlockSpec((FO_TM, FO_TN), lambda i, k, j: (i, jnp.where(k == 0, j, nj - 1))),
            pl.BlockSpec((1, D_MODEL), lambda i, k, j: (0, 0)),
        ],
        out_specs=pl.BlockSpec((FO_TM, D_MODEL), lambda i, k, j: (i, 0)),
        out_shape=jax.ShapeDtypeStruct((t_rows, D_MODEL), F32),
        compiler_params=_cparams(("parallel", "arbitrary", "arbitrary")),
    )(act, w_ffn_out, x1, norm_w)


def _rope_tables(seq):
    inv = ROPE_THETA ** (-jnp.arange(0, A_DH, 2, dtype=F32) / A_DH)
    ang = jnp.arange(seq, dtype=F32)[:, None] * inv[None, :]
    cos = jnp.cos(ang)
    sin = jnp.sin(ang)
    cos_t = jnp.concatenate([cos, cos, cos, cos], axis=1)
    sin_t = jnp.concatenate([-sin, -sin, sin, sin], axis=1)
    return cos_t, sin_t


def _rotary_layout(w_seg):
    d = w_seg.shape[0]
    half = A_DH // 2
    return w_seg.reshape(d, A_HEADS, 2, 2, half).transpose(0, 1, 3, 2, 4).reshape(d, A_WIDTH)


def kernel(x, norm1_w, w_in, b_igate, b_fgate, b_branch_gate, mlstm_norm_w, lam_q1, lam_k1, lam_q2, lam_k2, attn_norm_w, w_branch_m, w_branch_a, w_out, norm2_w, w_ffn_in, w_ffn_out, final_norm_w):
    batch, seq, d = x.shape
    depth = w_in.shape[0]
    assert d == D_MODEL and depth == 1 and seq % PJ_TM == 0 and seq % AT_TQ == 0
    t_rows = batch * seq
    x2 = x.reshape(t_rows, d)
    cos_t, sin_t = _rope_tables(seq)

    l = 0
    w = w_in[l].astype(BF16)
    qk_w = M_HEADS * M_DQK
    w_vq = jnp.concatenate([w[:, 2 * qk_w:OFF_MO], w[:, OFF_MQ:qk_w]], axis=1)
    w_kvt = jnp.concatenate([w[:, qk_w:2 * qk_w], w[:, OFF_AV:OFF_GT]], axis=1)
    w_sig = jnp.concatenate([w[:, OFF_GT:OFF_GT + N_BRANCH_GATES], w[:, OFF_MO:OFF_MG]], axis=1)
    w_rot = jnp.concatenate([_rotary_layout(w[:, OFF_AQ:OFF_AK]), _rotary_layout(w[:, OFF_AK:OFF_AV])], axis=1)
    w_gate = jnp.pad(w[:, OFF_MG:OFF_AQ], ((0, 0), (0, LANES - N_GATE)))

    scale_vq = jnp.ones((1, M_WIDTH + qk_w), F32)
    scale_kvt = jnp.concatenate([jnp.full((1, qk_w), M_DQK ** -0.5, F32), jnp.ones((1, A_WIDTH), F32)], axis=1)
    scale_rot = jnp.concatenate([jnp.full((1, A_WIDTH), Q_SCALE, F32), jnp.ones((1, A_WIDTH), F32)], axis=1)
    bias_sig = jnp.concatenate([b_branch_gate[l].astype(F32), jnp.zeros((M_WIDTH,), F32)]).reshape(1, -1)
    gate_bias = jnp.stack([b_igate[l], b_fgate[l]], axis=1).reshape(N_GATE).astype(F32)
    gate_bias = jnp.pad(gate_bias, (0, LANES - N_GATE))

    hn, gates = _rmsnorm(x2, norm1_w[l].reshape(1, d), w_gate)
    vq = _proj("scale", hn, w_vq, [scale_vq], seq, tn=PJ_TN_NARROW)
    sig = _proj("sigmoid", hn, w_sig, [bias_sig], seq)
    qk = _proj("rope", hn, w_rot, [cos_t, sin_t, scale_rot], seq)
    kvt = _proj("transpose", hn, w_kvt, [scale_kvt], seq, tn=PJ_TN_NARROW)

    hf, hb = _mlstm(vq, kvt, gates, gate_bias.reshape(1, LANES), gate_bias.reshape(LANES, 1), batch, seq)
    ha = _attention(qk, kvt, lam_q1[l].reshape(1, A_DH), lam_k1[l].reshape(1, A_DH),
                    lam_q2[l].reshape(1, A_DH), lam_k2[l].reshape(1, A_DH),
                    attn_norm_w[l].reshape(A_DV, 1), batch, seq)
    mixed = _merge(hf, hb, sig, ha, mlstm_norm_w[l].reshape(1, M_WIDTH),
                   w_branch_m[l].astype(BF16), w_branch_a[l].astype(BF16))
    x1, h2 = _outproj(mixed, x2, w_out[l].astype(BF16), norm2_w[l].reshape(1, d))
    act = _ffn_in(h2, w_ffn_in[l].astype(BF16))
    out = _ffn_out(act, w_ffn_out[l].astype(BF16), x1, final_norm_w.reshape(1, d))
    return out.reshape(batch, seq, d)
```

```python
import functools
import math

import jax
import jax.numpy as jnp
from jax import lax
from jax.experimental import pallas as pl
from jax.experimental.pallas import tpu as pltpu

F32 = jnp.float32
BF16 = jnp.bfloat16

D_MODEL = 2048
M_HEADS = 4
M_DQK = 128
M_DV = 256
M_CHUNK = 128
GATE_CAP = 15.0
A_HEADS = 8
A_DH = 64
A_DV = 2 * A_DH
ROPE_THETA = 10000.0
D_FF = 5632
EPS = 1e-6
M_WIDTH = M_HEADS * M_DV
A_WIDTH = A_HEADS * A_DV
N_BRANCH_GATES = 2 * D_MODEL
LAM_INIT = 0.8 - 0.6 * math.exp(-0.3 * 0)

OFF_MQ = 0
OFF_MO = 2 * M_HEADS * M_DQK + M_WIDTH
OFF_MG = OFF_MO + M_WIDTH
N_GATE = 4 * M_HEADS
OFF_AQ = OFF_MG + N_GATE
OFF_AK = OFF_AQ + A_WIDTH
OFF_AV = OFF_AK + A_WIDTH
OFF_GT = OFF_AV + A_WIDTH
LANES = 128

VMEM_LIMIT = 56 * 1024 * 1024


def _cparams(sem):
    return pltpu.CompilerParams(dimension_semantics=sem, vmem_limit_bytes=VMEM_LIMIT)


def _dot(a, b):
    return jnp.dot(a, b, preferred_element_type=F32)


def _dot_nt(a, b):
    return lax.dot_general(a, b, (((1,), (1,)), ((), ())), preferred_element_type=F32)


def _dot_tn(a, b):
    return lax.dot_general(a, b, (((0,), (0,)), ((), ())), preferred_element_type=F32)


def _sigmoid(x):
    return 0.5 * jnp.tanh(0.5 * x) + 0.5


NORM_TM = 512
PJ_TM = 2048
PJ_TN = 1024
PJ_TN_NARROW = 768
ROW_CHUNK = 256
Q_SCALE = (A_DH ** -0.5) * math.log2(math.e)


def _rmsnorm_kernel(x_ref, w_ref, wg_ref, o_ref, g_ref):
    x = x_ref[...]
    ms = jnp.mean(x * x, axis=-1, keepdims=True)
    hn = (x * lax.rsqrt(ms + EPS) * w_ref[...]).astype(BF16)
    o_ref[...] = hn
    g_ref[...] = _dot(hn, wg_ref[...])


def _rmsnorm(x2, norm_w, w_gate):
    t_rows, d = x2.shape
    return pl.pallas_call(
        _rmsnorm_kernel,
        name="rmsnorm",
        grid=(t_rows // NORM_TM,),
        in_specs=[pl.BlockSpec((NORM_TM, d), lambda i: (i, 0)),
                  pl.BlockSpec((1, d), lambda i: (0, 0)),
                  pl.BlockSpec((d, LANES), lambda i: (0, 0))],
        out_specs=[pl.BlockSpec((NORM_TM, d), lambda i: (i, 0)),
                   pl.BlockSpec((NORM_TM, LANES), lambda i: (i, 0))],
        out_shape=[jax.ShapeDtypeStruct((t_rows, d), BF16),
                   jax.ShapeDtypeStruct((t_rows, LANES), F32)],
        compiler_params=_cparams(("parallel",)),
    )(x2, norm_w, w_gate)


def _rope(acc, cos, sin_signed):
    outs = []
    for c in range(acc.shape[1] // LANES):
        t = acc[:, c * LANES:(c + 1) * LANES]
        outs.append(t * cos + pltpu.roll(t, LANES // 2, axis=1) * sin_signed)
    return jnp.concatenate(outs, axis=1)


def _proj_kernel(mode, h_ref, w_ref, *refs):
    o_ref = refs[-1]
    for r in range(PJ_TM // ROW_CHUNK):
        rows = slice(r * ROW_CHUNK, (r + 1) * ROW_CHUNK)
        acc = _dot(h_ref[rows, :], w_ref[...])
        if mode == "scale":
            o_ref[rows, :] = (acc * refs[0][...]).astype(BF16)
        elif mode == "sigmoid":
            o_ref[rows, :] = _sigmoid(acc + refs[0][...]).astype(BF16)
        elif mode == "rope":
            cos_ref, sin_ref, cs_ref = refs[:3]
            o_ref[rows, :] = (_rope(acc, cos_ref[rows, :], sin_ref[rows, :]) * cs_ref[...]).astype(BF16)
        else:
            assert mode == "transpose"
            o_ref[0, :, rows] = (acc * refs[0][...]).T.astype(BF16)


def _proj(mode, hn, w, aux, seq, tn=PJ_TN):
    t_rows, d = hn.shape
    n = w.shape[1]
    tn = min(tn, n)
    s_blocks = seq // PJ_TM
    col = pl.BlockSpec((1, tn), lambda i, j: (0, j))
    pos = pl.BlockSpec((PJ_TM, LANES), lambda i, j: (i % s_blocks, 0))
    aux_specs = {"scale": [col], "sigmoid": [col], "rope": [pos, pos, col], "transpose": [col]}[mode]
    if mode == "transpose":
        out_spec = pl.BlockSpec((1, tn, PJ_TM), lambda i, j: (i // s_blocks, j, i % s_blocks))
        out_shape = jax.ShapeDtypeStruct((t_rows // seq, n, seq), BF16)
    else:
        out_spec = pl.BlockSpec((PJ_TM, tn), lambda i, j: (i, j))
        out_shape = jax.ShapeDtypeStruct((t_rows, n), BF16)
    return pl.pallas_call(
        functools.partial(_proj_kernel, mode),
        name="proj_" + mode,
        grid=(t_rows // PJ_TM, n // tn),
        in_specs=[pl.BlockSpec((PJ_TM, d), lambda i, j: (i, 0)),
                  pl.BlockSpec((d, tn), lambda i, j: (0, j))] + aux_specs,
        out_specs=out_spec,
        out_shape=out_shape,
        compiler_params=_cparams(("parallel", "arbitrary")),
    )(hn, w, *aux)


L = M_CHUNK
MS_SUB = 4
DV_EXT = M_DV + LANES


def _softcap(t):
    return GATE_CAP * jnp.tanh(t / GATE_CAP)


def _log_sigmoid(t):
    return jnp.minimum(t, 0.0) - jnp.log(1.0 + jnp.exp(-jnp.abs(t)))


def _gate_act(pre, is_forget):
    c = _softcap(pre)
    return jnp.where(is_forget, _log_sigmoid(c), c)


def _split_dot(a, b, a_is_exact):
    if a_is_exact:
        hi = b.astype(BF16)
        lo = (b - hi.astype(F32)).astype(BF16)
        ab = a.astype(BF16)
        return _dot(ab, hi) + _dot(ab, lo)
    hi = a.astype(BF16)
    lo = (a - hi.astype(F32)).astype(BF16)
    bb = b.astype(BF16)
    return _dot(hi, bb) + _dot(lo, bb)


def _mlstm_kernel(qf_ref, kf_ref, vf_ref, gf_ref, qb_ref, kb_ref, vb_ref, gb_ref,
                  brow_ref, bcol_ref, hf_ref, hb_ref, c_ref, m_ref):
    step = pl.program_id(1)

    @pl.when(step == 0)
    def _():
        c_ref[...] = jnp.zeros_like(c_ref)
        m_ref[...] = jnp.zeros_like(m_ref)

    row = lax.broadcasted_iota(jnp.int32, (L, L), 0)
    col = lax.broadcasted_iota(jnp.int32, (L, L), 1)
    lane_id = lax.broadcasted_iota(jnp.int32, (1, LANES), 1)
    sub_id = lax.broadcasted_iota(jnp.int32, (LANES, 1), 0)
    forget_lane = (lane_id % 8) >= 4
    forget_sub = (sub_id % 8) >= 4
    ones_ext = jnp.ones((L, LANES), BF16)

    dirs = ((qf_ref, kf_ref, vf_ref, gf_ref, hf_ref), (qb_ref, kb_ref, vb_ref, gb_ref, hb_ref))
    for sub, d in [(sub, d) for sub in range(MS_SUB) for d in range(2)]:
        q_blk, k_blk, v_blk, g_blk, h_blk = dirs[d]
        r0 = (sub if d == 0 else MS_SUB - 1 - sub) * L
        q_ref, v_ref, g_ref, h_ref = (ref.at[r0:r0 + L, :] for ref in (q_blk, v_blk, g_blk, h_blk))
        kt_ref = k_blk.at[0, :, r0:r0 + L]
        visible = (row >= col) if d == 0 else (col >= row)
        vis_f = visible.astype(F32)

        g = g_ref[...]
        g_t = g.T
        act_c = _gate_act(g + brow_ref[...], forget_lane)
        act_r = _gate_act(g_t + bcol_ref[...], forget_sub)
        cum_c = _split_dot(vis_f, act_c, True)
        cum_r = _split_dot(act_r, vis_f.T, False)

        for h in range(M_HEADS):
            idx = d * M_HEADS + h
            ci = d * 8 + h
            cf = d * 8 + 4 + h
            bc = cum_c[:, cf:cf + 1]
            br = cum_r[cf:cf + 1, :]
            igr = act_r[ci:ci + 1, :]
            b_last = br[:, L - 1:L] if d == 0 else br[:, 0:1]
            m_old = m_ref[idx][0:1, 0:1]

            q = q_ref[:, h * M_DQK:(h + 1) * M_DQK]
            kt = kt_ref[h * M_DQK:(h + 1) * M_DQK, :]
            v_ext = jnp.concatenate([v_ref[:, h * M_DV:(h + 1) * M_DV], ones_ext], axis=1)

            dmat = jnp.where(visible, bc - br + igr, -jnp.inf)
            m_loc = jnp.max(dmat, axis=1, keepdims=True)
            s = _dot(q, kt) * jnp.exp(dmat - m_loc)
            sv = _dot(s.astype(BF16), v_ext)
            g_row = b_last - br + igr
            mg = jnp.max(g_row, axis=1, keepdims=True)
            kw_t = (kt.astype(F32) * jnp.exp(g_row - mg)).astype(BF16)
            u = _dot(kw_t, v_ext)

            c_old = c_ref[idx]
            inter = bc + m_old
            m_t = jnp.maximum(inter, m_loc)
            comb = jnp.exp(inter - m_t) * _dot(q, c_old.astype(BF16)) + jnp.exp(m_loc - m_t) * sv
            num = comb[:, :M_DV]
            den = comb[:, M_DV:M_DV + 1]
            hval = num / jnp.maximum(jnp.abs(den), jnp.exp(-m_t))
            h_ref[:, h * M_DV:(h + 1) * M_DV] = hval

            m_new = jnp.maximum(b_last + m_old, mg)
            c_ref[idx] = jnp.exp(b_last + m_old - m_new) * c_old + jnp.exp(mg - m_new) * u
            m_ref[idx] = jnp.broadcast_to(m_new, (8, LANES))


def _mlstm(vq, kvt, gates, bias_row, bias_col, batch, seq):
    t_rows = vq.shape[0]
    rows = MS_SUB * L
    nc = seq // rows
    fwd = lambda b, c: b * nc + c
    bwd = lambda b, c: b * nc + (nc - 1 - c)
    qk_w = M_HEADS * M_DQK
    in_specs = []
    for ch in (fwd, bwd):
        in_specs += [
            pl.BlockSpec((rows, qk_w), lambda b, c, ch=ch: (ch(b, c), M_WIDTH // qk_w)),
            pl.BlockSpec((1, qk_w, rows), lambda b, c, ch=ch: (b, 0, ch(0, c))),
            pl.BlockSpec((rows, M_WIDTH), lambda b, c, ch=ch: (ch(b, c), 0)),
            pl.BlockSpec((rows, LANES), lambda b, c, ch=ch: (ch(b, c), 0)),
        ]
    in_specs += [pl.BlockSpec((1, LANES), lambda b, c: (0, 0)),
                 pl.BlockSpec((LANES, 1), lambda b, c: (0, 0))]
    return pl.pallas_call(
        _mlstm_kernel,
        name="mlstm",
        grid=(batch, nc),
        in_specs=in_specs,
        out_specs=[pl.BlockSpec((rows, M_WIDTH), lambda b, c: (fwd(b, c), 0)),
                   pl.BlockSpec((rows, M_WIDTH), lambda b, c: (bwd(b, c), 0))],
        out_shape=[jax.ShapeDtypeStruct((t_rows, M_WIDTH), F32)] * 2,
        scratch_shapes=[pltpu.VMEM((2 * M_HEADS, M_DQK, DV_EXT), F32),
                        pltpu.VMEM((2 * M_HEADS, 8, LANES), F32)],
        compiler_params=_cparams(("parallel", "arbitrary")),
    )(vq, kvt, vq, gates, vq, kvt, vq, gates, bias_row, bias_col)


AT_TQ = 1024
AT_TK = 1024
AT_CG = 256


def _attn_kernel(q_ref, qn_ref, k_ref, vt_ref, lq1_ref, lk1_ref, lq2_ref, lk2_ref, nw_ref,
                 o_ref, acc1_ref, acc2_ref, sa1_ref, sa2_ref, sb1_ref, sb2_ref, mba_ref):
    seq = k_ref.shape[0]
    nblk = seq // AT_TK
    qi = pl.program_id(2)
    lane = lax.broadcasted_iota(jnp.int32, (1, LANES), 1)
    in_map1 = (lane % A_DH) < (A_DH // 2)

    def split_maps(q):
        zero = jnp.zeros_like(q)
        return jnp.where(in_map1, q, zero), jnp.where(in_map1, zero, q)

    q_cur = split_maps(q_ref[...])
    acc1_ref[...] = jnp.zeros_like(acc1_ref)
    acc2_ref[...] = jnp.zeros_like(acc2_ref)

    groups = [slice(g * AT_CG, (g + 1) * AT_CG) for g in range(AT_TQ // AT_CG)]
    ng = len(groups)

    def produce(i, qs, s_refs):
        off = pl.multiple_of(i * AT_TK, AT_TK)
        kblk = k_ref[pl.ds(off, AT_TK), :]
        mbs = []
        for qm, s_ref in zip(qs, s_refs):
            for gs in groups:
                s = _dot_nt(kblk, qm[gs, :])
                s_ref[:, gs] = s
                mbs.append(jnp.max(s, axis=0, keepdims=True))
        return tuple(mbs)

    def consume(i, s_refs, mbs, stats):
        off = pl.multiple_of(i * AT_TK, AT_TK)
        vtblk = vt_ref[0, :, pl.ds(off, AT_TK)]
        out = []
        for mi, (s_ref, acc_ref) in enumerate(zip(s_refs, (acc1_ref, acc2_ref))):
            for gi, gs in enumerate(groups):
                idx = mi * ng + gi
                m, l = stats[2 * idx], stats[2 * idx + 1]
                m_new = jnp.maximum(m, mbs[idx])
                alpha = jnp.exp2(m - m_new)
                p = jnp.exp2(s_ref[:, gs] - m_new)
                l_new = alpha * l + jnp.sum(p, axis=0, keepdims=True)
                acc_ref[:, gs] = alpha * acc_ref[:, gs] + _dot(vtblk, p.astype(BF16))
                out += [m_new, l_new]
        return tuple(out)

    slot_a, slot_b = (sa1_ref, sa2_ref), (sb1_ref, sb2_ref)

    def save_maxima(mbs):
        for idx, mb in enumerate(mbs):
            mba_ref[idx // ng:idx // ng + 1, groups[idx % ng]] = mb

    @pl.when(qi == 0)
    def _():
        save_maxima(produce(0, q_cur, slot_a))

    def body(j, carry):
        mb_a, stats = carry[:2 * ng], carry[2 * ng:]
        mb_b = produce(2 * j + 1, q_cur, slot_b)
        stats = consume(2 * j, slot_a, mb_a, stats)
        mb_a = produce(2 * j + 2, q_cur, slot_a)
        stats = consume(2 * j + 1, slot_b, mb_b, stats)
        return (*mb_a, *stats)

    neg = jnp.full((1, AT_CG), -jnp.inf, F32)
    zer = jnp.zeros((1, AT_CG), F32)
    mb_a0 = tuple(mba_ref[idx // ng:idx // ng + 1, groups[idx % ng]] for idx in range(2 * ng))
    carry = lax.fori_loop(0, nblk // 2 - 1, body, (*mb_a0, *((neg, zer) * (2 * ng))))
    mb_a, stats = carry[:2 * ng], carry[2 * ng:]
    mb_b = produce(nblk - 1, q_cur, slot_b)
    stats = consume(nblk - 2, slot_a, mb_a, stats)
    save_maxima(produce(0, split_maps(qn_ref[...]), slot_a))
    stats = consume(nblk - 1, slot_b, mb_b, stats)
    l1 = jnp.concatenate([stats[2 * g + 1] for g in range(ng)], axis=1)
    l2 = jnp.concatenate([stats[2 * (ng + g) + 1] for g in range(ng)], axis=1)

    lam = (jnp.exp(jnp.sum(lq1_ref[...] * lk1_ref[...], axis=1, keepdims=True))
           - jnp.exp(jnp.sum(lq2_ref[...] * lk2_ref[...], axis=1, keepdims=True))
           + LAM_INIT)
    o = acc1_ref[...] * (1.0 / l1) - acc2_ref[...] * (lam / l2)
    ms = jnp.mean(o * o, axis=0, keepdims=True)
    y = o * lax.rsqrt(ms + EPS) * nw_ref[...] * (1.0 - LAM_INIT)
    o_ref[...] = y.T.astype(BF16)


def _attention(qk, kvt, lq1, lk1, lq2, lk2, norm_w, batch, seq):
    v_blk0 = (M_HEADS * M_DQK) // A_DV
    t_rows = qk.shape[0]
    nq = seq // AT_TQ
    small = pl.BlockSpec((1, A_DH), lambda b, h, i: (0, 0))
    return pl.pallas_call(
        _attn_kernel,
        name="attention",
        grid=(batch, A_HEADS, nq),
        in_specs=[
            pl.BlockSpec((AT_TQ, LANES), lambda b, h, i: (b * nq + i, h)),
            pl.BlockSpec((AT_TQ, LANES), lambda b, h, i: (b * nq + jnp.minimum(i + 1, nq - 1), h)),
            pl.BlockSpec((seq, LANES), lambda b, h, i: (b, A_HEADS + h)),
            pl.BlockSpec((1, A_DV, seq), lambda b, h, i: (b, v_blk0 + h, 0)),
            small, small, small, small,
            pl.BlockSpec((A_DV, 1), lambda b, h, i: (0, 0)),
        ],
        out_specs=pl.BlockSpec((AT_TQ, LANES), lambda b, h, i: (b * nq + i, h)),
        out_shape=jax.ShapeDtypeStruct((t_rows, A_WIDTH), BF16),
        scratch_shapes=([pltpu.VMEM((A_DV, AT_TQ), F32)] * 2 + [pltpu.VMEM((AT_TK, AT_TQ), F32)] * 4
                        + [pltpu.VMEM((8, AT_TQ), F32)]),
        compiler_params=_cparams(("arbitrary", "arbitrary", "arbitrary")),
    )(qk, qk, qk, kvt, lq1, lk1, lq2, lk2, norm_w)


MG_TM = 512


def _merge_kernel(hf_ref, hb_ref, mo_ref, ha_ref, gm_ref, ga_ref, nw_ref, wm_ref, wa_ref, out_ref):
    hm = hf_ref[...] + hb_ref[...]
    parts = []
    for h in range(M_HEADS):
        seg = hm[:, h * M_DV:(h + 1) * M_DV]
        ms = jnp.mean(seg * seg, axis=-1, keepdims=True)
        parts.append(seg * lax.rsqrt(ms + EPS))
    hn = jnp.concatenate(parts, axis=1) * nw_ref[...]
    hn = (hn * mo_ref[...].astype(F32)).astype(BF16)
    branch_m = _dot(hn, wm_ref[...])
    branch_a = _dot(ha_ref[...], wa_ref[...])
    mixed = gm_ref[...].astype(F32) * branch_m + ga_ref[...].astype(F32) * branch_a
    out_ref[...] = mixed.astype(BF16)


def _merge(hf, hb, sig, ha, norm_w, w_m, w_a):
    t_rows = hf.shape[0]
    row = lambda i: (i, 0)
    const = lambda i: (0, 0)
    return pl.pallas_call(
        _merge_kernel,
        name="merge",
        grid=(t_rows // MG_TM,),
        in_specs=[
            pl.BlockSpec((MG_TM, M_WIDTH), row),
            pl.BlockSpec((MG_TM, M_WIDTH), row),
            pl.BlockSpec((MG_TM, M_WIDTH), lambda i: (i, N_BRANCH_GATES // M_WIDTH)),
            pl.BlockSpec((MG_TM, A_WIDTH), row),
            pl.BlockSpec((MG_TM, D_MODEL), lambda i: (i, 0)),
            pl.BlockSpec((MG_TM, D_MODEL), lambda i: (i, 1)),
            pl.BlockSpec((1, M_WIDTH), const),
            pl.BlockSpec((M_WIDTH, D_MODEL), const),
            pl.BlockSpec((A_WIDTH, D_MODEL), const),
        ],
        out_specs=pl.BlockSpec((MG_TM, D_MODEL), row),
        out_shape=jax.ShapeDtypeStruct((t_rows, D_MODEL), BF16),
        compiler_params=_cparams(("parallel",)),
    )(hf, hb, sig, ha, sig, sig, norm_w, w_m, w_a)


OP_TM = 512


def _outproj_kernel(mixed_ref, x_ref, w_ref, nw_ref, x1_ref, h2_ref):
    x1 = x_ref[...] + _dot(mixed_ref[...], w_ref[...])
    x1_ref[...] = x1
    ms = jnp.mean(x1 * x1, axis=-1, keepdims=True)
    h2_ref[...] = (x1 * lax.rsqrt(ms + EPS) * nw_ref[...]).astype(BF16)


def _outproj(mixed, x2, w_out, norm_w):
    t_rows = x2.shape[0]
    row = lambda i: (i, 0)
    const = lambda i: (0, 0)
    return pl.pallas_call(
        _outproj_kernel,
        name="outproj",
        grid=(t_rows // OP_TM,),
        in_specs=[
            pl.BlockSpec((OP_TM, D_MODEL), row),
            pl.BlockSpec((OP_TM, D_MODEL), row),
            pl.BlockSpec((D_MODEL, D_MODEL), const),
            pl.BlockSpec((1, D_MODEL), const),
        ],
        out_specs=[pl.BlockSpec((OP_TM, D_MODEL), row), pl.BlockSpec((OP_TM, D_MODEL), row)],
        out_shape=[jax.ShapeDtypeStruct((t_rows, D_MODEL), F32),
                   jax.ShapeDtypeStruct((t_rows, D_MODEL), BF16)],
        compiler_params=_cparams(("parallel",)),
    )(mixed, x2, w_out, norm_w)


FI_TM = 2048
FI_CHUNK = 512
FI_TN = 512


def _ffn_in_kernel(h_ref, wg_ref, wu_ref, out_ref):
    for r in range(FI_TM // FI_CHUNK):
        rows = slice(r * FI_CHUNK, (r + 1) * FI_CHUNK)
        h = h_ref[rows, :]
        gate = _dot(h, wg_ref[...])
        up = _dot(h, wu_ref[...])
        out_ref[rows, :] = (gate * _sigmoid(gate) * up).astype(BF16)


def _ffn_in(h2, w_ffn_in):
    t_rows = h2.shape[0]
    nj = D_FF // FI_TN
    return pl.pallas_call(
        _ffn_in_kernel,
        name="ffn_in",
        grid=(t_rows // FI_TM, nj),
        in_specs=[
            pl.BlockSpec((FI_TM, D_MODEL), lambda i, j: (i, 0)),
            pl.BlockSpec((D_MODEL, FI_TN), lambda i, j: (0, j)),
            pl.BlockSpec((D_MODEL, FI_TN), lambda i, j: (0, nj + j)),
        ],
        out_specs=pl.BlockSpec((FI_TM, FI_TN), lambda i, j: (i, j)),
        out_shape=jax.ShapeDtypeStruct((t_rows, D_FF), BF16),
        compiler_params=_cparams(("parallel", "arbitrary")),
    )(h2, w_ffn_in, w_ffn_in)


FO_TM = 1024
FO_TN = 1024
FO_KSPLIT = 2
FO_TK = D_FF // FO_KSPLIT


def _ffn_out_kernel(act_ref, w_ref, x1_ref, nw_ref, out_ref):
    k = pl.program_id(1)
    j = pl.program_id(2)
    cols = pl.ds(pl.multiple_of(j * FO_TN, FO_TN), FO_TN)
    for r in range(FO_TM // ROW_CHUNK):
        rows = slice(r * ROW_CHUNK, (r + 1) * ROW_CHUNK)
        base = jnp.where(k == 0, x1_ref[rows, :], out_ref[rows, cols])
        out_ref[rows, cols] = base + _dot(act_ref[rows, :], w_ref[...])

    @pl.when((k == FO_KSPLIT - 1) & (j == pl.num_programs(2) - 1))
    def _():
        x2 = out_ref[...]
        ms = jnp.mean(x2 * x2, axis=-1, keepdims=True)
        out_ref[...] = x2 * lax.rsqrt(ms + EPS) * nw_ref[...]


def _ffn_out(act, w_ffn_out, x1, norm_w):
    t_rows = x1.shape[0]
    nj = D_MODEL // FO_TN
    return pl.pallas_call(
        _ffn_out_kernel,
        name="ffn_out",
        grid=(t_rows // FO_TM, FO_KSPLIT, nj),
        in_specs=[
            pl.BlockSpec((FO_TM, FO_TK), lambda i, k, j: (i, k)),
            pl.BlockSpec((FO_TK, FO_TN), lambda i, k, j: (k, j)),
            pl.BlockSpec((FO_TM, FO_TN), lambda i, k, j: (i, jnp.where(k == 0, j, nj - 1))),
            pl.BlockSpec((1, D_MODEL), lambda i, k, j: (0, 0)),
        ],
        out_specs=pl.BlockSpec((FO_TM, D_MODEL), lambda i, k, j: (i, 0)),
        out_shape=jax.ShapeDtypeStruct((t_rows, D_MODEL), F32),
        compiler_params=_cparams(("parallel", "arbitrary", "arbitrary")),
    )(act, w_ffn_out, x1, norm_w)


def _rope_tables(seq):
    inv = ROPE_THETA ** (-jnp.arange(0, A_DH, 2, dtype=F32) / A_DH)
    ang = jnp.arange(seq, dtype=F32)[:, None] * inv[None, :]
    cos = jnp.cos(ang)
    sin = jnp.sin(ang)
    cos_t = jnp.concatenate([cos, cos, cos, cos], axis=1)
    sin_t = jnp.concatenate([-sin, -sin, sin, sin], axis=1)
    return cos_t, sin_t


def _rotary_layout(w_seg):
    d = w_seg.shape[0]
    half = A_DH // 2
    return w_seg.reshape(d, A_HEADS, 2, 2, half).transpose(0, 1, 3, 2, 4).reshape(d, A_WIDTH)


def kernel(x, norm1_w, w_in, b_igate, b_fgate, b_branch_gate, mlstm_norm_w, lam_q1, lam_k1, lam_q2, lam_k2, attn_norm_w, w_branch_m, w_branch_a, w_out, norm2_w, w_ffn_in, w_ffn_out, final_norm_w):
    batch, seq, d = x.shape
    depth = w_in.shape[0]
    assert d == D_MODEL and depth == 1 and seq % PJ_TM == 0 and seq % AT_TQ == 0
    t_rows = batch * seq
    x2 = x.reshape(t_rows, d)
    cos_t, sin_t = _rope_tables(seq)

    l = 0
    w = w_in[l].astype(BF16)
    qk_w = M_HEADS * M_DQK
    w_vq = jnp.concatenate([w[:, 2 * qk_w:OFF_MO], w[:, OFF_MQ:qk_w]], axis=1)
    w_kvt = jnp.concatenate([w[:, qk_w:2 * qk_w], w[:, OFF_AV:OFF_GT]], axis=1)
    w_sig = jnp.concatenate([w[:, OFF_GT:OFF_GT + N_BRANCH_GATES], w[:, OFF_MO:OFF_MG]], axis=1)
    w_rot = jnp.concatenate([_rotary_layout(w[:, OFF_AQ:OFF_AK]), _rotary_layout(w[:, OFF_AK:OFF_AV])], axis=1)
    w_gate = jnp.pad(w[:, OFF_MG:OFF_AQ], ((0, 0), (0, LANES - N_GATE)))

    scale_vq = jnp.ones((1, M_WIDTH + qk_w), F32)
    scale_kvt = jnp.concatenate([jnp.full((1, qk_w), M_DQK ** -0.5, F32), jnp.ones((1, A_WIDTH), F32)], axis=1)
    scale_rot = jnp.concatenate([jnp.full((1, A_WIDTH), Q_SCALE, F32), jnp.ones((1, A_WIDTH), F32)], axis=1)
    bias_sig = jnp.concatenate([b_branch_gate[l].astype(F32), jnp.zeros((M_WIDTH,), F32)]).reshape(1, -1)
    gate_bias = jnp.stack([b_igate[l], b_fgate[l]], axis=1).reshape(N_GATE).astype(F32)
    gate_bias = jnp.pad(gate_bias, (0, LANES - N_GATE))

    hn, gates = _rmsnorm(x2, norm1_w[l].reshape(1, d), w_gate)
    vq = _proj("scale", hn, w_vq, [scale_vq], seq, tn=PJ_TN_NARROW)
    sig = _proj("sigmoid", hn, w_sig, [bias_sig], seq)
    qk = _proj("rope", hn, w_rot, [cos_t, sin_t, scale_rot], seq)
    kvt = _proj("transpose", hn, w_kvt, [scale_kvt], seq, tn=PJ_TN_NARROW)

    hf, hb = _mlstm(vq, kvt, gates, gate_bias.reshape(1, LANES), gate_bias.reshape(LANES, 1), batch, seq)
    ha = _attention(qk, kvt, lam_q1[l].reshape(1, A_DH), lam_k1[l].reshape(1, A_DH),
                    lam_q2[l].reshape(1, A_DH), lam_k2[l].reshape(1, A_DH),
                    attn_norm_w[l].reshape(A_DV, 1), batch, seq)
    mixed = _merge(hf, hb, sig, ha, mlstm_norm_w[l].reshape(1, M_WIDTH),
                   w_branch_m[l].astype(BF16), w_branch_a[l].astype(BF16))
    x1, h2 = _outproj(mixed, x2, w_out[l].astype(BF16), norm2_w[l].reshape(1, d))
    act = _ffn_in(h2, w_ffn_in[l].astype(BF16))
    out = _ffn_out(act, w_ffn_out[l].astype(BF16), x1, final_norm_w.reshape(1, d))
    return out.reshape(batch, seq, d)
```

```python
import functools
import math

import jax
import jax.numpy as jnp
from jax import lax
from jax.experimental import pallas as pl
from jax.experimental.pallas import tpu as pltpu

F32 = jnp.float32
BF16 = jnp.bfloat16

D_MODEL = 2048
M_HEADS = 4
M_DQK = 128
M_DV = 256
M_CHUNK = 128
GATE_CAP = 15.0
A_HEADS = 8
A_DH = 64
A_DV = 2 * A_DH
ROPE_THETA = 10000.0
D_FF = 5632
EPS = 1e-6
M_WIDTH = M_HEADS * M_DV
A_WIDTH = A_HEADS * A_DV
N_BRANCH_GATES = 2 * D_MODEL
LAM_INIT = 0.8 - 0.6 * math.exp(-0.3 * 0)

OFF_MQ = 0
OFF_MO = 2 * M_HEADS * M_DQK + M_WIDTH
OFF_MG = OFF_MO + M_WIDTH
N_GATE = 4 * M_HEADS
OFF_AQ = OFF_MG + N_GATE
OFF_AK = OFF_AQ + A_WIDTH
OFF_AV = OFF_AK + A_WIDTH
OFF_GT = OFF_AV + A_WIDTH
LANES = 128
SUBLANES = 8

V7X_VMEM_BYTES = 64 * 1024 * 1024
VMEM_LIMIT = V7X_VMEM_BYTES * 7 // 8


def _cparams(sem):
    return pltpu.CompilerParams(dimension_semantics=sem, vmem_limit_bytes=VMEM_LIMIT)


def _dot(a, b):
    return jnp.dot(a, b, preferred_element_type=F32)


def _dot_nt(a, b):
    return lax.dot_general(a, b, (((1,), (1,)), ((), ())), preferred_element_type=F32)


def _dot_tn(a, b):
    return lax.dot_general(a, b, (((0,), (0,)), ((), ())), preferred_element_type=F32)


def _sigmoid(x):
    return 0.5 * jnp.tanh(0.5 * x) + 0.5


NORM_TM = 512
PJ_TM = 2048
PJ_TN = 1024
PJ_TN_NARROW = 768
ROW_CHUNK = 256
Q_SCALE = (A_DH ** -0.5) * math.log2(math.e)


def _rmsnorm_kernel(x_ref, w_ref, wg_ref, o_ref, g_ref):
    x = x_ref[...]
    ms = jnp.mean(x * x, axis=-1, keepdims=True)
    hn = (x * lax.rsqrt(ms + EPS) * w_ref[...]).astype(BF16)
    o_ref[...] = hn
    g_ref[...] = _dot(hn, wg_ref[...])


def _rmsnorm(x2, norm_w, w_gate):
    t_rows, d = x2.shape
    return pl.pallas_call(
        _rmsnorm_kernel,
        name="rmsnorm",
        grid=(t_rows // NORM_TM,),
        in_specs=[pl.BlockSpec((NORM_TM, d), lambda i: (i, 0)),
                  pl.BlockSpec((1, d), lambda i: (0, 0)),
                  pl.BlockSpec((d, LANES), lambda i: (0, 0))],
        out_specs=[pl.BlockSpec((NORM_TM, d), lambda i: (i, 0)),
                   pl.BlockSpec((NORM_TM, LANES), lambda i: (i, 0))],
        out_shape=[jax.ShapeDtypeStruct((t_rows, d), BF16),
                   jax.ShapeDtypeStruct((t_rows, LANES), F32)],
        compiler_params=_cparams(("parallel",)),
    )(x2, norm_w, w_gate)


def _rope(acc, cos, sin_signed):
    outs = []
    for c in range(acc.shape[1] // LANES):
        t = acc[:, c * LANES:(c + 1) * LANES]
        outs.append(t * cos + pltpu.roll(t, LANES // 2, axis=1) * sin_signed)
    return jnp.concatenate(outs, axis=1)


def _proj_kernel(mode, h_ref, w_ref, *refs):
    o_ref = refs[-1]
    for r in range(PJ_TM // ROW_CHUNK):
        rows = slice(r * ROW_CHUNK, (r + 1) * ROW_CHUNK)
        acc = _dot(h_ref[rows, :], w_ref[...])
        if mode == "scale":
            o_ref[rows, :] = (acc * refs[0][...]).astype(BF16)
        elif mode == "sigmoid":
            o_ref[rows, :] = _sigmoid(acc + refs[0][...]).astype(BF16)
        elif mode == "rope":
            cos_ref, sin_ref, cs_ref = refs[:3]
            o_ref[rows, :] = (_rope(acc, cos_ref[rows, :], sin_ref[rows, :]) * cs_ref[...]).astype(BF16)
        else:
            assert mode == "transpose"
            o_ref[0, :, rows] = (acc * refs[0][...]).T.astype(BF16)


def _proj(mode, hn, w, aux, seq, tn=PJ_TN):
    t_rows, d = hn.shape
    n = w.shape[1]
    tn = min(tn, n)
    s_blocks = seq // PJ_TM
    col = pl.BlockSpec((1, tn), lambda i, j: (0, j))
    pos = pl.BlockSpec((PJ_TM, LANES), lambda i, j: (i % s_blocks, 0))
    aux_specs = {"scale": [col], "sigmoid": [col], "rope": [pos, pos, col], "transpose": [col]}[mode]
    if mode == "transpose":
        out_spec = pl.BlockSpec((1, tn, PJ_TM), lambda i, j: (i // s_blocks, j, i % s_blocks))
        out_shape = jax.ShapeDtypeStruct((t_rows // seq, n, seq), BF16)
    else:
        out_spec = pl.BlockSpec((PJ_TM, tn), lambda i, j: (i, j))
        out_shape = jax.ShapeDtypeStruct((t_rows, n), BF16)
    return pl.pallas_call(
        functools.partial(_proj_kernel, mode),
        name="proj_" + mode,
        grid=(t_rows // PJ_TM, n // tn),
        in_specs=[pl.BlockSpec((PJ_TM, d), lambda i, j: (i, 0)),
                  pl.BlockSpec((d, tn), lambda i, j: (0, j))] + aux_specs,
        out_specs=out_spec,
        out_shape=out_shape,
        compiler_params=_cparams(("parallel", "arbitrary")),
    )(hn, w, *aux)


L = M_CHUNK
MS_SUB = 4
DV_EXT = M_DV + LANES


def _softcap(t):
    return GATE_CAP * jnp.tanh(t / GATE_CAP)


def _log_sigmoid(t):
    return jnp.minimum(t, 0.0) - jnp.log(1.0 + jnp.exp(-jnp.abs(t)))


def _gate_act(pre, is_forget):
    c = _softcap(pre)
    return jnp.where(is_forget, _log_sigmoid(c), c)


def _split_dot(a, b, a_is_exact):
    if a_is_exact:
        hi = b.astype(BF16)
        lo = (b - hi.astype(F32)).astype(BF16)
        ab = a.astype(BF16)
        return _dot(ab, hi) + _dot(ab, lo)
    hi = a.astype(BF16)
    lo = (a - hi.astype(F32)).astype(BF16)
    bb = b.astype(BF16)
    return _dot(hi, bb) + _dot(lo, bb)


def _mlstm_kernel(qf_ref, kf_ref, vf_ref, gf_ref, qb_ref, kb_ref, vb_ref, gb_ref,
                  brow_ref, bcol_ref, hf_ref, hb_ref, c_ref, m_ref):
    step = pl.program_id(1)

    @pl.when(step == 0)
    def _():
        c_ref[...] = jnp.zeros_like(c_ref)
        m_ref[...] = jnp.zeros_like(m_ref)

    row = lax.broadcasted_iota(jnp.int32, (L, L), 0)
    col = lax.broadcasted_iota(jnp.int32, (L, L), 1)
    lane_id = lax.broadcasted_iota(jnp.int32, (1, LANES), 1)
    sub_id = lax.broadcasted_iota(jnp.int32, (LANES, 1), 0)
    forget_lane = (lane_id % (2 * M_HEADS)) >= M_HEADS
    forget_sub = (sub_id % (2 * M_HEADS)) >= M_HEADS
    ones_ext = jnp.ones((L, LANES), BF16)

    dirs = ((qf_ref, kf_ref, vf_ref, gf_ref, hf_ref), (qb_ref, kb_ref, vb_ref, gb_ref, hb_ref))
    for sub, d in [(sub, d) for sub in range(MS_SUB) for d in range(2)]:
        q_blk, k_blk, v_blk, g_blk, h_blk = dirs[d]
        r0 = (sub if d == 0 else MS_SUB - 1 - sub) * L
        q_ref, v_ref, g_ref, h_ref = (ref.at[r0:r0 + L, :] for ref in (q_blk, v_blk, g_blk, h_blk))
        kt_ref = k_blk.at[0, :, r0:r0 + L]
        visible = (row >= col) if d == 0 else (col >= row)
        vis_f = visible.astype(F32)

        g = g_ref[...]
        g_t = g.T
        act_c = _gate_act(g + brow_ref[...], forget_lane)
        act_r = _gate_act(g_t + bcol_ref[...], forget_sub)
        cum_c = _split_dot(vis_f, act_c, True)
        cum_r = _split_dot(act_r, vis_f.T, False)

        for h in range(M_HEADS):
            idx = d * M_HEADS + h
            ci = d * 2 * M_HEADS + h
            cf = ci + M_HEADS
            bc = cum_c[:, cf:cf + 1]
            br = cum_r[cf:cf + 1, :]
            igr = act_r[ci:ci + 1, :]
            b_last = br[:, L - 1:L] if d == 0 else br[:, 0:1]
            m_old = m_ref[idx][0:1, 0:1]

            q = q_ref[:, h * M_DQK:(h + 1) * M_DQK]
            kt = kt_ref[h * M_DQK:(h + 1) * M_DQK, :]
            v_ext = jnp.concatenate([v_ref[:, h * M_DV:(h + 1) * M_DV], ones_ext], axis=1)

            dmat = jnp.where(visible, bc - br + igr, -jnp.inf)
            m_loc = jnp.max(dmat, axis=1, keepdims=True)
            s = _dot(q, kt) * jnp.exp(dmat - m_loc)
            sv = _dot(s.astype(BF16), v_ext)
            g_row = b_last - br + igr
            mg = jnp.max(g_row, axis=1, keepdims=True)
            kw_t = (kt.astype(F32) * jnp.exp(g_row - mg)).astype(BF16)
            u = _dot(kw_t, v_ext)

            c_old = c_ref[idx]
            inter = bc + m_old
            m_t = jnp.maximum(inter, m_loc)
            comb = jnp.exp(inter - m_t) * _dot(q, c_old.astype(BF16)) + jnp.exp(m_loc - m_t) * sv
            num = comb[:, :M_DV]
            den = comb[:, M_DV:M_DV + 1]
            hval = num / jnp.maximum(jnp.abs(den), jnp.exp(-m_t))
            h_ref[:, h * M_DV:(h + 1) * M_DV] = hval

            m_new = jnp.maximum(b_last + m_old, mg)
            c_ref[idx] = jnp.exp(b_last + m_old - m_new) * c_old + jnp.exp(mg - m_new) * u
            m_ref[idx] = jnp.broadcast_to(m_new, (SUBLANES, LANES))


def _mlstm(vq, kvt, gates, bias_row, bias_col, batch, seq):
    t_rows = vq.shape[0]
    rows = MS_SUB * L
    nc = seq // rows
    fwd = lambda b, c: b * nc + c
    bwd = lambda b, c: b * nc + (nc - 1 - c)
    qk_w = M_HEADS * M_DQK
    in_specs = []
    for ch in (fwd, bwd):
        in_specs += [
            pl.BlockSpec((rows, qk_w), lambda b, c, ch=ch: (ch(b, c), M_WIDTH // qk_w)),
            pl.BlockSpec((1, qk_w, rows), lambda b, c, ch=ch: (b, 0, ch(0, c))),
            pl.BlockSpec((rows, M_WIDTH), lambda b, c, ch=ch: (ch(b, c), 0)),
            pl.BlockSpec((rows, LANES), lambda b, c, ch=ch: (ch(b, c), 0)),
        ]
    in_specs += [pl.BlockSpec((1, LANES), lambda b, c: (0, 0)),
                 pl.BlockSpec((LANES, 1), lambda b, c: (0, 0))]
    return pl.pallas_call(
        _mlstm_kernel,
        name="mlstm",
        grid=(batch, nc),
        in_specs=in_specs,
        out_specs=[pl.BlockSpec((rows, M_WIDTH), lambda b, c: (fwd(b, c), 0)),
                   pl.BlockSpec((rows, M_WIDTH), lambda b, c: (bwd(b, c), 0))],
        out_shape=[jax.ShapeDtypeStruct((t_rows, M_WIDTH), F32)] * 2,
        scratch_shapes=[pltpu.VMEM((2 * M_HEADS, M_DQK, DV_EXT), F32),
                        pltpu.VMEM((2 * M_HEADS, SUBLANES, LANES), F32)],
        compiler_params=_cparams(("parallel", "arbitrary")),
    )(vq, kvt, vq, gates, vq, kvt, vq, gates, bias_row, bias_col)


AT_TQ = 1024
AT_TK = 1024
AT_CG = 256


def _attn_kernel(q_ref, qn_ref, k_ref, vt_ref, lq1_ref, lk1_ref, lq2_ref, lk2_ref, nw_ref,
                 o_ref, acc1_ref, acc2_ref, sa1_ref, sa2_ref, sb1_ref, sb2_ref, mba_ref):
    seq = k_ref.shape[0]
    nblk = seq // AT_TK
    qi = pl.program_id(2)
    lane = lax.broadcasted_iota(jnp.int32, (1, LANES), 1)
    in_map1 = (lane % A_DH) < (A_DH // 2)

    def split_maps(q):
        zero = jnp.zeros_like(q)
        return jnp.where(in_map1, q, zero), jnp.where(in_map1, zero, q)

    q_cur = split_maps(q_ref[...])
    acc1_ref[...] = jnp.zeros_like(acc1_ref)
    acc2_ref[...] = jnp.zeros_like(acc2_ref)

    groups = [slice(g * AT_CG, (g + 1) * AT_CG) for g in range(AT_TQ // AT_CG)]
    ng = len(groups)

    def produce_tasks(i, qs, s_refs, mbs):
        off = pl.multiple_of(i * AT_TK, AT_TK)
        kblk = k_ref[pl.ds(off, AT_TK), :]
        tasks = []
        for mi, (qm, s_ref) in enumerate(zip(qs, s_refs)):
            for gi, gs in enumerate(groups):
                def task(qm=qm, s_ref=s_ref, gs=gs, idx=mi * ng + gi):
                    s = _dot_nt(kblk, qm[gs, :])
                    s_ref[:, gs] = s
                    mbs[idx] = jnp.max(s, axis=0, keepdims=True)
                tasks.append(task)
        return tasks

    def consume_tasks(i, s_refs, mbs, stats, out):
        off = pl.multiple_of(i * AT_TK, AT_TK)
        vtblk = vt_ref[0, :, pl.ds(off, AT_TK)]
        tasks = []
        for mi, (s_ref, acc_ref) in enumerate(zip(s_refs, (acc1_ref, acc2_ref))):
            for gi, gs in enumerate(groups):
                def task(s_ref=s_ref, acc_ref=acc_ref, gs=gs, idx=mi * ng + gi):
                    m, l = stats[2 * idx], stats[2 * idx + 1]
                    m_new = jnp.maximum(m, mbs[idx])
                    alpha = jnp.exp2(m - m_new)
                    p = jnp.exp2(s_ref[:, gs] - m_new)
                    out[2 * idx] = m_new
                    out[2 * idx + 1] = alpha * l + jnp.sum(p, axis=0, keepdims=True)
                    acc_ref[:, gs] = alpha * acc_ref[:, gs] + _dot(vtblk, p.astype(BF16))
                tasks.append(task)
        return tasks

    def produce(i, qs, s_refs):
        mbs = [None] * (2 * ng)
        for task in produce_tasks(i, qs, s_refs, mbs):
            task()
        return tuple(mbs)

    def produce_and_consume(ip, qs, p_refs, ic, c_refs, mbs_c, stats):
        mbs_p = [None] * (2 * ng)
        out = [None] * (4 * ng)
        for pt, ct in zip(produce_tasks(ip, qs, p_refs, mbs_p), consume_tasks(ic, c_refs, mbs_c, stats, out)):
            pt()
            ct()
        return tuple(mbs_p), tuple(out)

    slot_a, slot_b = (sa1_ref, sa2_ref), (sb1_ref, sb2_ref)

    def save_maxima(mbs):
        for idx, mb in enumerate(mbs):
            mba_ref[idx // ng:idx // ng + 1, groups[idx % ng]] = mb

    @pl.when(qi == 0)
    def _():
        save_maxima(produce(0, q_cur, slot_a))

    def body(j, carry):
        mb_a, stats = carry[:2 * ng], carry[2 * ng:]
        mb_b, stats = produce_and_consume(2 * j + 1, q_cur, slot_b, 2 * j, slot_a, mb_a, stats)
        mb_a, stats = produce_and_consume(2 * j + 2, q_cur, slot_a, 2 * j + 1, slot_b, mb_b, stats)
        return (*mb_a, *stats)

    neg = jnp.full((1, AT_CG), -jnp.inf, F32)
    zer = jnp.zeros((1, AT_CG), F32)
    mb_a0 = tuple(mba_ref[idx // ng:idx // ng + 1, groups[idx % ng]] for idx in range(2 * ng))
    carry = lax.fori_loop(0, nblk // 2 - 1, body, (*mb_a0, *((neg, zer) * (2 * ng))))
    mb_a, stats = carry[:2 * ng], carry[2 * ng:]
    mb_b, stats = produce_and_consume(nblk - 1, q_cur, slot_b, nblk - 2, slot_a, mb_a, stats)
    mb_next, stats = produce_and_consume(0, split_maps(qn_ref[...]), slot_a, nblk - 1, slot_b, mb_b, stats)
    save_maxima(mb_next)
    l1 = jnp.concatenate([stats[2 * g + 1] for g in range(ng)], axis=1)
    l2 = jnp.concatenate([stats[2 * (ng + g) + 1] for g in range(ng)], axis=1)

    lam = (jnp.exp(jnp.sum(lq1_ref[...] * lk1_ref[...], axis=1, keepdims=True))
           - jnp.exp(jnp.sum(lq2_ref[...] * lk2_ref[...], axis=1, keepdims=True))
           + LAM_INIT)
    o = acc1_ref[...] * (1.0 / l1) - acc2_ref[...] * (lam / l2)
    ms = jnp.mean(o * o, axis=0, keepdims=True)
    y = o * lax.rsqrt(ms + EPS) * nw_ref[...] * (1.0 - LAM_INIT)
    o_ref[...] = y.T.astype(BF16)


def _attention(qk, kvt, lq1, lk1, lq2, lk2, norm_w, batch, seq):
    v_blk0 = (M_HEADS * M_DQK) // A_DV
    t_rows = qk.shape[0]
    nq = seq // AT_TQ
    small = pl.BlockSpec((1, A_DH), lambda b, h, i: (0, 0))
    return pl.pallas_call(
        _attn_kernel,
        name="attention",
        grid=(batch, A_HEADS, nq),
        in_specs=[
            pl.BlockSpec((AT_TQ, LANES), lambda b, h, i: (b * nq + i, h)),
            pl.BlockSpec((AT_TQ, LANES), lambda b, h, i: (b * nq + jnp.minimum(i + 1, nq - 1), h)),
            pl.BlockSpec((seq, LANES), lambda b, h, i: (b, A_HEADS + h)),
            pl.BlockSpec((1, A_DV, seq), lambda b, h, i: (b, v_blk0 + h, 0)),
            small, small, small, small,
            pl.BlockSpec((A_DV, 1), lambda b, h, i: (0, 0)),
        ],
        out_specs=pl.BlockSpec((AT_TQ, LANES), lambda b, h, i: (b * nq + i, h)),
        out_shape=jax.ShapeDtypeStruct((t_rows, A_WIDTH), BF16),
        scratch_shapes=([pltpu.VMEM((A_DV, AT_TQ), F32)] * 2 + [pltpu.VMEM((AT_TK, AT_TQ), F32)] * 4
                        + [pltpu.VMEM((SUBLANES, AT_TQ), F32)]),
        compiler_params=_cparams(("arbitrary", "arbitrary", "arbitrary")),
    )(qk, qk, qk, kvt, lq1, lk1, lq2, lk2, norm_w)


MG_TM = 512


def _merge_kernel(hf_ref, hb_ref, mo_ref, ha_ref, gm_ref, ga_ref, nw_ref, wm_ref, wa_ref, out_ref):
    hm = hf_ref[...] + hb_ref[...]
    parts = []
    for h in range(M_HEADS):
        seg = hm[:, h * M_DV:(h + 1) * M_DV]
        ms = jnp.mean(seg * seg, axis=-1, keepdims=True)
        parts.append(seg * lax.rsqrt(ms + EPS))
    hn = jnp.concatenate(parts, axis=1) * nw_ref[...]
    hn = (hn * mo_ref[...].astype(F32)).astype(BF16)
    branch_m = _dot(hn, wm_ref[...])
    branch_a = _dot(ha_ref[...], wa_ref[...])
    mixed = gm_ref[...].astype(F32) * branch_m + ga_ref[...].astype(F32) * branch_a
    out_ref[...] = mixed.astype(BF16)


def _merge(hf, hb, sig, ha, norm_w, w_m, w_a):
    t_rows = hf.shape[0]
    row = lambda i: (i, 0)
    const = lambda i: (0, 0)
    return pl.pallas_call(
        _merge_kernel,
        name="merge",
        grid=(t_rows // MG_TM,),
        in_specs=[
            pl.BlockSpec((MG_TM, M_WIDTH), row),
            pl.BlockSpec((MG_TM, M_WIDTH), row),
            pl.BlockSpec((MG_TM, M_WIDTH), lambda i: (i, N_BRANCH_GATES // M_WIDTH)),
            pl.BlockSpec((MG_TM, A_WIDTH), row),
            pl.BlockSpec((MG_TM, D_MODEL), lambda i: (i, 0)),
            pl.BlockSpec((MG_TM, D_MODEL), lambda i: (i, 1)),
            pl.BlockSpec((1, M_WIDTH), const),
            pl.BlockSpec((M_WIDTH, D_MODEL), const),
            pl.BlockSpec((A_WIDTH, D_MODEL), const),
        ],
        out_specs=pl.BlockSpec((MG_TM, D_MODEL), row),
        out_shape=jax.ShapeDtypeStruct((t_rows, D_MODEL), BF16),
        compiler_params=_cparams(("parallel",)),
    )(hf, hb, sig, ha, sig, sig, norm_w, w_m, w_a)


OP_TM = 512


def _outproj_kernel(mixed_ref, x_ref, w_ref, nw_ref, x1_ref, h2_ref):
    x1 = x_ref[...] + _dot(mixed_ref[...], w_ref[...])
    x1_ref[...] = x1
    ms = jnp.mean(x1 * x1, axis=-1, keepdims=True)
    h2_ref[...] = (x1 * lax.rsqrt(ms + EPS) * nw_ref[...]).astype(BF16)


def _outproj(mixed, x2, w_out, norm_w):
    t_rows = x2.shape[0]
    row = lambda i: (i, 0)
    const = lambda i: (0, 0)
    return pl.pallas_call(
        _outproj_kernel,
        name="outproj",
        grid=(t_rows // OP_TM,),
        in_specs=[
            pl.BlockSpec((OP_TM, D_MODEL), row),
            pl.BlockSpec((OP_TM, D_MODEL), row),
            pl.BlockSpec((D_MODEL, D_MODEL), const),
            pl.BlockSpec((1, D_MODEL), const),
        ],
        out_specs=[pl.BlockSpec((OP_TM, D_MODEL), row), pl.BlockSpec((OP_TM, D_MODEL), row)],
        out_shape=[jax.ShapeDtypeStruct((t_rows, D_MODEL), F32),
                   jax.ShapeDtypeStruct((t_rows, D_MODEL), BF16)],
        compiler_params=_cparams(("parallel",)),
    )(mixed, x2, w_out, norm_w)


FI_TM = 2048
FI_CHUNK = 512
FI_TN = 512


def _ffn_in_kernel(h_ref, wg_ref, wu_ref, out_ref):
    for r in range(FI_TM // FI_CHUNK):
        rows = slice(r * FI_CHUNK, (r + 1) * FI_CHUNK)
        h = h_ref[rows, :]
        gate = _dot(h, wg_ref[...])
        up = _dot(h, wu_ref[...])
        out_ref[rows, :] = (gate * _sigmoid(gate) * up).astype(BF16)


def _ffn_in(h2, w_ffn_in):
    t_rows = h2.shape[0]
    nj = D_FF // FI_TN
    return pl.pallas_call(
        _ffn_in_kernel,
        name="ffn_in",
        grid=(t_rows // FI_TM, nj),
        in_specs=[
            pl.BlockSpec((FI_TM, D_MODEL), lambda i, j: (i, 0)),
            pl.BlockSpec((D_MODEL, FI_TN), lambda i, j: (0, j)),
            pl.BlockSpec((D_MODEL, FI_TN), lambda i, j: (0, nj + j)),
        ],
        out_specs=pl.BlockSpec((FI_TM, FI_TN), lambda i, j: (i, j)),
        out_shape=jax.ShapeDtypeStruct((t_rows, D_FF), BF16),
        compiler_params=_cparams(("parallel", "arbitrary")),
    )(h2, w_ffn_in, w_ffn_in)


FO_TM = 1024
FO_TN = 1024
FO_KSPLIT = 2
FO_TK = D_FF // FO_KSPLIT


def _ffn_out_kernel(act_ref, w_ref, x1_ref, nw_ref, out_ref):
    k = pl.program_id(1)
    j = pl.program_id(2)
    cols = pl.ds(pl.multiple_of(j * FO_TN, FO_TN), FO_TN)
    for r in range(FO_TM // ROW_CHUNK):
        rows = slice(r * ROW_CHUNK, (r + 1) * ROW_CHUNK)
        base = jnp.where(k == 0, x1_ref[rows, :], out_ref[rows, cols])
        out_ref[rows, cols] = base + _dot(act_ref[rows, :], w_ref[...])

    @pl.when((k == FO_KSPLIT - 1) & (j == pl.num_programs(2) - 1))
    def _():
        x2 = out_ref[...]
        ms = jnp.mean(x2 * x2, axis=-1, keepdims=True)
        out_ref[...] = x2 * lax.rsqrt(ms + EPS) * nw_ref[...]


def _ffn_out(act, w_ffn_out, x1, norm_w):
    t_rows = x1.shape[0]
    nj = D_MODEL // FO_TN
    return pl.pallas_call(
        _ffn_out_kernel,
        name="ffn_out",
        grid=(t_rows // FO_TM, FO_KSPLIT, nj),
        in_specs=[
            pl.BlockSpec((FO_TM, FO_TK), lambda i, k, j: (i, k)),
            pl.BlockSpec((FO_TK, FO_TN), lambda i, k, j: (k, j)),
            pl.BlockSpec((FO_TM, FO_TN), lambda i, k, j: (i, jnp.where(k == 0, j, nj - 1))),
            pl.BlockSpec((1, D_MODEL), lambda i, k, j: (0, 0)),
        ],
        out_specs=pl.BlockSpec((FO_TM, D_MODEL), lambda i, k, j: (i, 0)),
        out_shape=jax.ShapeDtypeStruct((t_rows, D_MODEL), F32),
        compiler_params=_cparams(("parallel", "arbitrary", "arbitrary")),
    )(act, w_ffn_out, x1, norm_w)


def _rope_tables(seq):
    inv = ROPE_THETA ** (-jnp.arange(0, A_DH, 2, dtype=F32) / A_DH)
    ang = jnp.arange(seq, dtype=F32)[:, None] * inv[None, :]
    cos = jnp.cos(ang)
    sin = jnp.sin(ang)
    cos_t = jnp.concatenate([cos, cos, cos, cos], axis=1)
    sin_t = jnp.concatenate([-sin, -sin, sin, sin], axis=1)
    return cos_t, sin_t


def _rotary_layout(w_seg):
    d = w_seg.shape[0]
    half = A_DH // 2
    return w_seg.reshape(d, A_HEADS, 2, 2, half).transpose(0, 1, 3, 2, 4).reshape(d, A_WIDTH)


def kernel(x, norm1_w, w_in, b_igate, b_fgate, b_branch_gate, mlstm_norm_w, lam_q1, lam_k1, lam_q2, lam_k2, attn_norm_w, w_branch_m, w_branch_a, w_out, norm2_w, w_ffn_in, w_ffn_out, final_norm_w):
    batch, seq, d = x.shape
    depth = w_in.shape[0]
    assert d == D_MODEL and depth == 1
    assert seq % PJ_TM == 0 and seq % AT_TQ == 0 and seq % (2 * AT_TK) == 0 and seq % (MS_SUB * L) == 0
    assert (batch * seq) % FI_TM == 0 and (batch * seq) % FO_TM == 0
    t_rows = batch * seq
    x2 = x.reshape(t_rows, d)
    cos_t, sin_t = _rope_tables(seq)

    l = 0
    w = w_in[l].astype(BF16)
    qk_w = M_HEADS * M_DQK
    w_vq = jnp.concatenate([w[:, 2 * qk_w:OFF_MO], w[:, OFF_MQ:qk_w]], axis=1)
    w_kvt = jnp.concatenate([w[:, qk_w:2 * qk_w], w[:, OFF_AV:OFF_GT]], axis=1)
    w_sig = jnp.concatenate([w[:, OFF_GT:OFF_GT + N_BRANCH_GATES], w[:, OFF_MO:OFF_MG]], axis=1)
    w_rot = jnp.concatenate([_rotary_layout(w[:, OFF_AQ:OFF_AK]), _rotary_layout(w[:, OFF_AK:OFF_AV])], axis=1)
    w_gate = jnp.pad(w[:, OFF_MG:OFF_AQ], ((0, 0), (0, LANES - N_GATE)))

    scale_vq = jnp.ones((1, M_WIDTH + qk_w), F32)
    scale_kvt = jnp.concatenate([jnp.full((1, qk_w), M_DQK ** -0.5, F32), jnp.ones((1, A_WIDTH), F32)], axis=1)
    scale_rot = jnp.concatenate([jnp.full((1, A_WIDTH), Q_SCALE, F32), jnp.ones((1, A_WIDTH), F32)], axis=1)
    bias_sig = jnp.concatenate([b_branch_gate[l].astype(F32), jnp.zeros((M_WIDTH,), F32)]).reshape(1, -1)
    gate_bias = jnp.stack([b_igate[l], b_fgate[l]], axis=1).reshape(N_GATE).astype(F32)
    gate_bias = jnp.pad(gate_bias, (0, LANES - N_GATE))

    hn, gates = _rmsnorm(x2, norm1_w[l].reshape(1, d), w_gate)
    vq = _proj("scale", hn, w_vq, [scale_vq], seq, tn=PJ_TN_NARROW)
    sig = _proj("sigmoid", hn, w_sig, [bias_sig], seq)
    qk = _proj("rope", hn, w_rot, [cos_t, sin_t, scale_rot], seq)
    kvt = _proj("transpose", hn, w_kvt, [scale_kvt], seq, tn=PJ_TN_NARROW)

    hf, hb = _mlstm(vq, kvt, gates, gate_bias.reshape(1, LANES), gate_bias.reshape(LANES, 1), batch, seq)
    ha = _attention(qk, kvt, lam_q1[l].reshape(1, A_DH), lam_k1[l].reshape(1, A_DH),
                    lam_q2[l].reshape(1, A_DH), lam_k2[l].reshape(1, A_DH),
                    attn_norm_w[l].reshape(A_DV, 1), batch, seq)
    mixed = _merge(hf, hb, sig, ha, mlstm_norm_w[l].reshape(1, M_WIDTH),
                   w_branch_m[l].astype(BF16), w_branch_a[l].astype(BF16))
    x1, h2 = _outproj(mixed, x2, w_out[l].astype(BF16), norm2_w[l].reshape(1, d))
    act = _ffn_in(h2, w_ffn_in[l].astype(BF16))
    out = _ffn_out(act, w_ffn_out[l].astype(BF16), x1, final_norm_w.reshape(1, d))
    return out.reshape(batch, seq, d)
```

```python
import functools
import math

import jax
import jax.numpy as jnp
from jax import lax
from jax.experimental import pallas as pl
from jax.experimental.pallas import tpu as pltpu

F32 = jnp.float32
BF16 = jnp.bfloat16

D_MODEL = 2048
M_HEADS = 4
M_DQK = 128
M_DV = 256
M_CHUNK = 128
GATE_CAP = 15.0
A_HEADS = 8
A_DH = 64
A_DV = 2 * A_DH
ROPE_THETA = 10000.0
D_FF = 5632
EPS = 1e-6
M_WIDTH = M_HEADS * M_DV
A_WIDTH = A_HEADS * A_DV
N_BRANCH_GATES = 2 * D_MODEL
LAM_INIT = 0.8 - 0.6 * math.exp(-0.3 * 0)

OFF_MQ = 0
OFF_MO = 2 * M_HEADS * M_DQK + M_WIDTH
OFF_MG = OFF_MO + M_WIDTH
N_GATE = 4 * M_HEADS
OFF_AQ = OFF_MG + N_GATE
OFF_AK = OFF_AQ + A_WIDTH
OFF_AV = OFF_AK + A_WIDTH
OFF_GT = OFF_AV + A_WIDTH
LANES = 128
SUBLANES = 8

V7X_VMEM_BYTES = 64 * 1024 * 1024
VMEM_LIMIT = V7X_VMEM_BYTES * 7 // 8


def _cparams(sem):
    return pltpu.CompilerParams(dimension_semantics=sem, vmem_limit_bytes=VMEM_LIMIT)


def _dot(a, b):
    return jnp.dot(a, b, preferred_element_type=F32)


def _dot_nt(a, b):
    return lax.dot_general(a, b, (((1,), (1,)), ((), ())), preferred_element_type=F32)


def _dot_tn(a, b):
    return lax.dot_general(a, b, (((0,), (0,)), ((), ())), preferred_element_type=F32)


def _sigmoid(x):
    return 0.5 * jnp.tanh(0.5 * x) + 0.5


NORM_TM = 512
PJ_TM = 2048
PJ_TN = 1024
PJ_TN_NARROW = 768
ROW_CHUNK = 256
Q_SCALE = (A_DH ** -0.5) * math.log2(math.e)


def _rmsnorm_kernel(x_ref, w_ref, wg_ref, o_ref, g_ref):
    x = x_ref[...]
    ms = jnp.mean(x * x, axis=-1, keepdims=True)
    hn = (x * lax.rsqrt(ms + EPS) * w_ref[...]).astype(BF16)
    o_ref[...] = hn
    g_ref[...] = _dot(hn, wg_ref[...])


def _rmsnorm(x2, norm_w, w_gate):
    t_rows, d = x2.shape
    return pl.pallas_call(
        _rmsnorm_kernel,
        name="rmsnorm",
        grid=(t_rows // NORM_TM,),
        in_specs=[pl.BlockSpec((NORM_TM, d), lambda i: (i, 0)),
                  pl.BlockSpec((1, d), lambda i: (0, 0)),
                  pl.BlockSpec((d, LANES), lambda i: (0, 0))],
        out_specs=[pl.BlockSpec((NORM_TM, d), lambda i: (i, 0)),
                   pl.BlockSpec((NORM_TM, LANES), lambda i: (i, 0))],
        out_shape=[jax.ShapeDtypeStruct((t_rows, d), BF16),
                   jax.ShapeDtypeStruct((t_rows, LANES), F32)],
        compiler_params=_cparams(("parallel",)),
    )(x2, norm_w, w_gate)


def _rope(acc, cos, sin_signed):
    outs = []
    for c in range(acc.shape[1] // LANES):
        t = acc[:, c * LANES:(c + 1) * LANES]
        outs.append(t * cos + pltpu.roll(t, LANES // 2, axis=1) * sin_signed)
    return jnp.concatenate(outs, axis=1)


def _proj_kernel(mode, h_ref, w_ref, *refs):
    o_ref = refs[-1]
    for r in range(PJ_TM // ROW_CHUNK):
        rows = slice(r * ROW_CHUNK, (r + 1) * ROW_CHUNK)
        acc = _dot(h_ref[rows, :], w_ref[...])
        if mode == "scale":
            o_ref[rows, :] = (acc * refs[0][...]).astype(BF16)
        elif mode == "sigmoid":
            o_ref[rows, :] = _sigmoid(acc + refs[0][...]).astype(BF16)
        elif mode == "rope":
            cos_ref, sin_ref, cs_ref = refs[:3]
            o_ref[rows, :] = (_rope(acc, cos_ref[rows, :], sin_ref[rows, :]) * cs_ref[...]).astype(BF16)
        else:
            assert mode == "transpose"
            o_ref[0, :, rows] = (acc * refs[0][...]).T.astype(BF16)


def _proj(mode, hn, w, aux, seq, tn=PJ_TN):
    t_rows, d = hn.shape
    n = w.shape[1]
    tn = min(tn, n)
    s_blocks = seq // PJ_TM
    col = pl.BlockSpec((1, tn), lambda i, j: (0, j))
    pos = pl.BlockSpec((PJ_TM, LANES), lambda i, j: (i % s_blocks, 0))
    aux_specs = {"scale": [col], "sigmoid": [col], "rope": [pos, pos, col], "transpose": [col]}[mode]
    if mode == "transpose":
        out_spec = pl.BlockSpec((1, tn, PJ_TM), lambda i, j: (i // s_blocks, j, i % s_blocks))
        out_shape = jax.ShapeDtypeStruct((t_rows // seq, n, seq), BF16)
    else:
        out_spec = pl.BlockSpec((PJ_TM, tn), lambda i, j: (i, j))
        out_shape = jax.ShapeDtypeStruct((t_rows, n), BF16)
    return pl.pallas_call(
        functools.partial(_proj_kernel, mode),
        name="proj_" + mode,
        grid=(t_rows // PJ_TM, n // tn),
        in_specs=[pl.BlockSpec((PJ_TM, d), lambda i, j: (i, 0)),
                  pl.BlockSpec((d, tn), lambda i, j: (0, j))] + aux_specs,
        out_specs=out_spec,
        out_shape=out_shape,
        compiler_params=_cparams(("parallel", "arbitrary")),
    )(hn, w, *aux)


L = M_CHUNK
MS_SUB = 4
DV_EXT = M_DV + LANES


def _softcap(t):
    return GATE_CAP * jnp.tanh(t / GATE_CAP)


def _log_sigmoid(t):
    return jnp.minimum(t, 0.0) - jnp.log(1.0 + jnp.exp(-jnp.abs(t)))


def _gate_act(pre, is_forget):
    c = _softcap(pre)
    return jnp.where(is_forget, _log_sigmoid(c), c)


def _split_dot(a, b, a_is_exact):
    if a_is_exact:
        hi = b.astype(BF16)
        lo = (b - hi.astype(F32)).astype(BF16)
        ab = a.astype(BF16)
        return _dot(ab, hi) + _dot(ab, lo)
    hi = a.astype(BF16)
    lo = (a - hi.astype(F32)).astype(BF16)
    bb = b.astype(BF16)
    return _dot(hi, bb) + _dot(lo, bb)


def _mlstm_kernel(qf_ref, kf_ref, vf_ref, gf_ref, qb_ref, kb_ref, vb_ref, gb_ref,
                  brow_ref, bcol_ref, hf_ref, hb_ref, c_ref, m_ref):
    step = pl.program_id(1)

    @pl.when(step == 0)
    def _():
        c_ref[...] = jnp.zeros_like(c_ref)
        m_ref[...] = jnp.zeros_like(m_ref)

    row = lax.broadcasted_iota(jnp.int32, (L, L), 0)
    col = lax.broadcasted_iota(jnp.int32, (L, L), 1)
    lane_id = lax.broadcasted_iota(jnp.int32, (1, LANES), 1)
    sub_id = lax.broadcasted_iota(jnp.int32, (LANES, 1), 0)
    forget_lane = (lane_id % (2 * M_HEADS)) >= M_HEADS
    forget_sub = (sub_id % (2 * M_HEADS)) >= M_HEADS
    ones_ext = jnp.ones((L, LANES), BF16)

    dirs = ((qf_ref, kf_ref, vf_ref, gf_ref, hf_ref), (qb_ref, kb_ref, vb_ref, gb_ref, hb_ref))
    for sub, d in [(sub, d) for sub in range(MS_SUB) for d in range(2)]:
        q_blk, k_blk, v_blk, g_blk, h_blk = dirs[d]
        r0 = (sub if d == 0 else MS_SUB - 1 - sub) * L
        q_ref, v_ref, g_ref, h_ref = (ref.at[r0:r0 + L, :] for ref in (q_blk, v_blk, g_blk, h_blk))
        kt_ref = k_blk.at[0, :, r0:r0 + L]
        visible = (row >= col) if d == 0 else (col >= row)
        vis_f = visible.astype(F32)

        g = g_ref[...]
        g_t = g.T
        act_c = _gate_act(g + brow_ref[...], forget_lane)
        act_r = _gate_act(g_t + bcol_ref[...], forget_sub)
        cum_c = _split_dot(vis_f, act_c, True)
        cum_r = _split_dot(act_r, vis_f.T, False)

        for h in range(M_HEADS):
            idx = d * M_HEADS + h
            ci = d * 2 * M_HEADS + h
            cf = ci + M_HEADS
            bc = cum_c[:, cf:cf + 1]
            br = cum_r[cf:cf + 1, :]
            igr = act_r[ci:ci + 1, :]
            b_last = br[:, L - 1:L] if d == 0 else br[:, 0:1]
            m_old = m_ref[idx][0:1, 0:1]

            q = q_ref[:, h * M_DQK:(h + 1) * M_DQK]
            kt = kt_ref[h * M_DQK:(h + 1) * M_DQK, :]
            v_ext = jnp.concatenate([v_ref[:, h * M_DV:(h + 1) * M_DV], ones_ext], axis=1)

            dmat = jnp.where(visible, bc - br + igr, -jnp.inf)
            m_loc = jnp.max(dmat, axis=1, keepdims=True)
            s = _dot(q, kt) * jnp.exp(dmat - m_loc)
            sv = _dot(s.astype(BF16), v_ext)
            g_row = b_last - br + igr
            mg = jnp.max(g_row, axis=1, keepdims=True)
            kw_t = (kt.astype(F32) * jnp.exp(g_row - mg)).astype(BF16)
            u = _dot(kw_t, v_ext)

            c_old = c_ref[idx]
            inter = bc + m_old
            m_t = jnp.maximum(inter, m_loc)
            comb = jnp.exp(inter - m_t) * _dot(q, c_old.astype(BF16)) + jnp.exp(m_loc - m_t) * sv
            num = comb[:, :M_DV]
            den = comb[:, M_DV:M_DV + 1]
            hval = num / jnp.maximum(jnp.abs(den), jnp.exp(-m_t))
            h_ref[:, h * M_DV:(h + 1) * M_DV] = hval

            m_new = jnp.maximum(b_last + m_old, mg)
            c_ref[idx] = jnp.exp(b_last + m_old - m_new) * c_old + jnp.exp(mg - m_new) * u
            m_ref[idx] = jnp.broadcast_to(m_new, (SUBLANES, LANES))


def _mlstm(vq, kvt, gates, bias_row, bias_col, batch, seq):
    t_rows = vq.shape[0]
    rows = MS_SUB * L
    nc = seq // rows
    fwd = lambda b, c: b * nc + c
    bwd = lambda b, c: b * nc + (nc - 1 - c)
    qk_w = M_HEADS * M_DQK
    in_specs = []
    for ch in (fwd, bwd):
        in_specs += [
            pl.BlockSpec((rows, qk_w), lambda b, c, ch=ch: (ch(b, c), M_WIDTH // qk_w)),
            pl.BlockSpec((1, qk_w, rows), lambda b, c, ch=ch: (b, 0, ch(0, c))),
            pl.BlockSpec((rows, M_WIDTH), lambda b, c, ch=ch: (ch(b, c), 0)),
            pl.BlockSpec((rows, LANES), lambda b, c, ch=ch: (ch(b, c), 0)),
        ]
    in_specs += [pl.BlockSpec((1, LANES), lambda b, c: (0, 0)),
                 pl.BlockSpec((LANES, 1), lambda b, c: (0, 0))]
    return pl.pallas_call(
        _mlstm_kernel,
        name="mlstm",
        grid=(batch, nc),
        in_specs=in_specs,
        out_specs=[pl.BlockSpec((rows, M_WIDTH), lambda b, c: (fwd(b, c), 0)),
                   pl.BlockSpec((rows, M_WIDTH), lambda b, c: (bwd(b, c), 0))],
        out_shape=[jax.ShapeDtypeStruct((t_rows, M_WIDTH), F32)] * 2,
        scratch_shapes=[pltpu.VMEM((2 * M_HEADS, M_DQK, DV_EXT), F32),
                        pltpu.VMEM((2 * M_HEADS, SUBLANES, LANES), F32)],
        compiler_params=_cparams(("parallel", "arbitrary")),
    )(vq, kvt, vq, gates, vq, kvt, vq, gates, bias_row, bias_col)


AT_TQ = 2048
AT_TK = 1024
AT_CG = 256


def _attn_kernel(q_ref, qn_ref, k_ref, vt_ref, lq1_ref, lk1_ref, lq2_ref, lk2_ref, nw_ref,
                 o_ref, acc1_ref, acc2_ref, sa1_ref, sa2_ref, sb1_ref, sb2_ref, mba_ref):
    seq = k_ref.shape[0]
    nblk = seq // AT_TK
    qi = pl.program_id(2)
    lane = lax.broadcasted_iota(jnp.int32, (1, LANES), 1)
    in_map1 = (lane % A_DH) < (A_DH // 2)

    def split_maps(q):
        zero = jnp.zeros_like(q)
        return jnp.where(in_map1, q, zero), jnp.where(in_map1, zero, q)

    q_cur = split_maps(q_ref[...])
    acc1_ref[...] = jnp.zeros_like(acc1_ref)
    acc2_ref[...] = jnp.zeros_like(acc2_ref)

    groups = [slice(g * AT_CG, (g + 1) * AT_CG) for g in range(AT_TQ // AT_CG)]
    ng = len(groups)

    def produce_tasks(i, qs, s_refs, mbs):
        off = pl.multiple_of(i * AT_TK, AT_TK)
        kblk = k_ref[pl.ds(off, AT_TK), :]
        tasks = []
        for mi, (qm, s_ref) in enumerate(zip(qs, s_refs)):
            for gi, gs in enumerate(groups):
                def task(qm=qm, s_ref=s_ref, gs=gs, idx=mi * ng + gi):
                    s = _dot_nt(kblk, qm[gs, :])
                    s_ref[:, gs] = s
                    mbs[idx] = jnp.max(s, axis=0, keepdims=True)
                tasks.append(task)
        return tasks

    def consume_tasks(i, s_refs, mbs, stats, out):
        off = pl.multiple_of(i * AT_TK, AT_TK)
        vtblk = vt_ref[0, :, pl.ds(off, AT_TK)]
        tasks = []
        for mi, (s_ref, acc_ref) in enumerate(zip(s_refs, (acc1_ref, acc2_ref))):
            for gi, gs in enumerate(groups):
                def task(s_ref=s_ref, acc_ref=acc_ref, gs=gs, idx=mi * ng + gi):
                    m, l = stats[2 * idx], stats[2 * idx + 1]
                    m_new = jnp.maximum(m, mbs[idx])
                    alpha = jnp.exp2(m - m_new)
                    p = jnp.exp2(s_ref[:, gs] - m_new)
                    out[2 * idx] = m_new
                    out[2 * idx + 1] = alpha * l + jnp.sum(p, axis=0, keepdims=True)
                    acc_ref[:, gs] = alpha * acc_ref[:, gs] + _dot(vtblk, p.astype(BF16))
                tasks.append(task)
        return tasks

    def produce(i, qs, s_refs):
        mbs = [None] * (2 * ng)
        for task in produce_tasks(i, qs, s_refs, mbs):
            task()
        return tuple(mbs)

    def produce_and_consume(ip, qs, p_refs, ic, c_refs, mbs_c, stats):
        mbs_p = [None] * (2 * ng)
        out = [None] * (4 * ng)
        for pt, ct in zip(produce_tasks(ip, qs, p_refs, mbs_p), consume_tasks(ic, c_refs, mbs_c, stats, out)):
            pt()
            ct()
        return tuple(mbs_p), tuple(out)

    slot_a, slot_b = (sa1_ref, sa2_ref), (sb1_ref, sb2_ref)

    def save_maxima(mbs):
        for idx, mb in enumerate(mbs):
            mba_ref[idx // ng:idx // ng + 1, groups[idx % ng]] = mb

    @pl.when(qi == 0)
    def _():
        save_maxima(produce(0, q_cur, slot_a))

    def body(j, carry):
        mb_a, stats = carry[:2 * ng], carry[2 * ng:]
        mb_b, stats = produce_and_consume(2 * j + 1, q_cur, slot_b, 2 * j, slot_a, mb_a, stats)
        mb_a, stats = produce_and_consume(2 * j + 2, q_cur, slot_a, 2 * j + 1, slot_b, mb_b, stats)
        return (*mb_a, *stats)

    neg = jnp.full((1, AT_CG), -jnp.inf, F32)
    zer = jnp.zeros((1, AT_CG), F32)
    mb_a0 = tuple(mba_ref[idx // ng:idx // ng + 1, groups[idx % ng]] for idx in range(2 * ng))
    carry = lax.fori_loop(0, nblk // 2 - 1, body, (*mb_a0, *((neg, zer) * (2 * ng))))
    mb_a, stats = carry[:2 * ng], carry[2 * ng:]
    mb_b, stats = produce_and_consume(nblk - 1, q_cur, slot_b, nblk - 2, slot_a, mb_a, stats)
    mb_next, stats = produce_and_consume(0, split_maps(qn_ref[...]), slot_a, nblk - 1, slot_b, mb_b, stats)
    save_maxima(mb_next)
    l1 = jnp.concatenate([stats[2 * g + 1] for g in range(ng)], axis=1)
    l2 = jnp.concatenate([stats[2 * (ng + g) + 1] for g in range(ng)], axis=1)

    lam = (jnp.exp(jnp.sum(lq1_ref[...] * lk1_ref[...], axis=1, keepdims=True))
           - jnp.exp(jnp.sum(lq2_ref[...] * lk2_ref[...], axis=1, keepdims=True))
           + LAM_INIT)
    o = acc1_ref[...] * (1.0 / l1) - acc2_ref[...] * (lam / l2)
    ms = jnp.mean(o * o, axis=0, keepdims=True)
    y = o * lax.rsqrt(ms + EPS) * nw_ref[...] * (1.0 - LAM_INIT)
    o_ref[...] = y.T.astype(BF16)


def _attention(qk, kvt, lq1, lk1, lq2, lk2, norm_w, batch, seq):
    v_blk0 = (M_HEADS * M_DQK) // A_DV
    t_rows = qk.shape[0]
    nq = seq // AT_TQ
    small = pl.BlockSpec((1, A_DH), lambda b, h, i: (0, 0))
    return pl.pallas_call(
        _attn_kernel,
        name="attention",
        grid=(batch, A_HEADS, nq),
        in_specs=[
            pl.BlockSpec((AT_TQ, LANES), lambda b, h, i: (b * nq + i, h)),
            pl.BlockSpec((AT_TQ, LANES), lambda b, h, i: (b * nq + jnp.minimum(i + 1, nq - 1), h)),
            pl.BlockSpec((seq, LANES), lambda b, h, i: (b, A_HEADS + h)),
            pl.BlockSpec((1, A_DV, seq), lambda b, h, i: (b, v_blk0 + h, 0)),
            small, small, small, small,
            pl.BlockSpec((A_DV, 1), lambda b, h, i: (0, 0)),
        ],
        out_specs=pl.BlockSpec((AT_TQ, LANES), lambda b, h, i: (b * nq + i, h)),
        out_shape=jax.ShapeDtypeStruct((t_rows, A_WIDTH), BF16),
        scratch_shapes=([pltpu.VMEM((A_DV, AT_TQ), F32)] * 2 + [pltpu.VMEM((AT_TK, AT_TQ), F32)] * 4
                        + [pltpu.VMEM((SUBLANES, AT_TQ), F32)]),
        compiler_params=_cparams(("arbitrary", "arbitrary", "arbitrary")),
    )(qk, qk, qk, kvt, lq1, lk1, lq2, lk2, norm_w)


MG_TM = 512


def _merge_kernel(hf_ref, hb_ref, mo_ref, ha_ref, gm_ref, ga_ref, nw_ref, wm_ref, wa_ref, out_ref):
    hm = hf_ref[...] + hb_ref[...]
    parts = []
    for h in range(M_HEADS):
        seg = hm[:, h * M_DV:(h + 1) * M_DV]
        ms = jnp.mean(seg * seg, axis=-1, keepdims=True)
        parts.append(seg * lax.rsqrt(ms + EPS))
    hn = jnp.concatenate(parts, axis=1) * nw_ref[...]
    hn = (hn * mo_ref[...].astype(F32)).astype(BF16)
    branch_m = _dot(hn, wm_ref[...])
    branch_a = _dot(ha_ref[...], wa_ref[...])
    mixed = gm_ref[...].astype(F32) * branch_m + ga_ref[...].astype(F32) * branch_a
    out_ref[...] = mixed.astype(BF16)


def _merge(hf, hb, sig, ha, norm_w, w_m, w_a):
    t_rows = hf.shape[0]
    row = lambda i: (i, 0)
    const = lambda i: (0, 0)
    return pl.pallas_call(
        _merge_kernel,
        name="merge",
        grid=(t_rows // MG_TM,),
        in_specs=[
            pl.BlockSpec((MG_TM, M_WIDTH), row),
            pl.BlockSpec((MG_TM, M_WIDTH), row),
            pl.BlockSpec((MG_TM, M_WIDTH), lambda i: (i, N_BRANCH_GATES // M_WIDTH)),
            pl.BlockSpec((MG_TM, A_WIDTH), row),
            pl.BlockSpec((MG_TM, D_MODEL), lambda i: (i, 0)),
            pl.BlockSpec((MG_TM, D_MODEL), lambda i: (i, 1)),
            pl.BlockSpec((1, M_WIDTH), const),
            pl.BlockSpec((M_WIDTH, D_MODEL), const),
            pl.BlockSpec((A_WIDTH, D_MODEL), const),
        ],
        out_specs=pl.BlockSpec((MG_TM, D_MODEL), row),
        out_shape=jax.ShapeDtypeStruct((t_rows, D_MODEL), BF16),
        compiler_params=_cparams(("parallel",)),
    )(hf, hb, sig, ha, sig, sig, norm_w, w_m, w_a)


OP_TM = 512


def _outproj_kernel(mixed_ref, x_ref, w_ref, nw_ref, x1_ref, h2_ref):
    x1 = x_ref[...] + _dot(mixed_ref[...], w_ref[...])
    x1_ref[...] = x1
    ms = jnp.mean(x1 * x1, axis=-1, keepdims=True)
    h2_ref[...] = (x1 * lax.rsqrt(ms + EPS) * nw_ref[...]).astype(BF16)


def _outproj(mixed, x2, w_out, norm_w):
    t_rows = x2.shape[0]
    row = lambda i: (i, 0)
    const = lambda i: (0, 0)
    return pl.pallas_call(
        _outproj_kernel,
        name="outproj",
        grid=(t_rows // OP_TM,),
        in_specs=[
            pl.BlockSpec((OP_TM, D_MODEL), row),
            pl.BlockSpec((OP_TM, D_MODEL), row),
            pl.BlockSpec((D_MODEL, D_MODEL), const),
            pl.BlockSpec((1, D_MODEL), const),
        ],
        out_specs=[pl.BlockSpec((OP_TM, D_MODEL), row), pl.BlockSpec((OP_TM, D_MODEL), row)],
        out_shape=[jax.ShapeDtypeStruct((t_rows, D_MODEL), F32),
                   jax.ShapeDtypeStruct((t_rows, D_MODEL), BF16)],
        compiler_params=_cparams(("parallel",)),
    )(mixed, x2, w_out, norm_w)


FI_TM = 2048
FI_CHUNK = 512
FI_TN = 512


def _ffn_in_kernel(h_ref, wg_ref, wu_ref, out_ref):
    for r in range(FI_TM // FI_CHUNK):
        rows = slice(r * FI_CHUNK, (r + 1) * FI_CHUNK)
        h = h_ref[rows, :]
        gate = _dot(h, wg_ref[...])
        up = _dot(h, wu_ref[...])
        out_ref[rows, :] = (gate * _sigmoid(gate) * up).astype(BF16)


def _ffn_in(h2, w_ffn_in):
    t_rows = h2.shape[0]
    nj = D_FF // FI_TN
    return pl.pallas_call(
        _ffn_in_kernel,
        name="ffn_in",
        grid=(t_rows // FI_TM, nj),
        in_specs=[
            pl.BlockSpec((FI_TM, D_MODEL), lambda i, j: (i, 0)),
            pl.BlockSpec((D_MODEL, FI_TN), lambda i, j: (0, j)),
            pl.BlockSpec((D_MODEL, FI_TN), lambda i, j: (0, nj + j)),
        ],
        out_specs=pl.BlockSpec((FI_TM, FI_TN), lambda i, j: (i, j)),
        out_shape=jax.ShapeDtypeStruct((t_rows, D_FF), BF16),
        compiler_params=_cparams(("parallel", "arbitrary")),
    )(h2, w_ffn_in, w_ffn_in)


FO_TM = 1024
FO_TN = 1024
FO_KSPLIT = 2
FO_TK = D_FF // FO_KSPLIT


def _ffn_out_kernel(act_ref, w_ref, x1_ref, nw_ref, out_ref):
    k = pl.program_id(1)
    j = pl.program_id(2)
    cols = pl.ds(pl.multiple_of(j * FO_TN, FO_TN), FO_TN)
    for r in range(FO_TM // ROW_CHUNK):
        rows = slice(r * ROW_CHUNK, (r + 1) * ROW_CHUNK)
        base = jnp.where(k == 0, x1_ref[rows, :], out_ref[rows, cols])
        out_ref[rows, cols] = base + _dot(act_ref[rows, :], w_ref[...])

    @pl.when((k == FO_KSPLIT - 1) & (j == pl.num_programs(2) - 1))
    def _():
        x2 = out_ref[...]
        ms = jnp.mean(x2 * x2, axis=-1, keepdims=True)
        out_ref[...] = x2 * lax.rsqrt(ms + EPS) * nw_ref[...]


def _ffn_out(act, w_ffn_out, x1, norm_w):
    t_rows = x1.shape[0]
    nj = D_MODEL // FO_TN
    return pl.pallas_call(
        _ffn_out_kernel,
        name="ffn_out",
        grid=(t_rows // FO_TM, FO_KSPLIT, nj),
        in_specs=[
            pl.BlockSpec((FO_TM, FO_TK), lambda i, k, j: (i, k)),
            pl.BlockSpec((FO_TK, FO_TN), lambda i, k, j: (k, j)),
            pl.BlockSpec((FO_TM, FO_TN), lambda i, k, j: (i, jnp.where(k == 0, j, nj - 1))),
            pl.BlockSpec((1, D_MODEL), lambda i, k, j: (0, 0)),
        ],
        out_specs=pl.BlockSpec((FO_TM, D_MODEL), lambda i, k, j: (i, 0)),
        out_shape=jax.ShapeDtypeStruct((t_rows, D_MODEL), F32),
        compiler_params=_cparams(("parallel", "arbitrary", "arbitrary")),
    )(act, w_ffn_out, x1, norm_w)


def _rope_tables(seq):
    inv = ROPE_THETA ** (-jnp.arange(0, A_DH, 2, dtype=F32) / A_DH)
    ang = jnp.arange(seq, dtype=F32)[:, None] * inv[None, :]
    cos = jnp.cos(ang)
    sin = jnp.sin(ang)
    cos_t = jnp.concatenate([cos, cos, cos, cos], axis=1)
    sin_t = jnp.concatenate([-sin, -sin, sin, sin], axis=1)
    return cos_t, sin_t


def _rotary_layout(w_seg):
    d = w_seg.shape[0]
    half = A_DH // 2
    return w_seg.reshape(d, A_HEADS, 2, 2, half).transpose(0, 1, 3, 2, 4).reshape(d, A_WIDTH)


def kernel(x, norm1_w, w_in, b_igate, b_fgate, b_branch_gate, mlstm_norm_w, lam_q1, lam_k1, lam_q2, lam_k2, attn_norm_w, w_branch_m, w_branch_a, w_out, norm2_w, w_ffn_in, w_ffn_out, final_norm_w):
    batch, seq, d = x.shape
    depth = w_in.shape[0]
    assert d == D_MODEL and depth == 1
    assert seq % PJ_TM == 0 and seq % AT_TQ == 0 and seq % (2 * AT_TK) == 0 and seq % (MS_SUB * L) == 0
    assert (batch * seq) % FI_TM == 0 and (batch * seq) % FO_TM == 0
    t_rows = batch * seq
    x2 = x.reshape(t_rows, d)
    cos_t, sin_t = _rope_tables(seq)

    l = 0
    w = w_in[l].astype(BF16)
    qk_w = M_HEADS * M_DQK
    w_vq = jnp.concatenate([w[:, 2 * qk_w:OFF_MO], w[:, OFF_MQ:qk_w]], axis=1)
    w_kvt = jnp.concatenate([w[:, qk_w:2 * qk_w], w[:, OFF_AV:OFF_GT]], axis=1)
    w_sig = jnp.concatenate([w[:, OFF_GT:OFF_GT + N_BRANCH_GATES], w[:, OFF_MO:OFF_MG]], axis=1)
    w_rot = jnp.concatenate([_rotary_layout(w[:, OFF_AQ:OFF_AK]), _rotary_layout(w[:, OFF_AK:OFF_AV])], axis=1)
    w_gate = jnp.pad(w[:, OFF_MG:OFF_AQ], ((0, 0), (0, LANES - N_GATE)))

    scale_vq = jnp.ones((1, M_WIDTH + qk_w), F32)
    scale_kvt = jnp.concatenate([jnp.full((1, qk_w), M_DQK ** -0.5, F32), jnp.ones((1, A_WIDTH), F32)], axis=1)
    scale_rot = jnp.concatenate([jnp.full((1, A_WIDTH), Q_SCALE, F32), jnp.ones((1, A_WIDTH), F32)], axis=1)
    bias_sig = jnp.concatenate([b_branch_gate[l].astype(F32), jnp.zeros((M_WIDTH,), F32)]).reshape(1, -1)
    gate_bias = jnp.stack([b_igate[l], b_fgate[l]], axis=1).reshape(N_GATE).astype(F32)
    gate_bias = jnp.pad(gate_bias, (0, LANES - N_GATE))

    hn, gates = _rmsnorm(x2, norm1_w[l].reshape(1, d), w_gate)
    vq = _proj("scale", hn, w_vq, [scale_vq], seq, tn=PJ_TN_NARROW)
    sig = _proj("sigmoid", hn, w_sig, [bias_sig], seq)
    qk = _proj("rope", hn, w_rot, [cos_t, sin_t, scale_rot], seq)
    kvt = _proj("transpose", hn, w_kvt, [scale_kvt], seq, tn=PJ_TN_NARROW)

    hf, hb = _mlstm(vq, kvt, gates, gate_bias.reshape(1, LANES), gate_bias.reshape(LANES, 1), batch, seq)
    ha = _attention(qk, kvt, lam_q1[l].reshape(1, A_DH), lam_k1[l].reshape(1, A_DH),
                    lam_q2[l].reshape(1, A_DH), lam_k2[l].reshape(1, A_DH),
                    attn_norm_w[l].reshape(A_DV, 1), batch, seq)
    mixed = _merge(hf, hb, sig, ha, mlstm_norm_w[l].reshape(1, M_WIDTH),
                   w_branch_m[l].astype(BF16), w_branch_a[l].astype(BF16))
    x1, h2 = _outproj(mixed, x2, w_out[l].astype(BF16), norm2_w[l].reshape(1, d))
    act = _ffn_in(h2, w_ffn_in[l].astype(BF16))
    out = _ffn_out(act, w_ffn_out[l].astype(BF16), x1, final_norm_w.reshape(1, d))
    return out.reshape(batch, seq, d)
```

```python
import functools
import math

import jax
import jax.numpy as jnp
from jax import lax
from jax.experimental import pallas as pl
from jax.experimental.pallas import tpu as pltpu

F32 = jnp.float32
BF16 = jnp.bfloat16

D_MODEL = 2048
M_HEADS = 4
M_DQK = 128
M_DV = 256
M_CHUNK = 128
GATE_CAP = 15.0
A_HEADS = 8
A_DH = 64
A_DV = 2 * A_DH
ROPE_THETA = 10000.0
D_FF = 5632
EPS = 1e-6
M_WIDTH = M_HEADS * M_DV
A_WIDTH = A_HEADS * A_DV
N_BRANCH_GATES = 2 * D_MODEL
LAM_INIT = 0.8 - 0.6 * math.exp(-0.3 * 0)

OFF_MQ = 0
OFF_MO = 2 * M_HEADS * M_DQK + M_WIDTH
OFF_MG = OFF_MO + M_WIDTH
N_GATE = 4 * M_HEADS
OFF_AQ = OFF_MG + N_GATE
OFF_AK = OFF_AQ + A_WIDTH
OFF_AV = OFF_AK + A_WIDTH
OFF_GT = OFF_AV + A_WIDTH
LANES = 128
SUBLANES = 8

V7X_VMEM_BYTES = 64 * 1024 * 1024
VMEM_LIMIT = V7X_VMEM_BYTES * 7 // 8


def _cparams(sem):
    return pltpu.CompilerParams(dimension_semantics=sem, vmem_limit_bytes=VMEM_LIMIT)


def _dot(a, b):
    return jnp.dot(a, b, preferred_element_type=F32)


def _dot_nt(a, b):
    return lax.dot_general(a, b, (((1,), (1,)), ((), ())), preferred_element_type=F32)


def _dot_tn(a, b):
    return lax.dot_general(a, b, (((0,), (0,)), ((), ())), preferred_element_type=F32)


def _sigmoid(x):
    return 0.5 * jnp.tanh(0.5 * x) + 0.5


NORM_TM = 512
PJ_TM = 2048
PJ_TN = 1024
PJ_TN_NARROW = 512
ROW_CHUNK = 256
Q_SCALE = (A_DH ** -0.5) * math.log2(math.e)


def _rmsnorm_kernel(x_ref, w_ref, wg_ref, o_ref, g_ref):
    x = x_ref[...]
    ms = jnp.mean(x * x, axis=-1, keepdims=True)
    hn = (x * lax.rsqrt(ms + EPS) * w_ref[...]).astype(BF16)
    o_ref[...] = hn
    g_ref[...] = _dot(hn, wg_ref[...])


def _rmsnorm(x2, norm_w, w_gate):
    t_rows, d = x2.shape
    return pl.pallas_call(
        _rmsnorm_kernel,
        name="rmsnorm",
        grid=(t_rows // NORM_TM,),
        in_specs=[pl.BlockSpec((NORM_TM, d), lambda i: (i, 0)),
                  pl.BlockSpec((1, d), lambda i: (0, 0)),
                  pl.BlockSpec((d, LANES), lambda i: (0, 0))],
        out_specs=[pl.BlockSpec((NORM_TM, d), lambda i: (i, 0)),
                   pl.BlockSpec((NORM_TM, LANES), lambda i: (i, 0))],
        out_shape=[jax.ShapeDtypeStruct((t_rows, d), BF16),
                   jax.ShapeDtypeStruct((t_rows, LANES), F32)],
        compiler_params=_cparams(("parallel",)),
    )(x2, norm_w, w_gate)


def _rope(acc, cos, sin_signed):
    outs = []
    for c in range(acc.shape[1] // LANES):
        t = acc[:, c * LANES:(c + 1) * LANES]
        outs.append(t * cos + pltpu.roll(t, LANES // 2, axis=1) * sin_signed)
    return jnp.concatenate(outs, axis=1)


def _proj_kernel(mode, h_ref, w_ref, *refs):
    o_ref = refs[-1]
    for r in range(PJ_TM // ROW_CHUNK):
        rows = slice(r * ROW_CHUNK, (r + 1) * ROW_CHUNK)
        acc = _dot(h_ref[rows, :], w_ref[...])
        if mode == "scale":
            o_ref[rows, :] = (acc * refs[0][...]).astype(BF16)
        elif mode == "sigmoid":
            o_ref[rows, :] = _sigmoid(acc + refs[0][...]).astype(BF16)
        elif mode == "rope":
            cos_ref, sin_ref, cs_ref = refs[:3]
            o_ref[rows, :] = (_rope(acc, cos_ref[rows, :], sin_ref[rows, :]) * cs_ref[...]).astype(BF16)
        else:
            assert mode == "transpose"
            o_ref[0, :, rows] = (acc * refs[0][...]).T.astype(BF16)


def _proj(mode, hn, w, col_ranges, aux, seq, tn=PJ_TN):
    t_rows, d = hn.shape
    (start0, count0), (start1, count1) = (tuple(col_ranges) + ((0, 0),))[:2]
    n_tiles = count0 + count1
    n = n_tiles * tn
    s_blocks = seq // PJ_TM
    w_tile = lambda i, j: (0, jnp.where(j < count0, start0 + j, start1 + j - count0))
    col = pl.BlockSpec((1, tn), lambda i, j: (0, j))
    pos = pl.BlockSpec((PJ_TM, LANES), lambda i, j: (i % s_blocks, 0))
    aux_specs = {"scale": [col], "sigmoid": [col], "rope": [pos, pos, col], "transpose": [col]}[mode]
    if mode == "transpose":
        out_spec = pl.BlockSpec((1, tn, PJ_TM), lambda i, j: (i // s_blocks, j, i % s_blocks))
        out_shape = jax.ShapeDtypeStruct((t_rows // seq, n, seq), BF16)
    else:
        out_spec = pl.BlockSpec((PJ_TM, tn), lambda i, j: (i, j))
        out_shape = jax.ShapeDtypeStruct((t_rows, n), BF16)
    return pl.pallas_call(
        functools.partial(_proj_kernel, mode),
        name="proj_" + mode,
        grid=(t_rows // PJ_TM, n_tiles),
        in_specs=[pl.BlockSpec((PJ_TM, d), lambda i, j: (i, 0)),
                  pl.BlockSpec((d, tn), w_tile)] + aux_specs,
        out_specs=out_spec,
        out_shape=out_shape,
        compiler_params=_cparams(("parallel", "arbitrary")),
    )(hn, w, *aux)


L = M_CHUNK
MS_SUB = 4
DV_EXT = M_DV + LANES


def _softcap(t):
    return GATE_CAP * jnp.tanh(t / GATE_CAP)


def _log_sigmoid(t):
    return jnp.minimum(t, 0.0) - jnp.log(1.0 + jnp.exp(-jnp.abs(t)))


def _gate_act(pre, is_forget):
    c = _softcap(pre)
    return jnp.where(is_forget, _log_sigmoid(c), c)


def _split_dot(a, b, a_is_exact):
    if a_is_exact:
        hi = b.astype(BF16)
        lo = (b - hi.astype(F32)).astype(BF16)
        ab = a.astype(BF16)
        return _dot(ab, hi) + _dot(ab, lo)
    hi = a.astype(BF16)
    lo = (a - hi.astype(F32)).astype(BF16)
    bb = b.astype(BF16)
    return _dot(hi, bb) + _dot(lo, bb)


def _mlstm_kernel(qf_ref, kf_ref, vf_ref, gf_ref, qb_ref, kb_ref, vb_ref, gb_ref,
                  brow_ref, bcol_ref, hf_ref, hb_ref, c_ref, m_ref):
    step = pl.program_id(1)

    @pl.when(step == 0)
    def _():
        c_ref[...] = jnp.zeros_like(c_ref)
        m_ref[...] = jnp.zeros_like(m_ref)

    row = lax.broadcasted_iota(jnp.int32, (L, L), 0)
    col = lax.broadcasted_iota(jnp.int32, (L, L), 1)
    lane_id = lax.broadcasted_iota(jnp.int32, (1, LANES), 1)
    sub_id = lax.broadcasted_iota(jnp.int32, (LANES, 1), 0)
    forget_lane = (lane_id % (2 * M_HEADS)) >= M_HEADS
    forget_sub = (sub_id % (2 * M_HEADS)) >= M_HEADS
    ones_ext = jnp.ones((L, LANES), BF16)

    dirs = ((qf_ref, kf_ref, vf_ref, gf_ref, hf_ref), (qb_ref, kb_ref, vb_ref, gb_ref, hb_ref))
    for sub, d in [(sub, d) for sub in range(MS_SUB) for d in range(2)]:
        q_blk, k_blk, v_blk, g_blk, h_blk = dirs[d]
        r0 = (sub if d == 0 else MS_SUB - 1 - sub) * L
        q_ref, v_ref, g_ref, h_ref = (ref.at[r0:r0 + L, :] for ref in (q_blk, v_blk, g_blk, h_blk))
        kt_ref = k_blk.at[0, :, r0:r0 + L]
        visible = (row >= col) if d == 0 else (col >= row)
        vis_f = visible.astype(F32)

        g = g_ref[...]
        g_t = g.T
        act_c = _gate_act(g + brow_ref[...], forget_lane)
        act_r = _gate_act(g_t + bcol_ref[...], forget_sub)
        cum_c = _split_dot(vis_f, act_c, True)
        cum_r = _split_dot(act_r, vis_f.T, False)

        for h in range(M_HEADS):
            idx = d * M_HEADS + h
            ci = d * 2 * M_HEADS + h
            cf = ci + M_HEADS
            bc = cum_c[:, cf:cf + 1]
            br = cum_r[cf:cf + 1, :]
            igr = act_r[ci:ci + 1, :]
            b_last = br[:, L - 1:L] if d == 0 else br[:, 0:1]
            m_old = m_ref[idx][0:1, 0:1]

            q = q_ref[:, h * M_DQK:(h + 1) * M_DQK]
            kt = kt_ref[h * M_DQK:(h + 1) * M_DQK, :]
            v_ext = jnp.concatenate([v_ref[:, h * M_DV:(h + 1) * M_DV], ones_ext], axis=1)

            dmat = jnp.where(visible, bc - br + igr, -jnp.inf)
            m_loc = jnp.max(dmat, axis=1, keepdims=True)
            s = _dot(q, kt) * jnp.exp(dmat - m_loc)
            sv = _dot(s.astype(BF16), v_ext)
            g_row = b_last - br + igr
            mg = jnp.max(g_row, axis=1, keepdims=True)
            kw_t = (kt.astype(F32) * jnp.exp(g_row - mg)).astype(BF16)
            u = _dot(kw_t, v_ext)

            c_old = c_ref[idx]
            inter = bc + m_old
            m_t = jnp.maximum(inter, m_loc)
            comb = jnp.exp(inter - m_t) * _dot(q, c_old.astype(BF16)) + jnp.exp(m_loc - m_t) * sv
            num = comb[:, :M_DV]
            den = comb[:, M_DV:M_DV + 1]
            hval = num / jnp.maximum(jnp.abs(den), jnp.exp(-m_t))
            h_ref[:, h * M_DV:(h + 1) * M_DV] = hval

            m_new = jnp.maximum(b_last + m_old, mg)
            c_ref[idx] = jnp.exp(b_last + m_old - m_new) * c_old + jnp.exp(mg - m_new) * u
            m_ref[idx] = jnp.broadcast_to(m_new, (SUBLANES, LANES))


def _mlstm(vq, kvt, gates, bias_row, bias_col, batch, seq):
    t_rows = vq.shape[0]
    rows = MS_SUB * L
    nc = seq // rows
    fwd = lambda b, c: b * nc + c
    bwd = lambda b, c: b * nc + (nc - 1 - c)
    qk_w = M_HEADS * M_DQK
    in_specs = []
    for ch in (fwd, bwd):
        in_specs += [
            pl.BlockSpec((rows, qk_w), lambda b, c, ch=ch: (ch(b, c), M_WIDTH // qk_w)),
            pl.BlockSpec((1, qk_w, rows), lambda b, c, ch=ch: (b, 0, ch(0, c))),
            pl.BlockSpec((rows, M_WIDTH), lambda b, c, ch=ch: (ch(b, c), 0)),
            pl.BlockSpec((rows, LANES), lambda b, c, ch=ch: (ch(b, c), 0)),
        ]
    in_specs += [pl.BlockSpec((1, LANES), lambda b, c: (0, 0)),
                 pl.BlockSpec((LANES, 1), lambda b, c: (0, 0))]
    return pl.pallas_call(
        _mlstm_kernel,
        name="mlstm",
        grid=(batch, nc),
        in_specs=in_specs,
        out_specs=[pl.BlockSpec((rows, M_WIDTH), lambda b, c: (fwd(b, c), 0)),
                   pl.BlockSpec((rows, M_WIDTH), lambda b, c: (bwd(b, c), 0))],
        out_shape=[jax.ShapeDtypeStruct((t_rows, M_WIDTH), F32)] * 2,
        scratch_shapes=[pltpu.VMEM((2 * M_HEADS, M_DQK, DV_EXT), F32),
                        pltpu.VMEM((2 * M_HEADS, SUBLANES, LANES), F32)],
        compiler_params=_cparams(("parallel", "arbitrary")),
    )(vq, kvt, vq, gates, vq, kvt, vq, gates, bias_row, bias_col)


AT_TQ = 2048
AT_TK = 1024
AT_CG = 256


def _attn_kernel(q_ref, qn_ref, k_ref, vt_ref, lq1_ref, lk1_ref, lq2_ref, lk2_ref, nw_ref,
                 o_ref, acc1_ref, acc2_ref, sa1_ref, sa2_ref, sb1_ref, sb2_ref, mba_ref):
    seq = k_ref.shape[0]
    nblk = seq // AT_TK
    qi = pl.program_id(2)
    lane = lax.broadcasted_iota(jnp.int32, (1, LANES), 1)
    in_map1 = (lane % A_DH) < (A_DH // 2)

    def split_maps(q):
        zero = jnp.zeros_like(q)
        return jnp.where(in_map1, q, zero), jnp.where(in_map1, zero, q)

    q_cur = split_maps(q_ref[...])
    acc1_ref[...] = jnp.zeros_like(acc1_ref)
    acc2_ref[...] = jnp.zeros_like(acc2_ref)

    groups = [slice(g * AT_CG, (g + 1) * AT_CG) for g in range(AT_TQ // AT_CG)]
    ng = len(groups)

    def produce_tasks(i, qs, s_refs, mbs):
        off = pl.multiple_of(i * AT_TK, AT_TK)
        kblk = k_ref[pl.ds(off, AT_TK), :]
        tasks = []
        for mi, (qm, s_ref) in enumerate(zip(qs, s_refs)):
            for gi, gs in enumerate(groups):
                def task(qm=qm, s_ref=s_ref, gs=gs, idx=mi * ng + gi):
                    s = _dot_nt(kblk, qm[gs, :])
                    s_ref[:, gs] = s
                    mbs[idx] = jnp.max(s, axis=0, keepdims=True)
                tasks.append(task)
        return tasks

    def consume_tasks(i, s_refs, mbs, stats, out):
        off = pl.multiple_of(i * AT_TK, AT_TK)
        vtblk = vt_ref[0, :, pl.ds(off, AT_TK)]
        tasks = []
        for mi, (s_ref, acc_ref) in enumerate(zip(s_refs, (acc1_ref, acc2_ref))):
            for gi, gs in enumerate(groups):
                def task(s_ref=s_ref, acc_ref=acc_ref, gs=gs, idx=mi * ng + gi):
                    m, l = stats[2 * idx], stats[2 * idx + 1]
                    m_new = jnp.maximum(m, mbs[idx])
                    alpha = jnp.exp2(m - m_new)
                    p = jnp.exp2(s_ref[:, gs] - m_new)
                    out[2 * idx] = m_new
                    out[2 * idx + 1] = alpha * l + jnp.sum(p, axis=0, keepdims=True)
                    acc_ref[:, gs] = alpha * acc_ref[:, gs] + _dot(vtblk, p.astype(BF16))
                tasks.append(task)
        return tasks

    def produce(i, qs, s_refs):
        mbs = [None] * (2 * ng)
        for task in produce_tasks(i, qs, s_refs, mbs):
            task()
        return tuple(mbs)

    def produce_and_consume(ip, qs, p_refs, ic, c_refs, mbs_c, stats):
        mbs_p = [None] * (2 * ng)
        out = [None] * (4 * ng)
        for pt, ct in zip(produce_tasks(ip, qs, p_refs, mbs_p), consume_tasks(ic, c_refs, mbs_c, stats, out)):
            pt()
            ct()
        return tuple(mbs_p), tuple(out)

    slot_a, slot_b = (sa1_ref, sa2_ref), (sb1_ref, sb2_ref)

    def save_maxima(mbs):
        for idx, mb in enumerate(mbs):
            mba_ref[idx // ng:idx // ng + 1, groups[idx % ng]] = mb

    @pl.when(qi == 0)
    def _():
        save_maxima(produce(0, q_cur, slot_a))

    def body(j, carry):
        mb_a, stats = carry[:2 * ng], carry[2 * ng:]
        mb_b, stats = produce_and_consume(2 * j + 1, q_cur, slot_b, 2 * j, slot_a, mb_a, stats)
        mb_a, stats = produce_and_consume(2 * j + 2, q_cur, slot_a, 2 * j + 1, slot_b, mb_b, stats)
        return (*mb_a, *stats)

    neg = jnp.full((1, AT_CG), -jnp.inf, F32)
    zer = jnp.zeros((1, AT_CG), F32)
    mb_a0 = tuple(mba_ref[idx // ng:idx // ng + 1, groups[idx % ng]] for idx in range(2 * ng))
    carry = lax.fori_loop(0, nblk // 2 - 1, body, (*mb_a0, *((neg, zer) * (2 * ng))))
    mb_a, stats = carry[:2 * ng], carry[2 * ng:]
    mb_b, stats = produce_and_consume(nblk - 1, q_cur, slot_b, nblk - 2, slot_a, mb_a, stats)
    mb_next, stats = produce_and_consume(0, split_maps(qn_ref[...]), slot_a, nblk - 1, slot_b, mb_b, stats)
    save_maxima(mb_next)
    l1 = jnp.concatenate([stats[2 * g + 1] for g in range(ng)], axis=1)
    l2 = jnp.concatenate([stats[2 * (ng + g) + 1] for g in range(ng)], axis=1)

    lam = (jnp.exp(jnp.sum(lq1_ref[...] * lk1_ref[...], axis=1, keepdims=True))
           - jnp.exp(jnp.sum(lq2_ref[...] * lk2_ref[...], axis=1, keepdims=True))
           + LAM_INIT)
    o = acc1_ref[...] * (1.0 / l1) - acc2_ref[...] * (lam / l2)
    ms = jnp.mean(o * o, axis=0, keepdims=True)
    y = o * lax.rsqrt(ms + EPS) * nw_ref[...] * (1.0 - LAM_INIT)
    o_ref[...] = y.T.astype(BF16)


def _attention(qk, kvt, lq1, lk1, lq2, lk2, norm_w, batch, seq):
    v_blk0 = (M_HEADS * M_DQK) // A_DV
    t_rows = qk.shape[0]
    nq = seq // AT_TQ
    small = pl.BlockSpec((1, A_DH), lambda b, h, i: (0, 0))
    return pl.pallas_call(
        _attn_kernel,
        name="attention",
        grid=(batch, A_HEADS, nq),
        in_specs=[
            pl.BlockSpec((AT_TQ, LANES), lambda b, h, i: (b * nq + i, h)),
            pl.BlockSpec((AT_TQ, LANES), lambda b, h, i: (b * nq + jnp.minimum(i + 1, nq - 1), h)),
            pl.BlockSpec((seq, LANES), lambda b, h, i: (b, A_HEADS + h)),
            pl.BlockSpec((1, A_DV, seq), lambda b, h, i: (b, v_blk0 + h, 0)),
            small, small, small, small,
            pl.BlockSpec((A_DV, 1), lambda b, h, i: (0, 0)),
        ],
        out_specs=pl.BlockSpec((AT_TQ, LANES), lambda b, h, i: (b * nq + i, h)),
        out_shape=jax.ShapeDtypeStruct((t_rows, A_WIDTH), BF16),
        scratch_shapes=([pltpu.VMEM((A_DV, AT_TQ), F32)] * 2 + [pltpu.VMEM((AT_TK, AT_TQ), F32)] * 4
                        + [pltpu.VMEM((SUBLANES, AT_TQ), F32)]),
        compiler_params=_cparams(("arbitrary", "arbitrary", "arbitrary")),
    )(qk, qk, qk, kvt, lq1, lk1, lq2, lk2, norm_w)


MG_TM = 512


def _merge_kernel(hf_ref, hb_ref, mo_ref, ha_ref, gm_ref, ga_ref, nw_ref, wm_ref, wa_ref, out_ref):
    hm = hf_ref[...] + hb_ref[...]
    parts = []
    for h in range(M_HEADS):
        seg = hm[:, h * M_DV:(h + 1) * M_DV]
        ms = jnp.mean(seg * seg, axis=-1, keepdims=True)
        parts.append(seg * lax.rsqrt(ms + EPS))
    hn = jnp.concatenate(parts, axis=1) * nw_ref[...]
    hn = (hn * mo_ref[...].astype(F32)).astype(BF16)
    branch_m = _dot(hn, wm_ref[...])
    branch_a = _dot(ha_ref[...], wa_ref[...])
    mixed = gm_ref[...].astype(F32) * branch_m + ga_ref[...].astype(F32) * branch_a
    out_ref[...] = mixed.astype(BF16)


def _merge(hf, hb, sig, ha, norm_w, w_m, w_a):
    t_rows = hf.shape[0]
    row = lambda i: (i, 0)
    const = lambda i: (0, 0)
    return pl.pallas_call(
        _merge_kernel,
        name="merge",
        grid=(t_rows // MG_TM,),
        in_specs=[
            pl.BlockSpec((MG_TM, M_WIDTH), row),
            pl.BlockSpec((MG_TM, M_WIDTH), row),
            pl.BlockSpec((MG_TM, M_WIDTH), lambda i: (i, N_BRANCH_GATES // M_WIDTH)),
            pl.BlockSpec((MG_TM, A_WIDTH), row),
            pl.BlockSpec((MG_TM, D_MODEL), lambda i: (i, 0)),
            pl.BlockSpec((MG_TM, D_MODEL), lambda i: (i, 1)),
            pl.BlockSpec((1, M_WIDTH), const),
            pl.BlockSpec((M_WIDTH, D_MODEL), const),
            pl.BlockSpec((A_WIDTH, D_MODEL), const),
        ],
        out_specs=pl.BlockSpec((MG_TM, D_MODEL), row),
        out_shape=jax.ShapeDtypeStruct((t_rows, D_MODEL), BF16),
        compiler_params=_cparams(("parallel",)),
    )(hf, hb, sig, ha, sig, sig, norm_w, w_m, w_a)


OP_TM = 512


def _outproj_kernel(mixed_ref, x_ref, w_ref, nw_ref, x1_ref, h2_ref):
    x1 = x_ref[...] + _dot(mixed_ref[...], w_ref[...])
    x1_ref[...] = x1
    ms = jnp.mean(x1 * x1, axis=-1, keepdims=True)
    h2_ref[...] = (x1 * lax.rsqrt(ms + EPS) * nw_ref[...]).astype(BF16)


def _outproj(mixed, x2, w_out, norm_w):
    t_rows = x2.shape[0]
    row = lambda i: (i, 0)
    const = lambda i: (0, 0)
    return pl.pallas_call(
        _outproj_kernel,
        name="outproj",
        grid=(t_rows // OP_TM,),
        in_specs=[
            pl.BlockSpec((OP_TM, D_MODEL), row),
            pl.BlockSpec((OP_TM, D_MODEL), row),
            pl.BlockSpec((D_MODEL, D_MODEL), const),
            pl.BlockSpec((1, D_MODEL), const),
        ],
        out_specs=[pl.BlockSpec((OP_TM, D_MODEL), row), pl.BlockSpec((OP_TM, D_MODEL), row)],
        out_shape=[jax.ShapeDtypeStruct((t_rows, D_MODEL), F32),
                   jax.ShapeDtypeStruct((t_rows, D_MODEL), BF16)],
        compiler_params=_cparams(("parallel",)),
    )(mixed, x2, w_out, norm_w)


FI_TM = 2048
FI_CHUNK = 512
FI_TN = 512


def _ffn_in_kernel(h_ref, wg_ref, wu_ref, out_ref):
    for r in range(FI_TM // FI_CHUNK):
        rows = slice(r * FI_CHUNK, (r + 1) * FI_CHUNK)
        h = h_ref[rows, :]
        gate = _dot(h, wg_ref[...])
        up = _dot(h, wu_ref[...])
        out_ref[rows, :] = (gate * _sigmoid(gate) * up).astype(BF16)


def _ffn_in(h2, w_ffn_in):
    t_rows = h2.shape[0]
    nj = D_FF // FI_TN
    return pl.pallas_call(
        _ffn_in_kernel,
        name="ffn_in",
        grid=(t_rows // FI_TM, nj),
        in_specs=[
            pl.BlockSpec((FI_TM, D_MODEL), lambda i, j: (i, 0)),
            pl.BlockSpec((D_MODEL, FI_TN), lambda i, j: (0, j)),
            pl.BlockSpec((D_MODEL, FI_TN), lambda i, j: (0, nj + j)),
        ],
        out_specs=pl.BlockSpec((FI_TM, FI_TN), lambda i, j: (i, j)),
        out_shape=jax.ShapeDtypeStruct((t_rows, D_FF), BF16),
        compiler_params=_cparams(("parallel", "arbitrary")),
    )(h2, w_ffn_in, w_ffn_in)


FO_TM = 1024
FO_TN = 1024
FO_KSPLIT = 2
FO_TK = D_FF // FO_KSPLIT


def _ffn_out_kernel(act_ref, w_ref, x1_ref, nw_ref, out_ref):
    k = pl.program_id(1)
    j = pl.program_id(2)
    cols = pl.ds(pl.multiple_of(j * FO_TN, FO_TN), FO_TN)
    for r in range(FO_TM // ROW_CHUNK):
        rows = slice(r * ROW_CHUNK, (r + 1) * ROW_CHUNK)
        base = jnp.where(k == 0, x1_ref[rows, :], out_ref[rows, cols])
        out_ref[rows, cols] = base + _dot(act_ref[rows, :], w_ref[...])

    @pl.when((k == FO_KSPLIT - 1) & (j == pl.num_programs(2) - 1))
    def _():
        x2 = out_ref[...]
        ms = jnp.mean(x2 * x2, axis=-1, keepdims=True)
        out_ref[...] = x2 * lax.rsqrt(ms + EPS) * nw_ref[...]


def _ffn_out(act, w_ffn_out, x1, norm_w):
    t_rows = x1.shape[0]
    nj = D_MODEL // FO_TN
    return pl.pallas_call(
        _ffn_out_kernel,
        name="ffn_out",
        grid=(t_rows // FO_TM, FO_KSPLIT, nj),
        in_specs=[
            pl.BlockSpec((FO_TM, FO_TK), lambda i, k, j: (i, k)),
            pl.BlockSpec((FO_TK, FO_TN), lambda i, k, j: (k, j)),
            pl.BlockSpec((FO_TM, FO_TN), lambda i, k, j: (i, jnp.where(k == 0, j, nj - 1))),
            pl.BlockSpec((1, D_MODEL), lambda i, k, j: (0, 0)),
        ],
        out_specs=pl.BlockSpec((FO_TM, D_MODEL), lambda i, k, j: (i, 0)),
        out_shape=jax.ShapeDtypeStruct((t_rows, D_MODEL), F32),
        compiler_params=_cparams(("parallel", "arbitrary", "arbitrary")),
    )(act, w_ffn_out, x1, norm_w)


def _rope_tables(seq):
    inv = ROPE_THETA ** (-jnp.arange(0, A_DH, 2, dtype=F32) / A_DH)
    ang = jnp.arange(seq, dtype=F32)[:, None] * inv[None, :]
    cos = jnp.cos(ang)
    sin = jnp.sin(ang)
    cos_t = jnp.concatenate([cos, cos, cos, cos], axis=1)
    sin_t = jnp.concatenate([-sin, -sin, sin, sin], axis=1)
    return cos_t, sin_t


def _rotary_layout(w_seg):
    d = w_seg.shape[0]
    half = A_DH // 2
    return w_seg.reshape(d, A_HEADS, 2, 2, half).transpose(0, 1, 3, 2, 4).reshape(d, A_WIDTH)


def kernel(x, norm1_w, w_in, b_igate, b_fgate, b_branch_gate, mlstm_norm_w, lam_q1, lam_k1, lam_q2, lam_k2, attn_norm_w, w_branch_m, w_branch_a, w_out, norm2_w, w_ffn_in, w_ffn_out, final_norm_w):
    batch, seq, d = x.shape
    depth = w_in.shape[0]
    assert d == D_MODEL and depth == 1
    assert seq % PJ_TM == 0 and seq % AT_TQ == 0 and seq % (2 * AT_TK) == 0 and seq % (MS_SUB * L) == 0
    assert (batch * seq) % FI_TM == 0 and (batch * seq) % FO_TM == 0
    t_rows = batch * seq
    x2 = x.reshape(t_rows, d)
    cos_t, sin_t = _rope_tables(seq)

    l = 0
    w_full = w_in[l]
    w_al = jnp.concatenate([w_full[:, :OFF_MG], w_full[:, OFF_AQ:]], axis=1).astype(BF16)
    w_gate = jnp.pad(w_full[:, OFF_MG:OFF_AQ], ((0, 0), (0, LANES - N_GATE))).astype(BF16)
    qk_w = M_HEADS * M_DQK
    al_aq = OFF_MG
    al_av = al_aq + 2 * A_WIDTH
    al_gt = al_av + A_WIDTH
    w_rot = jnp.concatenate([_rotary_layout(w_al[:, al_aq:al_aq + A_WIDTH]),
                             _rotary_layout(w_al[:, al_aq + A_WIDTH:al_av])], axis=1)

    scale_vq = jnp.ones((1, M_WIDTH + qk_w), F32)
    scale_kvt = jnp.concatenate([jnp.full((1, qk_w), M_DQK ** -0.5, F32), jnp.ones((1, A_WIDTH), F32)], axis=1)
    scale_rot = jnp.concatenate([jnp.full((1, A_WIDTH), Q_SCALE, F32), jnp.ones((1, A_WIDTH), F32)], axis=1)
    bias_sig = jnp.concatenate([b_branch_gate[l].astype(F32), jnp.zeros((M_WIDTH,), F32)]).reshape(1, -1)
    gate_bias = jnp.stack([b_igate[l], b_fgate[l]], axis=1).reshape(N_GATE).astype(F32)
    gate_bias = jnp.pad(gate_bias, (0, LANES - N_GATE))

    hn, gates = _rmsnorm(x2, norm1_w[l].reshape(1, d), w_gate)
    tn = PJ_TN_NARROW
    vq = _proj("scale", hn, w_al, ((2 * qk_w // tn, M_WIDTH // tn), (OFF_MQ // tn, qk_w // tn)), [scale_vq], seq, tn)
    kvt = _proj("transpose", hn, w_al, ((qk_w // tn, qk_w // tn), (al_av // tn, A_WIDTH // tn)), [scale_kvt], seq, tn)
    sig = _proj("sigmoid", hn, w_al, ((al_gt // PJ_TN, N_BRANCH_GATES // PJ_TN), (OFF_MO // PJ_TN, M_WIDTH // PJ_TN)),
                [bias_sig], seq)
    qk = _proj("rope", hn, w_rot, ((0, 2 * A_WIDTH // PJ_TN),), [cos_t, sin_t, scale_rot], seq)

    hf, hb = _mlstm(vq, kvt, gates, gate_bias.reshape(1, LANES), gate_bias.reshape(LANES, 1), batch, seq)
    ha = _attention(qk, kvt, lam_q1[l].reshape(1, A_DH), lam_k1[l].reshape(1, A_DH),
                    lam_q2[l].reshape(1, A_DH), lam_k2[l].reshape(1, A_DH),
                    attn_norm_w[l].reshape(A_DV, 1), batch, seq)
    mixed = _merge(hf, hb, sig, ha, mlstm_norm_w[l].reshape(1, M_WIDTH),
                   w_branch_m[l].astype(BF16), w_branch_a[l].astype(BF16))
    x1, h2 = _outproj(mixed, x2, w_out[l].astype(BF16), norm2_w[l].reshape(1, d))
    act = _ffn_in(h2, w_ffn_in[l].astype(BF16))
    out = _ffn_out(act, w_ffn_out[l].astype(BF16), x1, final_norm_w.reshape(1, d))
    return out.reshape(batch, seq, d)
```

```python
import functools
import math

import jax
import jax.numpy as jnp
from jax import lax
from jax.experimental import pallas as pl
from jax.experimental.pallas import tpu as pltpu

F32 = jnp.float32
BF16 = jnp.bfloat16

D_MODEL = 2048
M_HEADS = 4
M_DQK = 128
M_DV = 256
M_CHUNK = 128
GATE_CAP = 15.0
A_HEADS = 8
A_DH = 64
A_DV = 2 * A_DH
ROPE_THETA = 10000.0
D_FF = 5632
EPS = 1e-6
M_WIDTH = M_HEADS * M_DV
A_WIDTH = A_HEADS * A_DV
N_BRANCH_GATES = 2 * D_MODEL
LAM_INIT = 0.8 - 0.6 * math.exp(-0.3 * 0)

OFF_MQ = 0
OFF_MO = 2 * M_HEADS * M_DQK + M_WIDTH
OFF_MG = OFF_MO + M_WIDTH
N_GATE = 4 * M_HEADS
OFF_AQ = OFF_MG + N_GATE
OFF_AK = OFF_AQ + A_WIDTH
OFF_AV = OFF_AK + A_WIDTH
OFF_GT = OFF_AV + A_WIDTH
LANES = 128
SUBLANES = 8

V7X_VMEM_BYTES = 64 * 1024 * 1024
VMEM_LIMIT = V7X_VMEM_BYTES * 7 // 8


def _cparams(sem):
    return pltpu.CompilerParams(dimension_semantics=sem, vmem_limit_bytes=VMEM_LIMIT)


def _dot(a, b):
    return jnp.dot(a, b, preferred_element_type=F32)


def _dot_nt(a, b):
    return lax.dot_general(a, b, (((1,), (1,)), ((), ())), preferred_element_type=F32)


def _dot_tn(a, b):
    return lax.dot_general(a, b, (((0,), (0,)), ((), ())), preferred_element_type=F32)


def _sigmoid(x):
    return 0.5 * jnp.tanh(0.5 * x) + 0.5


NORM_TM = 512
PJ_TM = 2048
PJ_TN = 1024
PJ_TN_NARROW = 768
ROW_CHUNK = 256
Q_SCALE = (A_DH ** -0.5) * math.log2(math.e)


def _rmsnorm_kernel(x_ref, w_ref, wg_ref, o_ref, g_ref):
    x = x_ref[...]
    ms = jnp.mean(x * x, axis=-1, keepdims=True)
    hn = (x * lax.rsqrt(ms + EPS) * w_ref[...]).astype(BF16)
    o_ref[...] = hn
    g_ref[...] = _dot(hn, wg_ref[...])


def _rmsnorm(x2, norm_w, w_gate):
    t_rows, d = x2.shape
    return pl.pallas_call(
        _rmsnorm_kernel,
        name="rmsnorm",
        grid=(t_rows // NORM_TM,),
        in_specs=[pl.BlockSpec((NORM_TM, d), lambda i: (i, 0)),
                  pl.BlockSpec((1, d), lambda i: (0, 0)),
                  pl.BlockSpec((d, LANES), lambda i: (0, 0))],
        out_specs=[pl.BlockSpec((NORM_TM, d), lambda i: (i, 0)),
                   pl.BlockSpec((NORM_TM, LANES), lambda i: (i, 0))],
        out_shape=[jax.ShapeDtypeStruct((t_rows, d), BF16),
                   jax.ShapeDtypeStruct((t_rows, LANES), F32)],
        compiler_params=_cparams(("parallel",)),
    )(x2, norm_w, w_gate)


def _rope(acc, cos, sin_signed):
    outs = []
    for c in range(acc.shape[1] // LANES):
        t = acc[:, c * LANES:(c + 1) * LANES]
        outs.append(t * cos + pltpu.roll(t, LANES // 2, axis=1) * sin_signed)
    return jnp.concatenate(outs, axis=1)


def _proj_kernel(mode, h_ref, w_ref, *refs):
    o_ref = refs[-1]
    for r in range(PJ_TM // ROW_CHUNK):
        rows = slice(r * ROW_CHUNK, (r + 1) * ROW_CHUNK)
        acc = _dot(h_ref[rows, :], w_ref[...])
        if mode == "scale":
            o_ref[rows, :] = (acc * refs[0][...]).astype(BF16)
        elif mode == "sigmoid":
            o_ref[rows, :] = _sigmoid(acc + refs[0][...]).astype(BF16)
        elif mode == "rope":
            cos_ref, sin_ref, cs_ref = refs[:3]
            o_ref[rows, :] = (_rope(acc, cos_ref[rows, :], sin_ref[rows, :]) * cs_ref[...]).astype(BF16)
        else:
            assert mode == "transpose"
            o_ref[0, :, rows] = (acc * refs[0][...]).T.astype(BF16)


def _proj(mode, hn, w, aux, seq, tn=PJ_TN):
    t_rows, d = hn.shape
    n = w.shape[1]
    tn = min(tn, n)
    s_blocks = seq // PJ_TM
    col = pl.BlockSpec((1, tn), lambda i, j: (0, j))
    pos = pl.BlockSpec((PJ_TM, LANES), lambda i, j: (i % s_blocks, 0))
    aux_specs = {"scale": [col], "sigmoid": [col], "rope": [pos, pos, col], "transpose": [col]}[mode]
    if mode == "transpose":
        out_spec = pl.BlockSpec((1, tn, PJ_TM), lambda i, j: (i // s_blocks, j, i % s_blocks))
        out_shape = jax.ShapeDtypeStruct((t_rows // seq, n, seq), BF16)
    else:
        out_spec = pl.BlockSpec((PJ_TM, tn), lambda i, j: (i, j))
        out_shape = jax.ShapeDtypeStruct((t_rows, n), BF16)
    return pl.pallas_call(
        functools.partial(_proj_kernel, mode),
        name="proj_" + mode,
        grid=(t_rows // PJ_TM, n // tn),
        in_specs=[pl.BlockSpec((PJ_TM, d), lambda i, j: (i, 0)),
                  pl.BlockSpec((d, tn), lambda i, j: (0, j))] + aux_specs,
        out_specs=out_spec,
        out_shape=out_shape,
        compiler_params=_cparams(("parallel", "arbitrary")),
    )(hn, w, *aux)


L = M_CHUNK
MS_SUB = 4
DV_EXT = M_DV + LANES


def _softcap(t):
    return GATE_CAP * jnp.tanh(t / GATE_CAP)


def _log_sigmoid(t):
    return jnp.minimum(t, 0.0) - jnp.log(1.0 + jnp.exp(-jnp.abs(t)))


def _gate_act(pre, is_forget):
    c = _softcap(pre)
    return jnp.where(is_forget, _log_sigmoid(c), c)


def _split_dot(a, b, a_is_exact):
    if a_is_exact:
        hi = b.astype(BF16)
        lo = (b - hi.astype(F32)).astype(BF16)
        ab = a.astype(BF16)
        return _dot(ab, hi) + _dot(ab, lo)
    hi = a.astype(BF16)
    lo = (a - hi.astype(F32)).astype(BF16)
    bb = b.astype(BF16)
    return _dot(hi, bb) + _dot(lo, bb)


def _mlstm_kernel(qf_ref, kf_ref, vf_ref, gf_ref, qb_ref, kb_ref, vb_ref, gb_ref,
                  brow_ref, bcol_ref, hf_ref, hb_ref, c_ref, m_ref):
    step = pl.program_id(1)

    @pl.when(step == 0)
    def _():
        c_ref[...] = jnp.zeros_like(c_ref)
        m_ref[...] = jnp.zeros_like(m_ref)

    row = lax.broadcasted_iota(jnp.int32, (L, L), 0)
    col = lax.broadcasted_iota(jnp.int32, (L, L), 1)
    lane_id = lax.broadcasted_iota(jnp.int32, (1, LANES), 1)
    sub_id = lax.broadcasted_iota(jnp.int32, (LANES, 1), 0)
    forget_lane = (lane_id % (2 * M_HEADS)) >= M_HEADS
    forget_sub = (sub_id % (2 * M_HEADS)) >= M_HEADS
    ones_ext = jnp.ones((L, LANES), BF16)

    dirs = ((qf_ref, kf_ref, vf_ref, gf_ref, hf_ref), (qb_ref, kb_ref, vb_ref, gb_ref, hb_ref))
    for sub, d in [(sub, d) for sub in range(MS_SUB) for d in range(2)]:
        q_blk, k_blk, v_blk, g_blk, h_blk = dirs[d]
        r0 = (sub if d == 0 else MS_SUB - 1 - sub) * L
        q_ref, v_ref, g_ref, h_ref = (ref.at[r0:r0 + L, :] for ref in (q_blk, v_blk, g_blk, h_blk))
        kt_ref = k_blk.at[0, :, r0:r0 + L]
        visible = (row >= col) if d == 0 else (col >= row)
        vis_f = visible.astype(F32)

        g = g_ref[...]
        g_t = g.T
        act_c = _gate_act(g + brow_ref[...], forget_lane)
        act_r = _gate_act(g_t + bcol_ref[...], forget_sub)
        cum_c = _split_dot(vis_f, act_c, True)
        cum_r = _split_dot(act_r, vis_f.T, False)

        for h in range(M_HEADS):
            idx = d * M_HEADS + h
            ci = d * 2 * M_HEADS + h
            cf = ci + M_HEADS
            bc = cum_c[:, cf:cf + 1]
            br = cum_r[cf:cf + 1, :]
            igr = act_r[ci:ci + 1, :]
            b_last = br[:, L - 1:L] if d == 0 else br[:, 0:1]
            m_old = m_ref[idx][0:1, 0:1]

            q = q_ref[:, h * M_DQK:(h + 1) * M_DQK]
            kt = kt_ref[h * M_DQK:(h + 1) * M_DQK, :]
            v_ext = jnp.concatenate([v_ref[:, h * M_DV:(h + 1) * M_DV], ones_ext], axis=1)

            dmat = jnp.where(visible, bc - br + igr, -jnp.inf)
            m_loc = jnp.max(dmat, axis=1, keepdims=True)
            s = _dot(q, kt) * jnp.exp(dmat - m_loc)
            sv = _dot(s.astype(BF16), v_ext)
            g_row = b_last - br + igr
            mg = jnp.max(g_row, axis=1, keepdims=True)
            kw_t = (kt.astype(F32) * jnp.exp(g_row - mg)).astype(BF16)
            u = _dot(kw_t, v_ext)

            c_old = c_ref[idx]
            inter = bc + m_old
            m_t = jnp.maximum(inter, m_loc)
            comb = jnp.exp(inter - m_t) * _dot(q, c_old.astype(BF16)) + jnp.exp(m_loc - m_t) * sv
            num = comb[:, :M_DV]
            den = comb[:, M_DV:M_DV + 1]
            hval = num / jnp.maximum(jnp.abs(den), jnp.exp(-m_t))
            h_ref[:, h * M_DV:(h + 1) * M_DV] = hval.astype(BF16)

            m_new = jnp.maximum(b_last + m_old, mg)
            c_ref[idx] = jnp.exp(b_last + m_old - m_new) * c_old + jnp.exp(mg - m_new) * u
            m_ref[idx] = jnp.broadcast_to(m_new, (SUBLANES, LANES))


def _mlstm(vq, kvt, gates, bias_row, bias_col, batch, seq):
    t_rows = vq.shape[0]
    rows = MS_SUB * L
    nc = seq // rows
    fwd = lambda b, c: b * nc + c
    bwd = lambda b, c: b * nc + (nc - 1 - c)
    qk_w = M_HEADS * M_DQK
    in_specs = []
    for ch in (fwd, bwd):
        in_specs += [
            pl.BlockSpec((rows, qk_w), lambda b, c, ch=ch: (ch(b, c), M_WIDTH // qk_w)),
            pl.BlockSpec((1, qk_w, rows), lambda b, c, ch=ch: (b, 0, ch(0, c))),
            pl.BlockSpec((rows, M_WIDTH), lambda b, c, ch=ch: (ch(b, c), 0)),
            pl.BlockSpec((rows, LANES), lambda b, c, ch=ch: (ch(b, c), 0)),
        ]
    in_specs += [pl.BlockSpec((1, LANES), lambda b, c: (0, 0)),
                 pl.BlockSpec((LANES, 1), lambda b, c: (0, 0))]
    return pl.pallas_call(
        _mlstm_kernel,
        name="mlstm",
        grid=(batch, nc),
        in_specs=in_specs,
        out_specs=[pl.BlockSpec((rows, M_WIDTH), lambda b, c: (fwd(b, c), 0)),
                   pl.BlockSpec((rows, M_WIDTH), lambda b, c: (bwd(b, c), 0))],
        out_shape=[jax.ShapeDtypeStruct((t_rows, M_WIDTH), BF16)] * 2,
        scratch_shapes=[pltpu.VMEM((2 * M_HEADS, M_DQK, DV_EXT), F32),
                        pltpu.VMEM((2 * M_HEADS, SUBLANES, LANES), F32)],
        compiler_params=_cparams(("parallel", "arbitrary")),
    )(vq, kvt, vq, gates, vq, kvt, vq, gates, bias_row, bias_col)


AT_TQ = 2048
AT_TK = 1024
AT_CG = 256


def _attn_kernel(q_ref, qn_ref, k_ref, vt_ref, lq1_ref, lk1_ref, lq2_ref, lk2_ref, nw_ref,
                 o_ref, acc1_ref, acc2_ref, sa1_ref, sa2_ref, sb1_ref, sb2_ref, mba_ref):
    seq = k_ref.shape[0]
    nblk = seq // AT_TK
    qi = pl.program_id(2)
    lane = lax.broadcasted_iota(jnp.int32, (1, LANES), 1)
    in_map1 = (lane % A_DH) < (A_DH // 2)

    def split_maps(q):
        zero = jnp.zeros_like(q)
        return jnp.where(in_map1, q, zero), jnp.where(in_map1, zero, q)

    q_cur = split_maps(q_ref[...])
    acc1_ref[...] = jnp.zeros_like(acc1_ref)
    acc2_ref[...] = jnp.zeros_like(acc2_ref)

    groups = [slice(g * AT_CG, (g + 1) * AT_CG) for g in range(AT_TQ // AT_CG)]
    ng = len(groups)

    def produce_tasks(i, qs, s_refs, mbs):
        off = pl.multiple_of(i * AT_TK, AT_TK)
        kblk = k_ref[pl.ds(off, AT_TK), :]
        tasks = []
        for mi, (qm, s_ref) in enumerate(zip(qs, s_refs)):
            for gi, gs in enumerate(groups):
                def task(qm=qm, s_ref=s_ref, gs=gs, idx=mi * ng + gi):
                    s = _dot_nt(kblk, qm[gs, :])
                    s_ref[:, gs] = s
                    mbs[idx] = jnp.max(s, axis=0, keepdims=True)
                tasks.append(task)
        return tasks

    def consume_tasks(i, s_refs, mbs, stats, out):
        off = pl.multiple_of(i * AT_TK, AT_TK)
        vtblk = vt_ref[0, :, pl.ds(off, AT_TK)]
        tasks = []
        for mi, (s_ref, acc_ref) in enumerate(zip(s_refs, (acc1_ref, acc2_ref))):
            for gi, gs in enumerate(groups):
                def task(s_ref=s_ref, acc_ref=acc_ref, gs=gs, idx=mi * ng + gi):
                    m, l = stats[2 * idx], stats[2 * idx + 1]
                    m_new = jnp.maximum(m, mbs[idx])
                    alpha = jnp.exp2(m - m_new)
                    p = jnp.exp2(s_ref[:, gs] - m_new)
                    out[2 * idx] = m_new
                    out[2 * idx + 1] = alpha * l + jnp.sum(p, axis=0, keepdims=True)
                    acc_ref[:, gs] = alpha * acc_ref[:, gs] + _dot(vtblk, p.astype(BF16))
                tasks.append(task)
        return tasks

    def produce(i, qs, s_refs):
        mbs = [None] * (2 * ng)
        for task in produce_tasks(i, qs, s_refs, mbs):
            task()
        return tuple(mbs)

    def produce_and_consume(ip, qs, p_refs, ic, c_refs, mbs_c, stats):
        mbs_p = [None] * (2 * ng)
        out = [None] * (4 * ng)
        for pt, ct in zip(produce_tasks(ip, qs, p_refs, mbs_p), consume_tasks(ic, c_refs, mbs_c, stats, out)):
            pt()
            ct()
        return tuple(mbs_p), tuple(out)

    slot_a, slot_b = (sa1_ref, sa2_ref), (sb1_ref, sb2_ref)

    def save_maxima(mbs):
        for idx, mb in enumerate(mbs):
            mba_ref[idx // ng:idx // ng + 1, groups[idx % ng]] = mb

    @pl.when(qi == 0)
    def _():
        save_maxima(produce(0, q_cur, slot_a))

    def body(j, carry):
        mb_a, stats = carry[:2 * ng], carry[2 * ng:]
        mb_b, stats = produce_and_consume(2 * j + 1, q_cur, slot_b, 2 * j, slot_a, mb_a, stats)
        mb_a, stats = produce_and_consume(2 * j + 2, q_cur, slot_a, 2 * j + 1, slot_b, mb_b, stats)
        return (*mb_a, *stats)

    neg = jnp.full((1, AT_CG), -jnp.inf, F32)
    zer = jnp.zeros((1, AT_CG), F32)
    mb_a0 = tuple(mba_ref[idx // ng:idx // ng + 1, groups[idx % ng]] for idx in range(2 * ng))
    carry = lax.fori_loop(0, nblk // 2 - 1, body, (*mb_a0, *((neg, zer) * (2 * ng))))
    mb_a, stats = carry[:2 * ng], carry[2 * ng:]
    mb_b, stats = produce_and_consume(nblk - 1, q_cur, slot_b, nblk - 2, slot_a, mb_a, stats)
    mb_next, stats = produce_and_consume(0, split_maps(qn_ref[...]), slot_a, nblk - 1, slot_b, mb_b, stats)
    save_maxima(mb_next)
    l1 = jnp.concatenate([stats[2 * g + 1] for g in range(ng)], axis=1)
    l2 = jnp.concatenate([stats[2 * (ng + g) + 1] for g in range(ng)], axis=1)

    lam = (jnp.exp(jnp.sum(lq1_ref[...] * lk1_ref[...], axis=1, keepdims=True))
           - jnp.exp(jnp.sum(lq2_ref[...] * lk2_ref[...], axis=1, keepdims=True))
           + LAM_INIT)
    o = acc1_ref[...] * (1.0 / l1) - acc2_ref[...] * (lam / l2)
    ms = jnp.mean(o * o, axis=0, keepdims=True)
    y = o * lax.rsqrt(ms + EPS) * nw_ref[...] * (1.0 - LAM_INIT)
    o_ref[...] = y.T.astype(BF16)


def _attention(qk, kvt, lq1, lk1, lq2, lk2, norm_w, batch, seq):
    v_blk0 = (M_HEADS * M_DQK) // A_DV
    t_rows = qk.shape[0]
    nq = seq // AT_TQ
    small = pl.BlockSpec((1, A_DH), lambda b, h, i: (0, 0))
    return pl.pallas_call(
        _attn_kernel,
        name="attention",
        grid=(batch, A_HEADS, nq),
        in_specs=[
            pl.BlockSpec((AT_TQ, LANES), lambda b, h, i: (b * nq + i, h)),
            pl.BlockSpec((AT_TQ, LANES), lambda b, h, i: (b * nq + jnp.minimum(i + 1, nq - 1), h)),
            pl.BlockSpec((seq, LANES), lambda b, h, i: (b, A_HEADS + h)),
            pl.BlockSpec((1, A_DV, seq), lambda b, h, i: (b, v_blk0 + h, 0)),
            small, small, small, small,
            pl.BlockSpec((A_DV, 1), lambda b, h, i: (0, 0)),
        ],
        out_specs=pl.BlockSpec((AT_TQ, LANES), lambda b, h, i: (b * nq + i, h)),
        out_shape=jax.ShapeDtypeStruct((t_rows, A_WIDTH), BF16),
        scratch_shapes=([pltpu.VMEM((A_DV, AT_TQ), F32)] * 2 + [pltpu.VMEM((AT_TK, AT_TQ), F32)] * 4
                        + [pltpu.VMEM((SUBLANES, AT_TQ), F32)]),
        compiler_params=_cparams(("arbitrary", "arbitrary", "arbitrary")),
    )(qk, qk, qk, kvt, lq1, lk1, lq2, lk2, norm_w)


MG_TM = 512


def _merge_kernel(hf_ref, hb_ref, mo_ref, ha_ref, gm_ref, ga_ref, nw_ref, wm_ref, wa_ref, out_ref):
    hm = hf_ref[...].astype(F32) + hb_ref[...].astype(F32)
    parts = []
    for h in range(M_HEADS):
        seg = hm[:, h * M_DV:(h + 1) * M_DV]
        ms = jnp.mean(seg * seg, axis=-1, keepdims=True)
        parts.append(seg * lax.rsqrt(ms + EPS))
    hn = jnp.concatenate(parts, axis=1) * nw_ref[...]
    hn = (hn * mo_ref[...].astype(F32)).astype(BF16)
    branch_m = _dot(hn, wm_ref[...])
    branch_a = _dot(ha_ref[...], wa_ref[...])
    mixed = gm_ref[...].astype(F32) * branch_m + ga_ref[...].astype(F32) * branch_a
    out_ref[...] = mixed.astype(BF16)


def _merge(hf, hb, sig, ha, norm_w, w_m, w_a):
    t_rows = hf.shape[0]
    row = lambda i: (i, 0)
    const = lambda i: (0, 0)
    return pl.pallas_call(
        _merge_kernel,
        name="merge",
        grid=(t_rows // MG_TM,),
        in_specs=[
            pl.BlockSpec((MG_TM, M_WIDTH), row),
            pl.BlockSpec((MG_TM, M_WIDTH), row),
            pl.BlockSpec((MG_TM, M_WIDTH), lambda i: (i, N_BRANCH_GATES // M_WIDTH)),
            pl.BlockSpec((MG_TM, A_WIDTH), row),
            pl.BlockSpec((MG_TM, D_MODEL), lambda i: (i, 0)),
            pl.BlockSpec((MG_TM, D_MODEL), lambda i: (i, 1)),
            pl.BlockSpec((1, M_WIDTH), const),
            pl.BlockSpec((M_WIDTH, D_MODEL), const),
            pl.BlockSpec((A_WIDTH, D_MODEL), const),
        ],
        out_specs=pl.BlockSpec((MG_TM, D_MODEL), row),
        out_shape=jax.ShapeDtypeStruct((t_rows, D_MODEL), BF16),
        compiler_params=_cparams(("parallel",)),
    )(hf, hb, sig, ha, sig, sig, norm_w, w_m, w_a)


OP_TM = 512


def _outproj_kernel(mixed_ref, x_ref, w_ref, nw_ref, x1_ref, h2_ref):
    x1 = x_ref[...] + _dot(mixed_ref[...], w_ref[...])
    x1_ref[...] = x1
    ms = jnp.mean(x1 * x1, axis=-1, keepdims=True)
    h2_ref[...] = (x1 * lax.rsqrt(ms + EPS) * nw_ref[...]).astype(BF16)


def _outproj(mixed, x2, w_out, norm_w):
    t_rows = x2.shape[0]
    row = lambda i: (i, 0)
    const = lambda i: (0, 0)
    return pl.pallas_call(
        _outproj_kernel,
        name="outproj",
        grid=(t_rows // OP_TM,),
        in_specs=[
            pl.BlockSpec((OP_TM, D_MODEL), row),
            pl.BlockSpec((OP_TM, D_MODEL), row),
            pl.BlockSpec((D_MODEL, D_MODEL), const),
            pl.BlockSpec((1, D_MODEL), const),
        ],
        out_specs=[pl.BlockSpec((OP_TM, D_MODEL), row), pl.BlockSpec((OP_TM, D_MODEL), row)],
        out_shape=[jax.ShapeDtypeStruct((t_rows, D_MODEL), F32),
                   jax.ShapeDtypeStruct((t_rows, D_MODEL), BF16)],
        compiler_params=_cparams(("parallel",)),
    )(mixed, x2, w_out, norm_w)


FI_TM = 2048
FI_CHUNK = 512
FI_TN = 512


def _ffn_in_kernel(h_ref, wg_ref, wu_ref, out_ref):
    for r in range(FI_TM // FI_CHUNK):
        rows = slice(r * FI_CHUNK, (r + 1) * FI_CHUNK)
        h = h_ref[rows, :]
        gate = _dot(h, wg_ref[...])
        up = _dot(h, wu_ref[...])
        out_ref[rows, :] = (gate * _sigmoid(gate) * up).astype(BF16)


def _ffn_in(h2, w_ffn_in):
    t_rows = h2.shape[0]
    nj = D_FF // FI_TN
    return pl.pallas_call(
        _ffn_in_kernel,
        name="ffn_in",
        grid=(t_rows // FI_TM, nj),
        in_specs=[
            pl.BlockSpec((FI_TM, D_MODEL), lambda i, j: (i, 0)),
            pl.BlockSpec((D_MODEL, FI_TN), lambda i, j: (0, j)),
            pl.BlockSpec((D_MODEL, FI_TN), lambda i, j: (0, nj + j)),
        ],
        out_specs=pl.BlockSpec((FI_TM, FI_TN), lambda i, j: (i, j)),
        out_shape=jax.ShapeDtypeStruct((t_rows, D_FF), BF16),
        compiler_params=_cparams(("parallel", "arbitrary")),
    )(h2, w_ffn_in, w_ffn_in)


FO_TM = 1024
FO_TN = 1024
FO_KSPLIT = 2
FO_TK = D_FF // FO_KSPLIT


def _ffn_out_kernel(act_ref, w_ref, x1_ref, nw_ref, out_ref):
    k = pl.program_id(1)
    j = pl.program_id(2)
    cols = pl.ds(pl.multiple_of(j * FO_TN, FO_TN), FO_TN)
    for r in range(FO_TM // ROW_CHUNK):
        rows = slice(r * ROW_CHUNK, (r + 1) * ROW_CHUNK)
        base = jnp.where(k == 0, x1_ref[rows, :], out_ref[rows, cols])
        out_ref[rows, cols] = base + _dot(act_ref[rows, :], w_ref[...])

    @pl.when((k == FO_KSPLIT - 1) & (j == pl.num_programs(2) - 1))
    def _():
        x2 = out_ref[...]
        ms = jnp.mean(x2 * x2, axis=-1, keepdims=True)
        out_ref[...] = x2 * lax.rsqrt(ms + EPS) * nw_ref[...]


def _ffn_out(act, w_ffn_out, x1, norm_w):
    t_rows = x1.shape[0]
    nj = D_MODEL // FO_TN
    return pl.pallas_call(
        _ffn_out_kernel,
        name="ffn_out",
        grid=(t_rows // FO_TM, FO_KSPLIT, nj),
        in_specs=[
            pl.BlockSpec((FO_TM, FO_TK), lambda i, k, j: (i, k)),
            pl.BlockSpec((FO_TK, FO_TN), lambda i, k, j: (k, j)),
            pl.BlockSpec((FO_TM, FO_TN), lambda i, k, j: (i, jnp.where(k == 0, j, nj - 1))),
            pl.BlockSpec((1, D_MODEL), lambda i, k, j: (0, 0)),
        ],
        out_specs=pl.BlockSpec((FO_TM, D_MODEL), lambda i, k, j: (i, 0)),
        out_shape=jax.ShapeDtypeStruct((t_rows, D_MODEL), F32),
        compiler_params=_cparams(("parallel", "arbitrary", "arbitrary")),
    )(act, w_ffn_out, x1, norm_w)


def _rope_tables(seq):
    inv = ROPE_THETA ** (-jnp.arange(0, A_DH, 2, dtype=F32) / A_DH)
    ang = jnp.arange(seq, dtype=F32)[:, None] * inv[None, :]
    cos = jnp.cos(ang)
    sin = jnp.sin(ang)
    cos_t = jnp.concatenate([cos, cos, cos, cos], axis=1)
    sin_t = jnp.concatenate([-sin, -sin, sin, sin], axis=1)
    return cos_t, sin_t


def _rotary_layout(w_seg):
    d = w_seg.shape[0]
    half = A_DH // 2
    return w_seg.reshape(d, A_HEADS, 2, 2, half).transpose(0, 1, 3, 2, 4).reshape(d, A_WIDTH)


def kernel(x, norm1_w, w_in, b_igate, b_fgate, b_branch_gate, mlstm_norm_w, lam_q1, lam_k1, lam_q2, lam_k2, attn_norm_w, w_branch_m, w_branch_a, w_out, norm2_w, w_ffn_in, w_ffn_out, final_norm_w):
    batch, seq, d = x.shape
    depth = w_in.shape[0]
    assert d == D_MODEL and depth == 1
    assert seq % PJ_TM == 0 and seq % AT_TQ == 0 and seq % (2 * AT_TK) == 0 and seq % (MS_SUB * L) == 0
    assert (batch * seq) % FI_TM == 0 and (batch * seq) % FO_TM == 0
    t_rows = batch * seq
    x2 = x.reshape(t_rows, d)
    cos_t, sin_t = _rope_tables(seq)

    l = 0
    w = w_in[l].astype(BF16)
    qk_w = M_HEADS * M_DQK
    w_vq = jnp.concatenate([w[:, 2 * qk_w:OFF_MO], w[:, OFF_MQ:qk_w]], axis=1)
    w_kvt = jnp.concatenate([w[:, qk_w:2 * qk_w], w[:, OFF_AV:OFF_GT]], axis=1)
    w_sig = jnp.concatenate([w[:, OFF_GT:OFF_GT + N_BRANCH_GATES], w[:, OFF_MO:OFF_MG]], axis=1)
    w_rot = jnp.concatenate([_rotary_layout(w[:, OFF_AQ:OFF_AK]), _rotary_layout(w[:, OFF_AK:OFF_AV])], axis=1)
    w_gate = jnp.pad(w[:, OFF_MG:OFF_AQ], ((0, 0), (0, LANES - N_GATE)))

    scale_vq = jnp.ones((1, M_WIDTH + qk_w), F32)
    scale_kvt = jnp.concatenate([jnp.full((1, qk_w), M_DQK ** -0.5, F32), jnp.ones((1, A_WIDTH), F32)], axis=1)
    scale_rot = jnp.concatenate([jnp.full((1, A_WIDTH), Q_SCALE, F32), jnp.ones((1, A_WIDTH), F32)], axis=1)
    bias_sig = jnp.concatenate([b_branch_gate[l].astype(F32), jnp.zeros((M_WIDTH,), F32)]).reshape(1, -1)
    gate_bias = jnp.stack([b_igate[l], b_fgate[l]], axis=1).reshape(N_GATE).astype(F32)
    gate_bias = jnp.pad(gate_bias, (0, LANES - N_GATE))

    hn, gates = _rmsnorm(x2, norm1_w[l].reshape(1, d), w_gate)
    vq = _proj("scale", hn, w_vq, [scale_vq], seq, tn=PJ_TN_NARROW)
    sig = _proj("sigmoid", hn, w_sig, [bias_sig], seq)
    qk = _proj("rope", hn, w_rot, [cos_t, sin_t, scale_rot], seq)
    kvt = _proj("transpose", hn, w_kvt, [scale_kvt], seq, tn=PJ_TN_NARROW)

    hf, hb = _mlstm(vq, kvt, gates, gate_bias.reshape(1, LANES), gate_bias.reshape(LANES, 1), batch, seq)
    ha = _attention(qk, kvt, lam_q1[l].reshape(1, A_DH), lam_k1[l].reshape(1, A_DH),
                    lam_q2[l].reshape(1, A_DH), lam_k2[l].reshape(1, A_DH),
                    attn_norm_w[l].reshape(A_DV, 1), batch, seq)
    mixed = _merge(hf, hb, sig, ha, mlstm_norm_w[l].reshape(1, M_WIDTH),
                   w_branch_m[l].astype(BF16), w_branch_a[l].astype(BF16))
    x1, h2 = _outproj(mixed, x2, w_out[l].astype(BF16), norm2_w[l].reshape(1, d))
    act = _ffn_in(h2, w_ffn_in[l].astype(BF16))
    out = _ffn_out(act, w_ffn_out[l].astype(BF16), x1, final_norm_w.reshape(1, d))
    return out.reshape(batch, seq, d)
```

```python
import functools
import math

import jax
import jax.numpy as jnp
from jax import lax
from jax.experimental import pallas as pl
from jax.experimental.pallas import tpu as pltpu

F32 = jnp.float32
BF16 = jnp.bfloat16

D_MODEL = 2048
M_HEADS = 4
M_DQK = 128
M_DV = 256
M_CHUNK = 128
GATE_CAP = 15.0
A_HEADS = 8
A_DH = 64
A_DV = 2 * A_DH
ROPE_THETA = 10000.0
D_FF = 5632
EPS = 1e-6
M_WIDTH = M_HEADS * M_DV
A_WIDTH = A_HEADS * A_DV
N_BRANCH_GATES = 2 * D_MODEL
LAM_INIT = 0.8 - 0.6 * math.exp(-0.3 * 0)

OFF_MQ = 0
OFF_MO = 2 * M_HEADS * M_DQK + M_WIDTH
OFF_MG = OFF_MO + M_WIDTH
N_GATE = 4 * M_HEADS
OFF_AQ = OFF_MG + N_GATE
OFF_AK = OFF_AQ + A_WIDTH
OFF_AV = OFF_AK + A_WIDTH
OFF_GT = OFF_AV + A_WIDTH
LANES = 128
SUBLANES = 8

V7X_VMEM_BYTES = 64 * 1024 * 1024
VMEM_LIMIT = V7X_VMEM_BYTES * 7 // 8


def _cparams(sem):
    return pltpu.CompilerParams(dimension_semantics=sem, vmem_limit_bytes=VMEM_LIMIT)


def _dot(a, b):
    return jnp.dot(a, b, preferred_element_type=F32)


def _dot_nt(a, b):
    return lax.dot_general(a, b, (((1,), (1,)), ((), ())), preferred_element_type=F32)


def _dot_tn(a, b):
    return lax.dot_general(a, b, (((0,), (0,)), ((), ())), preferred_element_type=F32)


def _sigmoid(x):
    return 0.5 * jnp.tanh(0.5 * x) + 0.5


NORM_TM = 512
PJ_TM = 2048
PJ_TN = 1024
PJ_TN_NARROW = 768
ROW_CHUNK = 256
Q_SCALE = (A_DH ** -0.5) * math.log2(math.e)


def _rmsnorm_kernel(x_ref, w_ref, wg_ref, o_ref, g_ref):
    x = x_ref[...]
    ms = jnp.mean(x * x, axis=-1, keepdims=True)
    hn = (x * lax.rsqrt(ms + EPS) * w_ref[...]).astype(BF16)
    o_ref[...] = hn
    g_ref[...] = _dot(hn, wg_ref[...])


def _rmsnorm(x2, norm_w, w_gate):
    t_rows, d = x2.shape
    return pl.pallas_call(
        _rmsnorm_kernel,
        name="rmsnorm",
        grid=(t_rows // NORM_TM,),
        in_specs=[pl.BlockSpec((NORM_TM, d), lambda i: (i, 0)),
                  pl.BlockSpec((1, d), lambda i: (0, 0)),
                  pl.BlockSpec((d, LANES), lambda i: (0, 0))],
        out_specs=[pl.BlockSpec((NORM_TM, d), lambda i: (i, 0)),
                   pl.BlockSpec((NORM_TM, LANES), lambda i: (i, 0))],
        out_shape=[jax.ShapeDtypeStruct((t_rows, d), BF16),
                   jax.ShapeDtypeStruct((t_rows, LANES), F32)],
        compiler_params=_cparams(("parallel",)),
    )(x2, norm_w, w_gate)


def _rope(acc, cos, sin_signed):
    outs = []
    for c in range(acc.shape[1] // LANES):
        t = acc[:, c * LANES:(c + 1) * LANES]
        outs.append(t * cos + pltpu.roll(t, LANES // 2, axis=1) * sin_signed)
    return jnp.concatenate(outs, axis=1)


def _proj_kernel(mode, h_ref, w_ref, *refs):
    o_ref = refs[-1]
    for r in range(PJ_TM // ROW_CHUNK):
        rows = slice(r * ROW_CHUNK, (r + 1) * ROW_CHUNK)
        acc = _dot(h_ref[rows, :], w_ref[...])
        if mode == "scale":
            o_ref[rows, :] = (acc * refs[0][...]).astype(BF16)
        elif mode == "sigmoid":
            o_ref[rows, :] = _sigmoid(acc + refs[0][...]).astype(BF16)
        elif mode == "rope":
            cos_ref, sin_ref, cs_ref = refs[:3]
            o_ref[rows, :] = (_rope(acc, cos_ref[rows, :], sin_ref[rows, :]) * cs_ref[...]).astype(BF16)
        else:
            assert mode == "transpose"
            o_ref[0, :, rows] = (acc * refs[0][...]).T.astype(BF16)


def _proj(mode, hn, w, aux, seq, tn=PJ_TN):
    t_rows, d = hn.shape
    n = w.shape[1]
    tn = min(tn, n)
    s_blocks = seq // PJ_TM
    col = pl.BlockSpec((1, tn), lambda i, j: (0, j))
    pos = pl.BlockSpec((PJ_TM, LANES), lambda i, j: (i % s_blocks, 0))
    aux_specs = {"scale": [col], "sigmoid": [col], "rope": [pos, pos, col], "transpose": [col]}[mode]
    if mode == "transpose":
        out_spec = pl.BlockSpec((1, tn, PJ_TM), lambda i, j: (i // s_blocks, j, i % s_blocks))
        out_shape = jax.ShapeDtypeStruct((t_rows // seq, n, seq), BF16)
    else:
        out_spec = pl.BlockSpec((PJ_TM, tn), lambda i, j: (i, j))
        out_shape = jax.ShapeDtypeStruct((t_rows, n), BF16)
    return pl.pallas_call(
        functools.partial(_proj_kernel, mode),
        name="proj_" + mode,
        grid=(t_rows // PJ_TM, n // tn),
        in_specs=[pl.BlockSpec((PJ_TM, d), lambda i, j: (i, 0)),
                  pl.BlockSpec((d, tn), lambda i, j: (0, j))] + aux_specs,
        out_specs=out_spec,
        out_shape=out_shape,
        compiler_params=_cparams(("parallel", "arbitrary")),
    )(hn, w, *aux)


L = M_CHUNK
MS_SUB = 8
DV_EXT = M_DV + LANES


def _softcap(t):
    return GATE_CAP * jnp.tanh(t / GATE_CAP)


def _log_sigmoid(t):
    return jnp.minimum(t, 0.0) - jnp.log(1.0 + jnp.exp(-jnp.abs(t)))


def _gate_act(pre, is_forget):
    c = _softcap(pre)
    return jnp.where(is_forget, _log_sigmoid(c), c)


def _split_dot(a, b, a_is_exact):
    if a_is_exact:
        hi = b.astype(BF16)
        lo = (b - hi.astype(F32)).astype(BF16)
        ab = a.astype(BF16)
        return _dot(ab, hi) + _dot(ab, lo)
    hi = a.astype(BF16)
    lo = (a - hi.astype(F32)).astype(BF16)
    bb = b.astype(BF16)
    return _dot(hi, bb) + _dot(lo, bb)


def _mlstm_kernel(vqf_ref, kf_ref, gf_ref, vqb_ref, kb_ref, gb_ref,
                  brow_ref, bcol_ref, hf_ref, hb_ref, c_ref, m_ref):
    step = pl.program_id(1)

    @pl.when(step == 0)
    def _():
        c_ref[...] = jnp.zeros_like(c_ref)
        m_ref[...] = jnp.zeros_like(m_ref)

    row = lax.broadcasted_iota(jnp.int32, (L, L), 0)
    col = lax.broadcasted_iota(jnp.int32, (L, L), 1)
    lane_id = lax.broadcasted_iota(jnp.int32, (1, LANES), 1)
    sub_id = lax.broadcasted_iota(jnp.int32, (LANES, 1), 0)
    forget_lane = (lane_id % (2 * M_HEADS)) >= M_HEADS
    forget_sub = (sub_id % (2 * M_HEADS)) >= M_HEADS
    ones_ext = jnp.ones((L, LANES), BF16)

    dirs = ((vqf_ref, kf_ref, gf_ref, hf_ref), (vqb_ref, kb_ref, gb_ref, hb_ref))
    for sub, d in [(sub, d) for sub in range(MS_SUB) for d in range(2)]:
        vq_blk, k_blk, g_blk, h_blk = dirs[d]
        r0 = (sub if d == 0 else MS_SUB - 1 - sub) * L
        g_ref, h_ref = g_blk.at[r0:r0 + L, :], h_blk.at[r0:r0 + L, :]
        v_ref = vq_blk.at[r0:r0 + L, 0:M_WIDTH]
        q_ref = vq_blk.at[r0:r0 + L, M_WIDTH:M_WIDTH + M_HEADS * M_DQK]
        kt_ref = k_blk.at[0, :, r0:r0 + L]
        visible = (row >= col) if d == 0 else (col >= row)
        vis_f = visible.astype(F32)

        g = g_ref[...]
        g_t = g.T
        act_c = _gate_act(g + brow_ref[...], forget_lane)
        act_r = _gate_act(g_t + bcol_ref[...], forget_sub)
        cum_c = _split_dot(vis_f, act_c, True)
        cum_r = _split_dot(act_r, vis_f.T, False)

        for h in range(M_HEADS):
            idx = d * M_HEADS + h
            ci = d * 2 * M_HEADS + h
            cf = ci + M_HEADS
            bc = cum_c[:, cf:cf + 1]
            br = cum_r[cf:cf + 1, :]
            igr = act_r[ci:ci + 1, :]
            b_last = br[:, L - 1:L] if d == 0 else br[:, 0:1]
            m_old = m_ref[idx][0:1, 0:1]

            q = q_ref[:, h * M_DQK:(h + 1) * M_DQK]
            kt = kt_ref[h * M_DQK:(h + 1) * M_DQK, :]
            v_ext = jnp.concatenate([v_ref[:, h * M_DV:(h + 1) * M_DV], ones_ext], axis=1)

            dmat = jnp.where(visible, bc - br + igr, -jnp.inf)
            m_loc = jnp.max(dmat, axis=1, keepdims=True)
            s = _dot(q, kt) * jnp.exp(dmat - m_loc)
            sv = _dot(s.astype(BF16), v_ext)
            g_row = b_last - br + igr
            mg = jnp.max(g_row, axis=1, keepdims=True)
            kw_t = (kt.astype(F32) * jnp.exp(g_row - mg)).astype(BF16)
            u = _dot(kw_t, v_ext)

            c_old = c_ref[idx]
            inter = bc + m_old
            m_t = jnp.maximum(inter, m_loc)
            comb = jnp.exp(inter - m_t) * _dot(q, c_old.astype(BF16)) + jnp.exp(m_loc - m_t) * sv
            num = comb[:, :M_DV]
            den = comb[:, M_DV:M_DV + 1]
            hval = num / jnp.maximum(jnp.abs(den), jnp.exp(-m_t))
            h_ref[:, h * M_DV:(h + 1) * M_DV] = hval

            m_new = jnp.maximum(b_last + m_old, mg)
            c_ref[idx] = jnp.exp(b_last + m_old - m_new) * c_old + jnp.exp(mg - m_new) * u
            m_ref[idx] = jnp.broadcast_to(m_new, (SUBLANES, LANES))


def _mlstm(vq, kvt, gates, bias_row, bias_col, batch, seq):
    t_rows = vq.shape[0]
    rows = MS_SUB * L
    nc = seq // rows
    fwd = lambda b, c: b * nc + c
    bwd = lambda b, c: b * nc + (nc - 1 - c)
    qk_w = M_HEADS * M_DQK
    in_specs = []
    for ch in (fwd, bwd):
        in_specs += [
            pl.BlockSpec((rows, M_WIDTH + qk_w), lambda b, c, ch=ch: (ch(b, c), 0)),
            pl.BlockSpec((1, qk_w, rows), lambda b, c, ch=ch: (b, 0, ch(0, c))),
            pl.BlockSpec((rows, LANES), lambda b, c, ch=ch: (ch(b, c), 0)),
        ]
    in_specs += [pl.BlockSpec((1, LANES), lambda b, c: (0, 0)),
                 pl.BlockSpec((LANES, 1), lambda b, c: (0, 0))]
    return pl.pallas_call(
        _mlstm_kernel,
        name="mlstm",
        grid=(batch, nc),
        in_specs=in_specs,
        out_specs=[pl.BlockSpec((rows, M_WIDTH), lambda b, c: (fwd(b, c), 0)),
                   pl.BlockSpec((rows, M_WIDTH), lambda b, c: (bwd(b, c), 0))],
        out_shape=[jax.ShapeDtypeStruct((t_rows, M_WIDTH), F32)] * 2,
        scratch_shapes=[pltpu.VMEM((2 * M_HEADS, M_DQK, DV_EXT), F32),
                        pltpu.VMEM((2 * M_HEADS, SUBLANES, LANES), F32)],
        compiler_params=_cparams(("parallel", "arbitrary")),
    )(vq, kvt, gates, vq, kvt, gates, bias_row, bias_col)


AT_TQ = 2048
AT_TK = 1024
AT_CG = 256


def _attn_kernel(q_ref, qn_ref, k_ref, vt_ref, lq1_ref, lk1_ref, lq2_ref, lk2_ref, nw_ref,
                 o_ref, acc1_ref, acc2_ref, sa1_ref, sa2_ref, sb1_ref, sb2_ref, mba_ref):
    seq = k_ref.shape[0]
    nblk = seq // AT_TK
    qi = pl.program_id(2)
    lane = lax.broadcasted_iota(jnp.int32, (1, LANES), 1)
    in_map1 = (lane % A_DH) < (A_DH // 2)

    def split_maps(q):
        zero = jnp.zeros_like(q)
        return jnp.where(in_map1, q, zero), jnp.where(in_map1, zero, q)

    q_cur = split_maps(q_ref[...])
    acc1_ref[...] = jnp.zeros_like(acc1_ref)
    acc2_ref[...] = jnp.zeros_like(acc2_ref)

    groups = [slice(g * AT_CG, (g + 1) * AT_CG) for g in range(AT_TQ // AT_CG)]
    ng = len(groups)

    def produce_tasks(i, qs, s_refs, mbs):
        off = pl.multiple_of(i * AT_TK, AT_TK)
        kblk = k_ref[pl.ds(off, AT_TK), :]
        tasks = []
        for mi, (qm, s_ref) in enumerate(zip(qs, s_refs)):
            for gi, gs in enumerate(groups):
                def task(qm=qm, s_ref=s_ref, gs=gs, idx=mi * ng + gi):
                    s = _dot_nt(kblk, qm[gs, :])
                    s_ref[:, gs] = s
                    mbs[idx] = jnp.max(s, axis=0, keepdims=True)
                tasks.append(task)
        return tasks

    def consume_tasks(i, s_refs, mbs, stats, out):
        off = pl.multiple_of(i * AT_TK, AT_TK)
        vtblk = vt_ref[0, :, pl.ds(off, AT_TK)]
        tasks = []
        for mi, (s_ref, acc_ref) in enumerate(zip(s_refs, (acc1_ref, acc2_ref))):
            for gi, gs in enumerate(groups):
                def task(s_ref=s_ref, acc_ref=acc_ref, gs=gs, idx=mi * ng + gi):
                    m, l = stats[2 * idx], stats[2 * idx + 1]
                    m_new = jnp.maximum(m, mbs[idx])
                    alpha = jnp.exp2(m - m_new)
                    p = jnp.exp2(s_ref[:, gs] - m_new)
                    out[2 * idx] = m_new
                    out[2 * idx + 1] = alpha * l + jnp.sum(p, axis=0, keepdims=True)
                    acc_ref[:, gs] = alpha * acc_ref[:, gs] + _dot(vtblk, p.astype(BF16))
                tasks.append(task)
        return tasks

    def produce(i, qs, s_refs):
        mbs = [None] * (2 * ng)
        for task in produce_tasks(i, qs, s_refs, mbs):
            task()
        return tuple(mbs)

    def produce_and_consume(ip, qs, p_refs, ic, c_refs, mbs_c, stats):
        mbs_p = [None] * (2 * ng)
        out = [None] * (4 * ng)
        for pt, ct in zip(produce_tasks(ip, qs, p_refs, mbs_p), consume_tasks(ic, c_refs, mbs_c, stats, out)):
            pt()
            ct()
        return tuple(mbs_p), tuple(out)

    slot_a, slot_b = (sa1_ref, sa2_ref), (sb1_ref, sb2_ref)

    def save_maxima(mbs):
        for idx, mb in enumerate(mbs):
            mba_ref[idx // ng:idx // ng + 1, groups[idx % ng]] = mb

    @pl.when(qi == 0)
    def _():
        save_maxima(produce(0, q_cur, slot_a))

    def body(j, carry):
        mb_a, stats = carry[:2 * ng], carry[2 * ng:]
        mb_b, stats = produce_and_consume(2 * j + 1, q_cur, slot_b, 2 * j, slot_a, mb_a, stats)
        mb_a, stats = produce_and_consume(2 * j + 2, q_cur, slot_a, 2 * j + 1, slot_b, mb_b, stats)
        return (*mb_a, *stats)

    neg = jnp.full((1, AT_CG), -jnp.inf, F32)
    zer = jnp.zeros((1, AT_CG), F32)
    mb_a0 = tuple(mba_ref[idx // ng:idx // ng + 1, groups[idx % ng]] for idx in range(2 * ng))
    carry = lax.fori_loop(0, nblk // 2 - 1, body, (*mb_a0, *((neg, zer) * (2 * ng))))
    mb_a, stats = carry[:2 * ng], carry[2 * ng:]
    mb_b, stats = produce_and_consume(nblk - 1, q_cur, slot_b, nblk - 2, slot_a, mb_a, stats)
    mb_next, stats = produce_and_consume(0, split_maps(qn_ref[...]), slot_a, nblk - 1, slot_b, mb_b, stats)
    save_maxima(mb_next)
    l1 = jnp.concatenate([stats[2 * g + 1] for g in range(ng)], axis=1)
    l2 = jnp.concatenate([stats[2 * (ng + g) + 1] for g in range(ng)], axis=1)

    lam = (jnp.exp(jnp.sum(lq1_ref[...] * lk1_ref[...], axis=1, keepdims=True))
           - jnp.exp(jnp.sum(lq2_ref[...] * lk2_ref[...], axis=1, keepdims=True))
           + LAM_INIT)
    o = acc1_ref[...] * (1.0 / l1) - acc2_ref[...] * (lam / l2)
    ms = jnp.mean(o * o, axis=0, keepdims=True)
    y = o * lax.rsqrt(ms + EPS) * nw_ref[...] * (1.0 - LAM_INIT)
    o_ref[...] = y.T.astype(BF16)


def _attention(qk, kvt, lq1, lk1, lq2, lk2, norm_w, batch, seq):
    v_blk0 = (M_HEADS * M_DQK) // A_DV
    t_rows = qk.shape[0]
    nq = seq // AT_TQ
    small = pl.BlockSpec((1, A_DH), lambda b, h, i: (0, 0))
    return pl.pallas_call(
        _attn_kernel,
        name="attention",
        grid=(batch, A_HEADS, nq),
        in_specs=[
            pl.BlockSpec((AT_TQ, LANES), lambda b, h, i: (b * nq + i, h)),
            pl.BlockSpec((AT_TQ, LANES), lambda b, h, i: (b * nq + jnp.minimum(i + 1, nq - 1), h)),
            pl.BlockSpec((seq, LANES), lambda b, h, i: (b, A_HEADS + h)),
            pl.BlockSpec((1, A_DV, seq), lambda b, h, i: (b, v_blk0 + h, 0)),
            small, small, small, small,
            pl.BlockSpec((A_DV, 1), lambda b, h, i: (0, 0)),
        ],
        out_specs=pl.BlockSpec((AT_TQ, LANES), lambda b, h, i: (b * nq + i, h)),
        out_shape=jax.ShapeDtypeStruct((t_rows, A_WIDTH), BF16),
        scratch_shapes=([pltpu.VMEM((A_DV, AT_TQ), F32)] * 2 + [pltpu.VMEM((AT_TK, AT_TQ), F32)] * 4
                        + [pltpu.VMEM((SUBLANES, AT_TQ), F32)]),
        compiler_params=_cparams(("arbitrary", "arbitrary", "arbitrary")),
    )(qk, qk, qk, kvt, lq1, lk1, lq2, lk2, norm_w)


MG_TM = 512


def _merge_kernel(hf_ref, hb_ref, mo_ref, ha_ref, gm_ref, ga_ref, nw_ref, wm_ref, wa_ref, out_ref):
    hm = hf_ref[...] + hb_ref[...]
    parts = []
    for h in range(M_HEADS):
        seg = hm[:, h * M_DV:(h + 1) * M_DV]
        ms = jnp.mean(seg * seg, axis=-1, keepdims=True)
        parts.append(seg * lax.rsqrt(ms + EPS))
    hn = jnp.concatenate(parts, axis=1) * nw_ref[...]
    hn = (hn * mo_ref[...].astype(F32)).astype(BF16)
    branch_m = _dot(hn, wm_ref[...])
    branch_a = _dot(ha_ref[...], wa_ref[...])
    mixed = gm_ref[...].astype(F32) * branch_m + ga_ref[...].astype(F32) * branch_a
    out_ref[...] = mixed.astype(BF16)


def _merge(hf, hb, sig, ha, norm_w, w_m, w_a):
    t_rows = hf.shape[0]
    row = lambda i: (i, 0)
    const = lambda i: (0, 0)
    return pl.pallas_call(
        _merge_kernel,
        name="merge",
        grid=(t_rows // MG_TM,),
        in_specs=[
            pl.BlockSpec((MG_TM, M_WIDTH), row),
            pl.BlockSpec((MG_TM, M_WIDTH), row),
            pl.BlockSpec((MG_TM, M_WIDTH), lambda i: (i, N_BRANCH_GATES // M_WIDTH)),
            pl.BlockSpec((MG_TM, A_WIDTH), row),
            pl.BlockSpec((MG_TM, D_MODEL), lambda i: (i, 0)),
            pl.BlockSpec((MG_TM, D_MODEL), lambda i: (i, 1)),
            pl.BlockSpec((1, M_WIDTH), const),
            pl.BlockSpec((M_WIDTH, D_MODEL), const),
            pl.BlockSpec((A_WIDTH, D_MODEL), const),
        ],
        out_specs=pl.BlockSpec((MG_TM, D_MODEL), row),
        out_shape=jax.ShapeDtypeStruct((t_rows, D_MODEL), BF16),
        compiler_params=_cparams(("parallel",)),
    )(hf, hb, sig, ha, sig, sig, norm_w, w_m, w_a)


OP_TM = 512


def _outproj_kernel(mixed_ref, x_ref, w_ref, nw_ref, x1_ref, h2_ref):
    x1 = x_ref[...] + _dot(mixed_ref[...], w_ref[...])
    x1_ref[...] = x1
    ms = jnp.mean(x1 * x1, axis=-1, keepdims=True)
    h2_ref[...] = (x1 * lax.rsqrt(ms + EPS) * nw_ref[...]).astype(BF16)


def _outproj(mixed, x2, w_out, norm_w):
    t_rows = x2.shape[0]
    row = lambda i: (i, 0)
    const = lambda i: (0, 0)
    return pl.pallas_call(
        _outproj_kernel,
        name="outproj",
        grid=(t_rows // OP_TM,),
        in_specs=[
            pl.BlockSpec((OP_TM, D_MODEL), row),
            pl.BlockSpec((OP_TM, D_MODEL), row),
            pl.BlockSpec((D_MODEL, D_MODEL), const),
            pl.BlockSpec((1, D_MODEL), const),
        ],
        out_specs=[pl.BlockSpec((OP_TM, D_MODEL), row), pl.BlockSpec((OP_TM, D_MODEL), row)],
        out_shape=[jax.ShapeDtypeStruct((t_rows, D_MODEL), F32),
                   jax.ShapeDtypeStruct((t_rows, D_MODEL), BF16)],
        compiler_params=_cparams(("parallel",)),
    )(mixed, x2, w_out, norm_w)


FI_TM = 2048
FI_CHUNK = 512
FI_TN = 512


def _ffn_in_kernel(h_ref, wg_ref, wu_ref, out_ref):
    for r in range(FI_TM // FI_CHUNK):
        rows = slice(r * FI_CHUNK, (r + 1) * FI_CHUNK)
        h = h_ref[rows, :]
        gate = _dot(h, wg_ref[...])
        up = _dot(h, wu_ref[...])
        out_ref[rows, :] = (gate * _sigmoid(gate) * up).astype(BF16)


def _ffn_in(h2, w_ffn_in):
    t_rows = h2.shape[0]
    nj = D_FF // FI_TN
    return pl.pallas_call(
        _ffn_in_kernel,
        name="ffn_in",
        grid=(t_rows // FI_TM, nj),
        in_specs=[
            pl.BlockSpec((FI_TM, D_MODEL), lambda i, j: (i, 0)),
            pl.BlockSpec((D_MODEL, FI_TN), lambda i, j: (0, j)),
            pl.BlockSpec((D_MODEL, FI_TN), lambda i, j: (0, nj + j)),
        ],
        out_specs=pl.BlockSpec((FI_TM, FI_TN), lambda i, j: (i, j)),
        out_shape=jax.ShapeDtypeStruct((t_rows, D_FF), BF16),
        compiler_params=_cparams(("parallel", "arbitrary")),
    )(h2, w_ffn_in, w_ffn_in)


FO_TM = 1024
FO_TN = 1024
FO_KSPLIT = 2
FO_TK = D_FF // FO_KSPLIT


def _ffn_out_kernel(act_ref, w_ref, x1_ref, nw_ref, out_ref):
    k = pl.program_id(1)
    j = pl.program_id(2)
    cols = pl.ds(pl.multiple_of(j * FO_TN, FO_TN), FO_TN)
    for r in range(FO_TM // ROW_CHUNK):
        rows = slice(r * ROW_CHUNK, (r + 1) * ROW_CHUNK)
        base = jnp.where(k == 0, x1_ref[rows, :], out_ref[rows, cols])
        out_ref[rows, cols] = base + _dot(act_ref[rows, :], w_ref[...])

    @pl.when((k == FO_KSPLIT - 1) & (j == pl.num_programs(2) - 1))
    def _():
        x2 = out_ref[...]
        ms = jnp.mean(x2 * x2, axis=-1, keepdims=True)
        out_ref[...] = x2 * lax.rsqrt(ms + EPS) * nw_ref[...]


def _ffn_out(act, w_ffn_out, x1, norm_w):
    t_rows = x1.shape[0]
    nj = D_MODEL // FO_TN
    return pl.pallas_call(
        _ffn_out_kernel,
        name="ffn_out",
        grid=(t_rows // FO_TM, FO_KSPLIT, nj),
        in_specs=[
            pl.BlockSpec((FO_TM, FO_TK), lambda i, k, j: (i, k)),
            pl.BlockSpec((FO_TK, FO_TN), lambda i, k, j: (k, j)),
            pl.BlockSpec((FO_TM, FO_TN), lambda i, k, j: (i, jnp.where(k == 0, j, nj - 1))),
            pl.BlockSpec((1, D_MODEL), lambda i, k, j: (0, 0)),
        ],
        out_specs=pl.BlockSpec((FO_TM, D_MODEL), lambda i, k, j: (i, 0)),
        out_shape=jax.ShapeDtypeStruct((t_rows, D_MODEL), F32),
        compiler_params=_cparams(("parallel", "arbitrary", "arbitrary")),
    )(act, w_ffn_out, x1, norm_w)


def _rope_tables(seq):
    inv = ROPE_THETA ** (-jnp.arange(0, A_DH, 2, dtype=F32) / A_DH)
    ang = jnp.arange(seq, dtype=F32)[:, None] * inv[None, :]
    cos = jnp.cos(ang)
    sin = jnp.sin(ang)
    cos_t = jnp.concatenate([cos, cos, cos, cos], axis=1)
    sin_t = jnp.concatenate([-sin, -sin, sin, sin], axis=1)
    return cos_t, sin_t


def _rotary_layout(w_seg):
    d = w_seg.shape[0]
    half = A_DH // 2
    return w_seg.reshape(d, A_HEADS, 2, 2, half).transpose(0, 1, 3, 2, 4).reshape(d, A_WIDTH)


def kernel(x, norm1_w, w_in, b_igate, b_fgate, b_branch_gate, mlstm_norm_w, lam_q1, lam_k1, lam_q2, lam_k2, attn_norm_w, w_branch_m, w_branch_a, w_out, norm2_w, w_ffn_in, w_ffn_out, final_norm_w):
    batch, seq, d = x.shape
    depth = w_in.shape[0]
    assert d == D_MODEL and depth == 1
    assert seq % PJ_TM == 0 and seq % AT_TQ == 0 and seq % (2 * AT_TK) == 0 and seq % (MS_SUB * L) == 0
    assert (batch * seq) % FI_TM == 0 and (batch * seq) % FO_TM == 0
    t_rows = batch * seq
    x2 = x.reshape(t_rows, d)
    cos_t, sin_t = _rope_tables(seq)

    l = 0
    w = w_in[l].astype(BF16)
    qk_w = M_HEADS * M_DQK
    w_vq = jnp.concatenate([w[:, 2 * qk_w:OFF_MO], w[:, OFF_MQ:qk_w]], axis=1)
    w_kvt = jnp.concatenate([w[:, qk_w:2 * qk_w], w[:, OFF_AV:OFF_GT]], axis=1)
    w_sig = jnp.concatenate([w[:, OFF_GT:OFF_GT + N_BRANCH_GATES], w[:, OFF_MO:OFF_MG]], axis=1)
    w_rot = jnp.concatenate([_rotary_layout(w[:, OFF_AQ:OFF_AK]), _rotary_layout(w[:, OFF_AK:OFF_AV])], axis=1)
    w_gate = jnp.pad(w[:, OFF_MG:OFF_AQ], ((0, 0), (0, LANES - N_GATE)))

    scale_vq = jnp.ones((1, M_WIDTH + qk_w), F32)
    scale_kvt = jnp.concatenate([jnp.full((1, qk_w), M_DQK ** -0.5, F32), jnp.ones((1, A_WIDTH), F32)], axis=1)
    scale_rot = jnp.concatenate([jnp.full((1, A_WIDTH), Q_SCALE, F32), jnp.ones((1, A_WIDTH), F32)], axis=1)
    bias_sig = jnp.concatenate([b_branch_gate[l].astype(F32), jnp.zeros((M_WIDTH,), F32)]).reshape(1, -1)
    gate_bias = jnp.stack([b_igate[l], b_fgate[l]], axis=1).reshape(N_GATE).astype(F32)
    gate_bias = jnp.pad(gate_bias, (0, LANES - N_GATE))

    hn, gates = _rmsnorm(x2, norm1_w[l].reshape(1, d), w_gate)
    vq = _proj("scale", hn, w_vq, [scale_vq], seq, tn=PJ_TN_NARROW)
    sig = _proj("sigmoid", hn, w_sig, [bias_sig], seq)
    qk = _proj("rope", hn, w_rot, [cos_t, sin_t, scale_rot], seq)
    kvt = _proj("transpose", hn, w_kvt, [scale_kvt], seq, tn=PJ_TN_NARROW)

    hf, hb = _mlstm(vq, kvt, gates, gate_bias.reshape(1, LANES), gate_bias.reshape(LANES, 1), batch, seq)
    ha = _attention(qk, kvt, lam_q1[l].reshape(1, A_DH), lam_k1[l].reshape(1, A_DH),
                    lam_q2[l].reshape(1, A_DH), lam_k2[l].reshape(1, A_DH),
                    attn_norm_w[l].reshape(A_DV, 1), batch, seq)
    mixed = _merge(hf, hb, sig, ha, mlstm_norm_w[l].reshape(1, M_WIDTH),
                   w_branch_m[l].astype(BF16), w_branch_a[l].astype(BF16))
    x1, h2 = _outproj(mixed, x2, w_out[l].astype(BF16), norm2_w[l].reshape(1, d))
    act = _ffn_in(h2, w_ffn_in[l].astype(BF16))
    out = _ffn_out(act, w_ffn_out[l].astype(BF16), x1, final_norm_w.reshape(1, d))
    return out.reshape(batch, seq, d)
```

```python
import functools
import math

import jax
import jax.numpy as jnp
from jax import lax
from jax.experimental import pallas as pl
from jax.experimental.pallas import tpu as pltpu

F32 = jnp.float32
BF16 = jnp.bfloat16

D_MODEL = 2048
M_HEADS = 4
M_DQK = 128
M_DV = 256
M_CHUNK = 128
GATE_CAP = 15.0
A_HEADS = 8
A_DH = 64
A_DV = 2 * A_DH
ROPE_THETA = 10000.0
D_FF = 5632
EPS = 1e-6
M_WIDTH = M_HEADS * M_DV
A_WIDTH = A_HEADS * A_DV
N_BRANCH_GATES = 2 * D_MODEL
LAM_INIT = 0.8 - 0.6 * math.exp(-0.3 * 0)

OFF_MQ = 0
OFF_MO = 2 * M_HEADS * M_DQK + M_WIDTH
OFF_MG = OFF_MO + M_WIDTH
N_GATE = 4 * M_HEADS
OFF_AQ = OFF_MG + N_GATE
OFF_AK = OFF_AQ + A_WIDTH
OFF_AV = OFF_AK + A_WIDTH
OFF_GT = OFF_AV + A_WIDTH
LANES = 128
SUBLANES = 8

V7X_VMEM_BYTES = 64 * 1024 * 1024
VMEM_LIMIT = V7X_VMEM_BYTES * 7 // 8


def _cparams(sem):
    return pltpu.CompilerParams(dimension_semantics=sem, vmem_limit_bytes=VMEM_LIMIT)


def _dot(a, b):
    return jnp.dot(a, b, preferred_element_type=F32)


def _dot_nt(a, b):
    return lax.dot_general(a, b, (((1,), (1,)), ((), ())), preferred_element_type=F32)


def _sigmoid(x):
    return 0.5 * jnp.tanh(0.5 * x) + 0.5


NORM_TM = 512
PJ_TM = 2048
PJ_TN = 1024
PJ_TN_NARROW = 768
ROW_CHUNK = 256
Q_SCALE = (A_DH ** -0.5) * math.log2(math.e)


def _rmsnorm_kernel(x_ref, w_ref, wg_ref, o_ref, g_ref):
    x = x_ref[...]
    ms = jnp.mean(x * x, axis=-1, keepdims=True)
    hn = (x * lax.rsqrt(ms + EPS) * w_ref[...]).astype(BF16)
    o_ref[...] = hn
    g_ref[...] = _dot(hn, wg_ref[...])


def _rmsnorm(x2, norm_w, w_gate):
    t_rows, d = x2.shape
    return pl.pallas_call(
        _rmsnorm_kernel,
        name="rmsnorm",
        grid=(t_rows // NORM_TM,),
        in_specs=[pl.BlockSpec((NORM_TM, d), lambda i: (i, 0)),
                  pl.BlockSpec((1, d), lambda i: (0, 0)),
                  pl.BlockSpec((d, LANES), lambda i: (0, 0))],
        out_specs=[pl.BlockSpec((NORM_TM, d), lambda i: (i, 0)),
                   pl.BlockSpec((NORM_TM, LANES), lambda i: (i, 0))],
        out_shape=[jax.ShapeDtypeStruct((t_rows, d), BF16),
                   jax.ShapeDtypeStruct((t_rows, LANES), F32)],
        compiler_params=_cparams(("parallel",)),
    )(x2, norm_w, w_gate)


def _rope(acc, cos, sin_signed):
    outs = []
    for c in range(acc.shape[1] // LANES):
        t = acc[:, c * LANES:(c + 1) * LANES]
        outs.append(t * cos + pltpu.roll(t, LANES // 2, axis=1) * sin_signed)
    return jnp.concatenate(outs, axis=1)


def _proj_kernel(mode, h_ref, w_ref, *refs):
    o_ref = refs[-1]
    for r in range(PJ_TM // ROW_CHUNK):
        rows = slice(r * ROW_CHUNK, (r + 1) * ROW_CHUNK)
        acc = _dot(h_ref[rows, :], w_ref[...])
        if mode == "scale":
            o_ref[rows, :] = (acc * refs[0][...]).astype(BF16)
        elif mode == "sigmoid":
            o_ref[rows, :] = _sigmoid(acc + refs[0][...]).astype(BF16)
        elif mode == "rope":
            cos_ref, sin_ref, cs_ref = refs[:3]
            o_ref[rows, :] = (_rope(acc, cos_ref[rows, :], sin_ref[rows, :]) * cs_ref[...]).astype(BF16)
        else:
            assert mode == "transpose"
            o_ref[0, :, rows] = (acc * refs[0][...]).T.astype(BF16)


def _proj(mode, hn, w, aux, seq, tn=PJ_TN):
    t_rows, d = hn.shape
    n = w.shape[1]
    tn = min(tn, n)
    s_blocks = seq // PJ_TM
    col = pl.BlockSpec((1, tn), lambda i, j: (0, j))
    pos = pl.BlockSpec((PJ_TM, LANES), lambda i, j: (i % s_blocks, 0))
    aux_specs = {"scale": [col], "sigmoid": [col], "rope": [pos, pos, col], "transpose": [col]}[mode]
    if mode == "transpose":
        out_spec = pl.BlockSpec((1, tn, PJ_TM), lambda i, j: (i // s_blocks, j, i % s_blocks))
        out_shape = jax.ShapeDtypeStruct((t_rows // seq, n, seq), BF16)
    else:
        out_spec = pl.BlockSpec((PJ_TM, tn), lambda i, j: (i, j))
        out_shape = jax.ShapeDtypeStruct((t_rows, n), BF16)
    return pl.pallas_call(
        functools.partial(_proj_kernel, mode),
        name="proj_" + mode,
        grid=(t_rows // PJ_TM, n // tn),
        in_specs=[pl.BlockSpec((PJ_TM, d), lambda i, j: (i, 0)),
                  pl.BlockSpec((d, tn), lambda i, j: (0, j))] + aux_specs,
        out_specs=out_spec,
        out_shape=out_shape,
        compiler_params=_cparams(("parallel", "arbitrary")),
    )(hn, w, *aux)


L = M_CHUNK
MS_SUB = 4
DV_EXT = M_DV + LANES


def _softcap(t):
    return GATE_CAP * jnp.tanh(t / GATE_CAP)


def _log_sigmoid(t):
    return jnp.minimum(t, 0.0) - jnp.log(1.0 + jnp.exp(-jnp.abs(t)))


def _gate_act(pre, is_forget):
    c = _softcap(pre)
    return jnp.where(is_forget, _log_sigmoid(c), c)


def _split_dot(a, b, a_is_exact):
    if a_is_exact:
        hi = b.astype(BF16)
        lo = (b - hi.astype(F32)).astype(BF16)
        ab = a.astype(BF16)
        return _dot(ab, hi) + _dot(ab, lo)
    hi = a.astype(BF16)
    lo = (a - hi.astype(F32)).astype(BF16)
    bb = b.astype(BF16)
    return _dot(hi, bb) + _dot(lo, bb)


def _mlstm_kernel(qf_ref, kf_ref, vf_ref, gf_ref, qb_ref, kb_ref, vb_ref, gb_ref,
                  brow_ref, bcol_ref, hf_ref, hb_ref, c_ref, m_ref):
    step = pl.program_id(1)

    @pl.when(step == 0)
    def _():
        c_ref[...] = jnp.zeros_like(c_ref)
        m_ref[...] = jnp.zeros_like(m_ref)

    row = lax.broadcasted_iota(jnp.int32, (L, L), 0)
    col = lax.broadcasted_iota(jnp.int32, (L, L), 1)
    lane_id = lax.broadcasted_iota(jnp.int32, (1, LANES), 1)
    sub_id = lax.broadcasted_iota(jnp.int32, (LANES, 1), 0)
    forget_lane = (lane_id % (2 * M_HEADS)) >= M_HEADS
    forget_sub = (sub_id % (2 * M_HEADS)) >= M_HEADS
    ones_ext = jnp.ones((L, LANES), BF16)

    dirs = ((qf_ref, kf_ref, vf_ref, gf_ref, hf_ref), (qb_ref, kb_ref, vb_ref, gb_ref, hb_ref))
    for sub, d in [(sub, d) for sub in range(MS_SUB) for d in range(2)]:
        q_blk, k_blk, v_blk, g_blk, h_blk = dirs[d]
        r0 = (sub if d == 0 else MS_SUB - 1 - sub) * L
        q_ref, v_ref, g_ref, h_ref = (ref.at[r0:r0 + L, :] for ref in (q_blk, v_blk, g_blk, h_blk))
        kt_ref = k_blk.at[0, :, r0:r0 + L]
        visible = (row >= col) if d == 0 else (col >= row)
        vis_f = visible.astype(F32)

        g = g_ref[...]
        g_t = g.T
        act_c = _gate_act(g + brow_ref[...], forget_lane)
        act_r = _gate_act(g_t + bcol_ref[...], forget_sub)
        cum_c = _split_dot(vis_f, act_c, True)
        cum_r = _split_dot(act_r, vis_f.T, False)

        for h in range(M_HEADS):
            idx = d * M_HEADS + h
            ci = d * 2 * M_HEADS + h
            cf = ci + M_HEADS
            bc = cum_c[:, cf:cf + 1]
            br = cum_r[cf:cf + 1, :]
            igr = act_r[ci:ci + 1, :]
            b_last = br[:, L - 1:L] if d == 0 else br[:, 0:1]
            m_old = m_ref[idx][0:1, 0:1]

            q = q_ref[:, h * M_DQK:(h + 1) * M_DQK]
            kt = kt_ref[h * M_DQK:(h + 1) * M_DQK, :]
            v_ext = jnp.concatenate([v_ref[:, h * M_DV:(h + 1) * M_DV], ones_ext], axis=1)

            dmat = jnp.where(visible, bc - br + igr, -jnp.inf)
            m_loc = jnp.max(dmat, axis=1, keepdims=True)
            s = _dot(q, kt) * jnp.exp(dmat - m_loc)
            sv = _dot(s.astype(BF16), v_ext)
            g_row = b_last - br + igr
            mg = jnp.max(g_row, axis=1, keepdims=True)
            kw_t = (kt.astype(F32) * jnp.exp(g_row - mg)).astype(BF16)
            u = _dot(kw_t, v_ext)

            c_old = c_ref[idx]
            inter = bc + m_old
            m_t = jnp.maximum(inter, m_loc)
            comb = jnp.exp(inter - m_t) * _dot(q, c_old.astype(BF16)) + jnp.exp(m_loc - m_t) * sv
            num = comb[:, :M_DV]
            den = comb[:, M_DV:M_DV + 1]
            hval = num / jnp.maximum(jnp.abs(den), jnp.exp(-m_t))
            h_ref[:, h * M_DV:(h + 1) * M_DV] = hval

            m_new = jnp.maximum(b_last + m_old, mg)
            c_ref[idx] = jnp.exp(b_last + m_old - m_new) * c_old + jnp.exp(mg - m_new) * u
            m_ref[idx] = jnp.broadcast_to(m_new, (SUBLANES, LANES))


def _mlstm(vq, kvt, gates, bias_row, bias_col, batch, seq):
    t_rows = vq.shape[0]
    rows = MS_SUB * L
    nc = seq // rows
    fwd = lambda b, c: b * nc + c
    bwd = lambda b, c: b * nc + (nc - 1 - c)
    qk_w = M_HEADS * M_DQK
    in_specs = []
    for ch in (fwd, bwd):
        in_specs += [
            pl.BlockSpec((rows, qk_w), lambda b, c, ch=ch: (ch(b, c), M_WIDTH // qk_w)),
            pl.BlockSpec((1, qk_w, rows), lambda b, c, ch=ch: (b, 0, ch(0, c))),
            pl.BlockSpec((rows, M_WIDTH), lambda b, c, ch=ch: (ch(b, c), 0)),
            pl.BlockSpec((rows, LANES), lambda b, c, ch=ch: (ch(b, c), 0)),
        ]
    in_specs += [pl.BlockSpec((1, LANES), lambda b, c: (0, 0)),
                 pl.BlockSpec((LANES, 1), lambda b, c: (0, 0))]
    return pl.pallas_call(
        _mlstm_kernel,
        name="mlstm",
        grid=(batch, nc),
        in_specs=in_specs,
        out_specs=[pl.BlockSpec((rows, M_WIDTH), lambda b, c: (fwd(b, c), 0)),
                   pl.BlockSpec((rows, M_WIDTH), lambda b, c: (bwd(b, c), 0))],
        out_shape=[jax.ShapeDtypeStruct((t_rows, M_WIDTH), F32)] * 2,
        scratch_shapes=[pltpu.VMEM((2 * M_HEADS, M_DQK, DV_EXT), F32),
                        pltpu.VMEM((2 * M_HEADS, SUBLANES, LANES), F32)],
        compiler_params=_cparams(("parallel", "arbitrary")),
    )(vq, kvt, vq, gates, vq, kvt, vq, gates, bias_row, bias_col)


AT_TQ = 2048
AT_TK = 1024
AT_CG = 256


def _attn_kernel(q_ref, qn_ref, k_ref, vt_ref, lq1_ref, lk1_ref, lq2_ref, lk2_ref, nw_ref,
                 o_ref, acc1_ref, acc2_ref, sa1_ref, sa2_ref, sb1_ref, sb2_ref, mba_ref):
    seq = k_ref.shape[0]
    nblk = seq // AT_TK
    qi = pl.program_id(2)
    lane = lax.broadcasted_iota(jnp.int32, (1, LANES), 1)
    in_map1 = (lane % A_DH) < (A_DH // 2)

    def split_maps(q):
        zero = jnp.zeros_like(q)
        return jnp.where(in_map1, q, zero), jnp.where(in_map1, zero, q)

    q_cur = split_maps(q_ref[...])
    acc1_ref[...] = jnp.zeros_like(acc1_ref)
    acc2_ref[...] = jnp.zeros_like(acc2_ref)

    groups = [slice(g * AT_CG, (g + 1) * AT_CG) for g in range(AT_TQ // AT_CG)]
    ng = len(groups)

    def produce_tasks(i, qs, s_refs, mbs):
        off = pl.multiple_of(i * AT_TK, AT_TK)
        kblk = k_ref[pl.ds(off, AT_TK), :]
        tasks = []
        for mi, (qm, s_ref) in enumerate(zip(qs, s_refs)):
            for gi, gs in enumerate(groups):
                def task(qm=qm, s_ref=s_ref, gs=gs, idx=mi * ng + gi):
                    s = _dot_nt(kblk, qm[gs, :])
                    s_ref[:, gs] = s
                    mbs[idx] = jnp.max(s, axis=0, keepdims=True)
                tasks.append(task)
        return tasks

    def consume_tasks(i, s_refs, mbs, stats, out):
        off = pl.multiple_of(i * AT_TK, AT_TK)
        vtblk = vt_ref[0, :, pl.ds(off, AT_TK)]
        tasks = []
        for mi, (s_ref, acc_ref) in enumerate(zip(s_refs, (acc1_ref, acc2_ref))):
            for gi, gs in enumerate(groups):
                def task(s_ref=s_ref, acc_ref=acc_ref, gs=gs, idx=mi * ng + gi):
                    m, l = stats[2 * idx], stats[2 * idx + 1]
                    m_new = jnp.maximum(m, mbs[idx])
                    alpha = jnp.exp2(m - m_new)
                    p = jnp.exp2(s_ref[:, gs] - m_new)
                    out[2 * idx] = m_new
                    out[2 * idx + 1] = alpha * l + jnp.sum(p, axis=0, keepdims=True)
                    acc_ref[:, gs] = alpha * acc_ref[:, gs] + _dot(vtblk, p.astype(BF16))
                tasks.append(task)
        return tasks

    def produce(i, qs, s_refs):
        mbs = [None] * (2 * ng)
        for task in produce_tasks(i, qs, s_refs, mbs):
            task()
        return tuple(mbs)

    def produce_and_consume(ip, qs, p_refs, ic, c_refs, mbs_c, stats):
        mbs_p = [None] * (2 * ng)
        out = [None] * (4 * ng)
        for pt, ct in zip(produce_tasks(ip, qs, p_refs, mbs_p), consume_tasks(ic, c_refs, mbs_c, stats, out)):
            pt()
            ct()
        return tuple(mbs_p), tuple(out)

    slot_a, slot_b = (sa1_ref, sa2_ref), (sb1_ref, sb2_ref)

    def save_maxima(mbs):
        for idx, mb in enumerate(mbs):
            mba_ref[idx // ng:idx // ng + 1, groups[idx % ng]] = mb

    @pl.when(qi == 0)
    def _():
        save_maxima(produce(0, q_cur, slot_a))

    def body(j, carry):
        mb_a, stats = carry[:2 * ng], carry[2 * ng:]
        mb_b, stats = produce_and_consume(2 * j + 1, q_cur, slot_b, 2 * j, slot_a, mb_a, stats)
        mb_a, stats = produce_and_consume(2 * j + 2, q_cur, slot_a, 2 * j + 1, slot_b, mb_b, stats)
        return (*mb_a, *stats)

    neg = jnp.full((1, AT_CG), -jnp.inf, F32)
    zer = jnp.zeros((1, AT_CG), F32)
    mb_a0 = tuple(mba_ref[idx // ng:idx // ng + 1, groups[idx % ng]] for idx in range(2 * ng))
    carry = lax.fori_loop(0, nblk // 2 - 1, body, (*mb_a0, *((neg, zer) * (2 * ng))))
    mb_a, stats = carry[:2 * ng], carry[2 * ng:]
    mb_b, stats = produce_and_consume(nblk - 1, q_cur, slot_b, nblk - 2, slot_a, mb_a, stats)
    mb_next, stats = produce_and_consume(0, split_maps(qn_ref[...]), slot_a, nblk - 1, slot_b, mb_b, stats)
    save_maxima(mb_next)
    l1 = jnp.concatenate([stats[2 * g + 1] for g in range(ng)], axis=1)
    l2 = jnp.concatenate([stats[2 * (ng + g) + 1] for g in range(ng)], axis=1)

    lam = (jnp.exp(jnp.sum(lq1_ref[...] * lk1_ref[...], axis=1, keepdims=True))
           - jnp.exp(jnp.sum(lq2_ref[...] * lk2_ref[...], axis=1, keepdims=True))
           + LAM_INIT)
    o = acc1_ref[...] * (1.0 / l1) - acc2_ref[...] * (lam / l2)
    ms = jnp.mean(o * o, axis=0, keepdims=True)
    y = o * lax.rsqrt(ms + EPS) * nw_ref[...] * (1.0 - LAM_INIT)
    o_ref[...] = y.T.astype(BF16)


def _attention(qk, kvt, lq1, lk1, lq2, lk2, norm_w, batch, seq):
    v_blk0 = (M_HEADS * M_DQK) // A_DV
    t_rows = qk.shape[0]
    nq = seq // AT_TQ
    small = pl.BlockSpec((1, A_DH), lambda b, h, i: (0, 0))
    return pl.pallas_call(
        _attn_kernel,
        name="attention",
        grid=(batch, A_HEADS, nq),
        in_specs=[
            pl.BlockSpec((AT_TQ, LANES), lambda b, h, i: (b * nq + i, h)),
            pl.BlockSpec((AT_TQ, LANES), lambda b, h, i: (b * nq + jnp.minimum(i + 1, nq - 1), h)),
            pl.BlockSpec((seq, LANES), lambda b, h, i: (b, A_HEADS + h)),
            pl.BlockSpec((1, A_DV, seq), lambda b, h, i: (b, v_blk0 + h, 0)),
            small, small, small, small,
            pl.BlockSpec((A_DV, 1), lambda b, h, i: (0, 0)),
        ],
        out_specs=pl.BlockSpec((AT_TQ, LANES), lambda b, h, i: (b * nq + i, h)),
        out_shape=jax.ShapeDtypeStruct((t_rows, A_WIDTH), BF16),
        scratch_shapes=([pltpu.VMEM((A_DV, AT_TQ), F32)] * 2 + [pltpu.VMEM((AT_TK, AT_TQ), F32)] * 4
                        + [pltpu.VMEM((SUBLANES, AT_TQ), F32)]),
        compiler_params=_cparams(("arbitrary", "arbitrary", "arbitrary")),
    )(qk, qk, qk, kvt, lq1, lk1, lq2, lk2, norm_w)


MG_TM = 512


def _merge_kernel(hf_ref, hb_ref, mo_ref, ha_ref, gm_ref, ga_ref, nw_ref, wm_ref, wa_ref, out_ref):
    hm = hf_ref[...] + hb_ref[...]
    parts = []
    for h in range(M_HEADS):
        seg = hm[:, h * M_DV:(h + 1) * M_DV]
        ms = jnp.mean(seg * seg, axis=-1, keepdims=True)
        parts.append(seg * lax.rsqrt(ms + EPS))
    hn = jnp.concatenate(parts, axis=1) * nw_ref[...]
    hn = (hn * mo_ref[...].astype(F32)).astype(BF16)
    branch_m = _dot(hn, wm_ref[...])
    branch_a = _dot(ha_ref[...], wa_ref[...])
    mixed = gm_ref[...].astype(F32) * branch_m + ga_ref[...].astype(F32) * branch_a
    out_ref[...] = mixed.astype(BF16)


def _merge(hf, hb, sig, ha, norm_w, w_m, w_a):
    t_rows = hf.shape[0]
    row = lambda i: (i, 0)
    const = lambda i: (0, 0)
    return pl.pallas_call(
        _merge_kernel,
        name="merge",
        grid=(t_rows // MG_TM,),
        in_specs=[
            pl.BlockSpec((MG_TM, M_WIDTH), row),
            pl.BlockSpec((MG_TM, M_WIDTH), row),
            pl.BlockSpec((MG_TM, M_WIDTH), lambda i: (i, N_BRANCH_GATES // M_WIDTH)),
            pl.BlockSpec((MG_TM, A_WIDTH), row),
            pl.BlockSpec((MG_TM, D_MODEL), lambda i: (i, 0)),
            pl.BlockSpec((MG_TM, D_MODEL), lambda i: (i, 1)),
            pl.BlockSpec((1, M_WIDTH), const),
            pl.BlockSpec((M_WIDTH, D_MODEL), const),
            pl.BlockSpec((A_WIDTH, D_MODEL), const),
        ],
        out_specs=pl.BlockSpec((MG_TM, D_MODEL), row),
        out_shape=jax.ShapeDtypeStruct((t_rows, D_MODEL), BF16),
        compiler_params=_cparams(("parallel",)),
    )(hf, hb, sig, ha, sig, sig, norm_w, w_m, w_a)


OP_TM = 512


def _outproj_kernel(mixed_ref, x_ref, w_ref, nw_ref, x1_ref, h2_ref):
    x1 = x_ref[...] + _dot(mixed_ref[...], w_ref[...])
    x1_ref[...] = x1
    ms = jnp.mean(x1 * x1, axis=-1, keepdims=True)
    h2_ref[...] = (x1 * lax.rsqrt(ms + EPS) * nw_ref[...]).astype(BF16)


def _outproj(mixed, x2, w_out, norm_w):
    t_rows = x2.shape[0]
    row = lambda i: (i, 0)
    const = lambda i: (0, 0)
    return pl.pallas_call(
        _outproj_kernel,
        name="outproj",
        grid=(t_rows // OP_TM,),
        in_specs=[
            pl.BlockSpec((OP_TM, D_MODEL), row),
            pl.BlockSpec((OP_TM, D_MODEL), row),
            pl.BlockSpec((D_MODEL, D_MODEL), const),
            pl.BlockSpec((1, D_MODEL), const),
        ],
        out_specs=[pl.BlockSpec((OP_TM, D_MODEL), row), pl.BlockSpec((OP_TM, D_MODEL), row)],
        out_shape=[jax.ShapeDtypeStruct((t_rows, D_MODEL), F32),
                   jax.ShapeDtypeStruct((t_rows, D_MODEL), BF16)],
        compiler_params=_cparams(("parallel",)),
    )(mixed, x2, w_out, norm_w)


FI_TM = 2048
FI_CHUNK = 512
FI_TN = 512


def _ffn_in_kernel(h_ref, wg_ref, wu_ref, out_ref):
    for r in range(FI_TM // FI_CHUNK):
        rows = slice(r * FI_CHUNK, (r + 1) * FI_CHUNK)
        h = h_ref[rows, :]
        gate = _dot(h, wg_ref[...])
        up = _dot(h, wu_ref[...])
        out_ref[rows, :] = (gate * _sigmoid(gate) * up).astype(BF16)


def _ffn_in(h2, w_ffn_in):
    t_rows = h2.shape[0]
    nj = D_FF // FI_TN
    return pl.pallas_call(
        _ffn_in_kernel,
        name="ffn_in",
        grid=(t_rows // FI_TM, nj),
        in_specs=[
            pl.BlockSpec((FI_TM, D_MODEL), lambda i, j: (i, 0)),
            pl.BlockSpec((D_MODEL, FI_TN), lambda i, j: (0, j)),
            pl.BlockSpec((D_MODEL, FI_TN), lambda i, j: (0, nj + j)),
        ],
        out_specs=pl.BlockSpec((FI_TM, FI_TN), lambda i, j: (i, j)),
        out_shape=jax.ShapeDtypeStruct((t_rows, D_FF), BF16),
        compiler_params=_cparams(("parallel", "arbitrary")),
    )(h2, w_ffn_in, w_ffn_in)


FO_TM = 1024
FO_TN = 1024
FO_KSPLIT = 2
FO_TK = D_FF // FO_KSPLIT


def _ffn_out_kernel(act_ref, w_ref, x1_ref, nw_ref, out_ref):
    k = pl.program_id(1)
    j = pl.program_id(2)
    cols = pl.ds(pl.multiple_of(j * FO_TN, FO_TN), FO_TN)
    for r in range(FO_TM // ROW_CHUNK):
        rows = slice(r * ROW_CHUNK, (r + 1) * ROW_CHUNK)
        base = jnp.where(k == 0, x1_ref[rows, :], out_ref[rows, cols])
        out_ref[rows, cols] = base + _dot(act_ref[rows, :], w_ref[...])

    @pl.when((k == FO_KSPLIT - 1) & (j == pl.num_programs(2) - 1))
    def _():
        x2 = out_ref[...]
        ms = jnp.mean(x2 * x2, axis=-1, keepdims=True)
        out_ref[...] = x2 * lax.rsqrt(ms + EPS) * nw_ref[...]


def _ffn_out(act, w_ffn_out, x1, norm_w):
    t_rows = x1.shape[0]
    nj = D_MODEL // FO_TN
    return pl.pallas_call(
        _ffn_out_kernel,
        name="ffn_out",
        grid=(t_rows // FO_TM, FO_KSPLIT, nj),
        in_specs=[
            pl.BlockSpec((FO_TM, FO_TK), lambda i, k, j: (i, k)),
            pl.BlockSpec((FO_TK, FO_TN), lambda i, k, j: (k, j)),
            pl.BlockSpec((FO_TM, FO_TN), lambda i, k, j: (i, jnp.where(k == 0, j, nj - 1))),
            pl.BlockSpec((1, D_MODEL), lambda i, k, j: (0, 0)),
        ],
        out_specs=pl.BlockSpec((FO_TM, D_MODEL), lambda i, k, j: (i, 0)),
        out_shape=jax.ShapeDtypeStruct((t_rows, D_MODEL), F32),
        compiler_params=_cparams(("parallel", "arbitrary", "arbitrary")),
    )(act, w_ffn_out, x1, norm_w)


def _rope_tables(seq):
    inv = ROPE_THETA ** (-jnp.arange(0, A_DH, 2, dtype=F32) / A_DH)
    ang = jnp.arange(seq, dtype=F32)[:, None] * inv[None, :]
    cos = jnp.cos(ang)
    sin = jnp.sin(ang)
    cos_t = jnp.concatenate([cos, cos, cos, cos], axis=1)
    sin_t = jnp.concatenate([-sin, -sin, sin, sin], axis=1)
    return cos_t, sin_t


def _rotary_layout(w_seg):
    d = w_seg.shape[0]
    half = A_DH // 2
    return w_seg.reshape(d, A_HEADS, 2, 2, half).transpose(0, 1, 3, 2, 4).reshape(d, A_WIDTH)


def kernel(x, norm1_w, w_in, b_igate, b_fgate, b_branch_gate, mlstm_norm_w, lam_q1, lam_k1, lam_q2, lam_k2, attn_norm_w, w_branch_m, w_branch_a, w_out, norm2_w, w_ffn_in, w_ffn_out, final_norm_w):
    batch, seq, d = x.shape
    depth = w_in.shape[0]
    assert d == D_MODEL and depth == 1
    assert seq % PJ_TM == 0 and seq % AT_TQ == 0 and seq % (2 * AT_TK) == 0 and seq % (MS_SUB * L) == 0
    assert (batch * seq) % FI_TM == 0 and (batch * seq) % FO_TM == 0
    t_rows = batch * seq
    x2 = x.reshape(t_rows, d)
    cos_t, sin_t = _rope_tables(seq)

    l = 0
    w = w_in[l].astype(BF16)
    qk_w = M_HEADS * M_DQK
    w_vq = jnp.concatenate([w[:, 2 * qk_w:OFF_MO], w[:, OFF_MQ:qk_w]], axis=1)
    w_kvt = jnp.concatenate([w[:, qk_w:2 * qk_w], w[:, OFF_AV:OFF_GT]], axis=1)
    w_sig = jnp.concatenate([w[:, OFF_GT:OFF_GT + N_BRANCH_GATES], w[:, OFF_MO:OFF_MG]], axis=1)
    w_rot = jnp.concatenate([_rotary_layout(w[:, OFF_AQ:OFF_AK]), _rotary_layout(w[:, OFF_AK:OFF_AV])], axis=1)
    w_gate = jnp.pad(w[:, OFF_MG:OFF_AQ], ((0, 0), (0, LANES - N_GATE)))

    scale_vq = jnp.ones((1, M_WIDTH + qk_w), F32)
    scale_kvt = jnp.concatenate([jnp.full((1, qk_w), M_DQK ** -0.5, F32), jnp.ones((1, A_WIDTH), F32)], axis=1)
    scale_rot = jnp.concatenate([jnp.full((1, A_WIDTH), Q_SCALE, F32), jnp.ones((1, A_WIDTH), F32)], axis=1)
    bias_sig = jnp.concatenate([b_branch_gate[l].astype(F32), jnp.zeros((M_WIDTH,), F32)]).reshape(1, -1)
    gate_bias = jnp.stack([b_igate[l], b_fgate[l]], axis=1).reshape(N_GATE).astype(F32)
    gate_bias = jnp.pad(gate_bias, (0, LANES - N_GATE))

    hn, gates = _rmsnorm(x2, norm1_w[l].reshape(1, d), w_gate)
    vq = _proj("scale", hn, w_vq, [scale_vq], seq, tn=PJ_TN_NARROW)
    sig = _proj("sigmoid", hn, w_sig, [bias_sig], seq)
    qk = _proj("rope", hn, w_rot, [cos_t, sin_t, scale_rot], seq)
    kvt = _proj("transpose", hn, w_kvt, [scale_kvt], seq, tn=PJ_TN_NARROW)

    hf, hb = _mlstm(vq, kvt, gates, gate_bias.reshape(1, LANES), gate_bias.reshape(LANES, 1), batch, seq)
    ha = _attention(qk, kvt, lam_q1[l].reshape(1, A_DH), lam_k1[l].reshape(1, A_DH),
                    lam_q2[l].reshape(1, A_DH), lam_k2[l].reshape(1, A_DH),
                    attn_norm_w[l].reshape(A_DV, 1), batch, seq)
    mixed = _merge(hf, hb, sig, ha, mlstm_norm_w[l].reshape(1, M_WIDTH),
                   w_branch_m[l].astype(BF16), w_branch_a[l].astype(BF16))
    x1, h2 = _outproj(mixed, x2, w_out[l].astype(BF16), norm2_w[l].reshape(1, d))
    act = _ffn_in(h2, w_ffn_in[l].astype(BF16))
    out = _ffn_out(act, w_ffn_out[l].astype(BF16), x1, final_norm_w.reshape(1, d))
    return out.reshape(batch, seq, d)
```
